```python
import math
import jax, jax.numpy as jnp
from jax import lax
import numpy as np

D_MODEL = 1024
BATCH = 16
SEQ = 2048
DEPTH = 1
DEC_BATCH = 32
DEC_SEQ = 1
PAST_LEN = 16384
PAGE_SIZE = 128

HEAD_DIM = 64
N_HEADS = 8
ATTN_W = N_HEADS * HEAD_DIM
N_IDX_HEADS = 8
IDX_DIM = 64
TOPK_MAX = 256
N_MEM = 256
N_MEM_HEADS = 4
MEM_W = N_MEM_HEADS * HEAD_DIM
CONV_W = 256
CONV_K = 3
N_BRANCH = 3
N_GROUPS = 4
EXPERTS_PER_GROUP = 8
N_EXPERTS = N_GROUPS * EXPERTS_PER_GROUP
TOP_E = 2
D_EXPERT = 256
ROPE_THETA = 10000.0
ALPHA = (2.0 * DEPTH) ** 0.25
BETA = (8.0 * DEPTH) ** -0.25
QBLOCK = 32
LN_EPS = 1e-5
NEG = -1e30
SPLIT_SIZES = (CONV_W, CONV_W, CONV_W, ATTN_W, ATTN_W, ATTN_W,
               N_IDX_HEADS * IDX_DIM, IDX_DIM, N_IDX_HEADS, MEM_W, N_BRANCH * D_MODEL)
D_IN = sum(SPLIT_SIZES)

kernel_name = "hybrid_conv_dsa_mem_hiermoe_step"


def layernorm(x, g, b):
    xf = x.astype(jnp.float32)
    mu = jnp.mean(xf, axis=-1, keepdims=True)
    var = jnp.mean(jnp.square(xf - mu), axis=-1, keepdims=True)
    return ((xf - mu) * lax.rsqrt(var + LN_EPS) * g.astype(jnp.float32) + b.astype(jnp.float32)).astype(x.dtype)


def rope(x, pos):
    half = x.shape[-1] // 2
    inv = ROPE_THETA ** (-jnp.arange(half, dtype=jnp.float32) / half)
    ang = pos.astype(jnp.float32)[:, None] * inv[None, :]
    cos = jnp.cos(ang)[:, None, :].astype(x.dtype)
    sin = jnp.sin(ang)[:, None, :].astype(x.dtype)
    x1, x2 = x[..., :half], x[..., half:]
    return jnp.concatenate([x1 * cos - x2 * sin, x1 * sin + x2 * cos], axis=-1)


def split_proj(x, pos, w_in, b_gate):
    b, t, _ = x.shape
    h = jnp.einsum('btd,dc->btc', x, w_in)
    parts = []
    start = 0
    for size in SPLIT_SIZES:
        parts.append(h[..., start:start + size])
        start += size
    u, cB, cC, q, k, v, qi, ki, wi, qm, g = parts
    q = rope(q.reshape(b, t, N_HEADS, HEAD_DIM), pos)
    k = rope(k.reshape(b, t, N_HEADS, HEAD_DIM), pos)
    v = v.reshape(b, t, N_HEADS, HEAD_DIM)
    qi = rope(qi.reshape(b, t, N_IDX_HEADS, IDX_DIM), pos)
    ki = rope(ki[:, :, None, :], pos)[:, :, 0, :]
    qm = qm.reshape(b, t, N_MEM_HEADS, HEAD_DIM)
    g = jax.nn.sigmoid(g + b_gate).reshape(b, t, N_BRANCH, D_MODEL)
    return u, cB, cC, q, k, v, qi, ki, wi, qm, g


def short_conv(u, cB, cC, prev, conv_w):
    z = cC * u
    ext = jnp.concatenate([prev, z], axis=1)
    t = z.shape[1]
    y = ext[:, 0:t] * conv_w[0]
    for j in range(1, CONV_K):
        y = y + ext[:, j:j + t] * conv_w[j]
    return cB * y, ext[:, -(CONV_K - 1):]


def indexer_scores(qi, ki, wi):
    s = jnp.einsum('bthd,bsd->bths', qi, ki).astype(jnp.float32) * (IDX_DIM ** -0.5)
    return jnp.einsum('bths,bth->bts', jax.nn.relu(s), wi.astype(jnp.float32) * (N_IDX_HEADS ** -0.5))


def gathered_attention(q, kg, vg, valid):
    s = jnp.einsum('bthd,btkhd->bthk', q, kg).astype(jnp.float32) * (HEAD_DIM ** -0.5)
    s = jnp.where(valid[:, :, None, :], s, NEG)
    p = jax.nn.softmax(s, axis=-1).astype(vg.dtype)
    return jnp.einsum('bthk,btkhd->bthd', p, vg)


def sparse_attn_prompt(q, k, v, qi, ki, wi):
    b, s_len = q.shape[0], q.shape[1]
    topk = min(TOPK_MAX, s_len // 4)
    n_blk = s_len // QBLOCK
    key_pos = jnp.arange(s_len, dtype=jnp.int32)

    def block(i):
        t0 = i * QBLOCK
        qb = lax.dynamic_slice_in_dim(q, t0, QBLOCK, axis=1)
        qib = lax.dynamic_slice_in_dim(qi, t0, QBLOCK, axis=1)
        wib = lax.dynamic_slice_in_dim(wi, t0, QBLOCK, axis=1)
        pos_q = t0 + jnp.arange(QBLOCK, dtype=jnp.int32)
        sc = indexer_scores(qib, ki, wib)
        causal = key_pos[None, :] <= pos_q[:, None]
        sc = jnp.where(causal[None], sc, -jnp.inf)
        _, idx = lax.top_k(sc, topk)
        valid = idx <= pos_q[None, :, None]
        kg = jax.vmap(lambda kk, ii: kk[ii])(k, idx)
        vg = jax.vmap(lambda vv, ii: vv[ii])(v, idx)
        return gathered_attention(qb, kg, vg, valid)

    out = lax.map(block, jnp.arange(n_blk, dtype=jnp.int32))
    return jnp.transpose(out, (1, 0, 2, 3, 4)).reshape(b, s_len, ATTN_W)


def sparse_attn_sample(q, k, v, qi, ki, wi, ck, cv, cki, page_table, past):
    bd, t = q.shape[0], q.shape[1]
    l_vis = past + t
    topk = min(TOPK_MAX, l_vis // 4)
    ki_past = cki[page_table].reshape(bd, past, IDX_DIM)
    ki_all = jnp.concatenate([ki_past, ki], axis=1)
    sc = indexer_scores(qi, ki_all, wi)
    pos_q = past + jnp.arange(t, dtype=jnp.int32)
    causal = jnp.arange(l_vis, dtype=jnp.int32)[None, :] <= pos_q[:, None]
    sc = jnp.where(causal[None], sc, -jnp.inf)
    _, idx = lax.top_k(sc, topk)
    valid = idx <= pos_q[None, :, None]
    bidx = jnp.arange(bd, dtype=jnp.int32)[:, None, None]
    is_past = (idx < past)[..., None, None]
    pidx = jnp.minimum(idx, past - 1)
    phys = page_table[bidx, pidx // PAGE_SIZE]
    off = pidx % PAGE_SIZE
    nidx = jnp.clip(idx - past, 0, t - 1)
    kg = jnp.where(is_past, ck[phys, off], k[bidx, nidx])
    vg = jnp.where(is_past, cv[phys, off], v[bidx, nidx])
    return gathered_attention(q, kg, vg, valid).reshape(bd, t, ATTN_W)


def mem_attention(qm, mk, mv):
    s = jnp.einsum('bthd,bmhd->bhtm', qm, mk).astype(jnp.float32) * (HEAD_DIM ** -0.5)
    p = jax.nn.softmax(s, axis=-1).astype(mv.dtype)
    return jnp.einsum('bhtm,bmhd->bthd', p, mv).reshape(qm.shape[0], qm.shape[1], MEM_W)


def hier_moe(x, w_group, b_group, w_expert, b_expert, w_up, w_down):
    b, t, d = x.shape
    xt = x.reshape(b * t, d)
    n = xt.shape[0]
    lg = (xt @ w_group + b_group).astype(jnp.float32)
    pg = jax.nn.softmax(lg, axis=-1)
    gsel = jnp.argmax(lg, axis=-1)
    gw = jnp.take_along_axis(pg, gsel[:, None], axis=1)
    le = (xt @ w_expert + b_expert).astype(jnp.float32).reshape(n, N_GROUPS, EXPERTS_PER_GROUP)
    le_sel = jnp.take_along_axis(le, gsel[:, None, None], axis=1)[:, 0]
    pe = jax.nn.softmax(le_sel, axis=-1)
    top_p, top_i = lax.top_k(pe, TOP_E)
    top_p = top_p / jnp.sum(top_p, axis=-1, keepdims=True)
    eid = gsel[:, None] * EXPERTS_PER_GROUP + top_i
    gate = jnp.zeros((n, N_EXPERTS), jnp.float32).at[jnp.arange(n)[:, None], eid].add(gw * top_p)
    gate = gate.astype(x.dtype)
    y = jnp.zeros_like(xt)
    for e in range(N_EXPERTS):
        hu = xt @ w_up[e]
        h = jax.nn.silu(hu[:, :D_EXPERT]) * hu[:, D_EXPERT:]
        y = y + gate[:, e:e + 1] * (h @ w_down[e])
    return y.reshape(b, t, d)


def merge_and_ffn(x, g, conv_out, attn_out, mem_out, w_br_conv, w_br_attn, w_br_mem, w_o,
                  ln1_g, ln1_b, w_group, b_group, w_expert, b_expert, w_up, w_down, ln2_g, ln2_b):
    m = (g[:, :, 0] * (conv_out @ w_br_conv) + g[:, :, 1] * (attn_out @ w_br_attn)
         + g[:, :, 2] * (mem_out @ w_br_mem))
    h = layernorm(ALPHA * x + m @ w_o, ln1_g, ln1_b)
    return layernorm(ALPHA * h + hier_moe(h, w_group, b_group, w_expert, b_expert, w_up, w_down), ln2_g, ln2_b)


def setup_inputs(seed: int = 0) -> dict:
    key = jax.random.key(seed)
    ks = jax.random.split(key, 32)
    n_pages = PAST_LEN // PAGE_SIZE
    n_pool = (DEC_BATCH * n_pages * 5) // 4
    f32 = jnp.float32
    nrm = lambda k, s, sc=1.0: jax.random.normal(k, s, f32) * sc
    page_table = jax.random.permutation(ks[0], n_pool)[:DEC_BATCH * n_pages].reshape(DEC_BATCH, n_pages).astype(jnp.int32)
    return {
        "x_prompt": nrm(ks[1], (BATCH, SEQ, D_MODEL)),
        "x_sample": nrm(ks[2], (DEC_BATCH, DEC_SEQ, D_MODEL)),
        "mem_prompt": nrm(ks[3], (BATCH, N_MEM, D_MODEL)),
        "cache_k": nrm(ks[4], (DEPTH, n_pool, PAGE_SIZE, N_HEADS, HEAD_DIM)),
        "cache_v": nrm(ks[5], (DEPTH, n_pool, PAGE_SIZE, N_HEADS, HEAD_DIM)),
        "cache_k_idx": nrm(ks[6], (DEPTH, n_pool, PAGE_SIZE, IDX_DIM)),
        "state_conv": nrm(ks[7], (DEPTH, DEC_BATCH, CONV_K - 1, CONV_W)),
        "cache_mem_k": nrm(ks[8], (DEPTH, DEC_BATCH, N_MEM, N_MEM_HEADS, HEAD_DIM)),
        "cache_mem_v": nrm(ks[9], (DEPTH, DEC_BATCH, N_MEM, N_MEM_HEADS, HEAD_DIM)),
        "page_table": page_table,
        "w_in": nrm(ks[10], (DEPTH, D_MODEL, D_IN), D_MODEL ** -0.5),
        "b_gate": nrm(ks[11], (DEPTH, N_BRANCH * D_MODEL), 0.02),
        "conv_w": nrm(ks[12], (DEPTH, CONV_K, CONV_W), CONV_K ** -0.5),
        "w_br_conv": nrm(ks[13], (DEPTH, CONV_W, D_MODEL), CONV_W ** -0.5),
        "w_br_attn": nrm(ks[14], (DEPTH, ATTN_W, D_MODEL), ATTN_W ** -0.5),
        "w_br_mem": nrm(ks[15], (DEPTH, MEM_W, D_MODEL), MEM_W ** -0.5),
        "w_o": nrm(ks[16], (DEPTH, D_MODEL, D_MODEL), BETA * D_MODEL ** -0.5),
        "w_mem_k": nrm(ks[17], (DEPTH, D_MODEL, MEM_W), D_MODEL ** -0.5),
        "w_mem_v": nrm(ks[18], (DEPTH, D_MODEL, MEM_W), D_MODEL ** -0.5),
        "ln1_g": 1.0 + nrm(ks[19], (DEPTH, D_MODEL), 0.02),
        "ln1_b": nrm(ks[20], (DEPTH, D_MODEL), 0.02),
        "w_group": nrm(ks[21], (DEPTH, D_MODEL, N_GROUPS), D_MODEL ** -0.5),
        "b_group": nrm(ks[22], (DEPTH, N_GROUPS), 0.01),
        "w_expert": nrm(ks[23], (DEPTH, D_MODEL, N_EXPERTS), D_MODEL ** -0.5),
        "b_expert": nrm(ks[24], (DEPTH, N_EXPERTS), 0.01),
        "w_up": nrm(ks[25], (DEPTH, N_EXPERTS, D_MODEL, 2 * D_EXPERT), D_MODEL ** -0.5),
        "w_down": nrm(ks[26], (DEPTH, N_EXPERTS, D_EXPERT, D_MODEL), BETA * D_EXPERT ** -0.5),
        "ln2_g": 1.0 + nrm(ks[27], (DEPTH, D_MODEL), 0.02),
        "ln2_b": nrm(ks[28], (DEPTH, D_MODEL), 0.02),
    }


def reference(x_prompt, x_sample, mem_prompt, cache_k, cache_v, cache_k_idx, state_conv,
              cache_mem_k, cache_mem_v, page_table, w_in, b_gate, conv_w, w_br_conv, w_br_attn,
              w_br_mem, w_o, w_mem_k, w_mem_v, ln1_g, ln1_b, w_group, b_group, w_expert,
              b_expert, w_up, w_down, ln2_g, ln2_b):
    b, s_len = x_prompt.shape[0], x_prompt.shape[1]
    bd, t = x_sample.shape[0], x_sample.shape[1]
    past = page_table.shape[1] * PAGE_SIZE
    pos_p = jnp.arange(s_len, dtype=jnp.int32)
    pos_s = past + jnp.arange(t, dtype=jnp.int32)
    hp, hs = x_prompt, x_sample
    kp_l, vp_l, kip_l, cp_l, mkp_l, mvp_l = [], [], [], [], [], []
    ks_l, vs_l, kis_l, cs_l = [], [], [], []
    for l in range(DEPTH):
        ffn_w = (w_br_conv[l], w_br_attn[l], w_br_mem[l], w_o[l], ln1_g[l], ln1_b[l], w_group[l], b_group[l],
                 w_expert[l], b_expert[l], w_up[l], w_down[l], ln2_g[l], ln2_b[l])
        u, cB, cC, q, k, v, qi, ki, wi, qm, g = split_proj(hp, pos_p, w_in[l], b_gate[l])
        conv_out, conv_new = short_conv(u, cB, cC, jnp.zeros((b, CONV_K - 1, CONV_W), hp.dtype), conv_w[l])
        attn_out = sparse_attn_prompt(q, k, v, qi, ki, wi)
        mk = jnp.einsum('bmd,dc->bmc', mem_prompt, w_mem_k[l]).reshape(b, N_MEM, N_MEM_HEADS, HEAD_DIM)
        mv = jnp.einsum('bmd,dc->bmc', mem_prompt, w_mem_v[l]).reshape(b, N_MEM, N_MEM_HEADS, HEAD_DIM)
        mem_out = mem_attention(qm, mk, mv)
        hp = merge_and_ffn(hp, g, conv_out, attn_out, mem_out, *ffn_w)
        kp_l.append(k); vp_l.append(v); kip_l.append(ki); cp_l.append(conv_new); mkp_l.append(mk); mvp_l.append(mv)
        u, cB, cC, q, k, v, qi, ki, wi, qm, g = split_proj(hs, pos_s, w_in[l], b_gate[l])
        conv_out, conv_new = short_conv(u, cB, cC, state_conv[l].astype(hs.dtype), conv_w[l])
        attn_out = sparse_attn_sample(q, k, v, qi, ki, wi, cache_k[l], cache_v[l], cache_k_idx[l], page_table, past)
        mem_out = mem_attention(qm, cache_mem_k[l], cache_mem_v[l])
        hs = merge_and_ffn(hs, g, conv_out, attn_out, mem_out, *ffn_w)
        ks_l.append(k); vs_l.append(v); kis_l.append(ki); cs_l.append(conv_new)
    return (hp, hs,
            jnp.stack(kp_l), jnp.stack(vp_l), jnp.stack(kip_l), jnp.stack(cp_l), jnp.stack(mkp_l), jnp.stack(mvp_l),
            jnp.stack(ks_l), jnp.stack(vs_l), jnp.stack(kis_l), jnp.stack(cs_l))
```

```python
import functools

import jax
import jax.numpy as jnp
from jax import lax
from jax.experimental import pallas as pl
from jax.experimental.pallas import tpu as pltpu

F32 = jnp.float32
BF16 = jnp.bfloat16
I32 = jnp.int32

HEAD_DIM = 64
N_HEADS = 8
ATTN_W = N_HEADS * HEAD_DIM
N_IDX_HEADS = 8
IDX_DIM = 64
TOPK_MAX = 256
N_MEM_HEADS = 4
MEM_W = N_MEM_HEADS * HEAD_DIM
CONV_W = 256
CONV_K = 3
N_BRANCH = 3
N_GROUPS = 4
EXPERTS_PER_GROUP = 8
N_EXPERTS = N_GROUPS * EXPERTS_PER_GROUP
PAGE_SIZE = 128
ROPE_THETA = 10000.0
LN_EPS = 1e-5
NEG = -1e30
LANES = 128

C_U, C_B, C_C = 0, CONV_W, 2 * CONV_W
C_Q = 3 * CONV_W
C_K = C_Q + ATTN_W
C_V = C_K + ATTN_W
C_QI = C_V + ATTN_W
C_QM = C_QI + N_IDX_HEADS * IDX_DIM
C_KI = C_QM + MEM_W
C_END = C_KI + LANES

INT_MIN = -(2 ** 31)
KEY_OF_NEG_INF = -2139095041
BIG_IDX = 2 ** 30
VMEM_LIMIT = 56 * 1024 * 1024

_NT = (((1,), (1,)), ((), ()))


def _dot(a, b):
    return jnp.dot(a, b, preferred_element_type=F32)


def _dot_nt(a, b):
    return lax.dot_general(a, b, _NT, preferred_element_type=F32)


def _params(*sem):
    return pltpu.CompilerParams(dimension_semantics=sem, vmem_limit_bytes=VMEM_LIMIT)


def _rope(x, cos, sin):
    w = x.shape[1]
    lane = lax.broadcasted_iota(I32, x.shape, 1)
    swapped = jnp.where((lane & 63) < 32, pltpu.roll(x, w - 32, 1), pltpu.roll(x, 32, 1))
    reps = w // LANES
    if reps > 1:
        cos = jnp.concatenate([cos] * reps, axis=1)
        sin = jnp.concatenate([sin] * reps, axis=1)
    return x * cos + swapped * sin


def _order_key(x):
    b = lax.bitcast_convert_type(x, I32)
    return b ^ ((b >> 31) & 0x7FFFFFFF)


def _layernorm(x, g, b):
    mu = jnp.mean(x, axis=-1, keepdims=True)
    xc = x - mu
    var = jnp.mean(xc * xc, axis=-1, keepdims=True)
    return xc * lax.rsqrt(var + LN_EPS) * g + b


def _softmax_rows(s):
    m = jnp.max(s, axis=-1, keepdims=True)
    e = jnp.exp(s - m)
    return e / jnp.sum(e, axis=-1, keepdims=True)


def _memkv_kernel(mem_ref, wk_ref, wv_ref, mk_ref, mv_ref):
    m = mem_ref[0].astype(BF16)
    mk_ref[0] = _dot(m, wk_ref[...])
    mv_ref[0] = _dot(m, wv_ref[...])


def _memkv(mem, wk, wv):
    b, n_mem, d = mem.shape
    return pl.pallas_call(
        _memkv_kernel,
        grid=(b,),
        in_specs=[pl.BlockSpec((1, n_mem, d), lambda i: (i, 0, 0)),
                  pl.BlockSpec((d, MEM_W), lambda i: (0, 0)),
                  pl.BlockSpec((d, MEM_W), lambda i: (0, 0))],
        out_specs=[pl.BlockSpec((1, n_mem, MEM_W), lambda i: (i, 0, 0))] * 2,
        out_shape=[jax.ShapeDtypeStruct((b, n_mem, MEM_W), F32)] * 2,
        compiler_params=_params("parallel"),
        name="memkv",
    )(mem, wk, wv)


def _inproj_kernel(x_ref, w_ref, cos_ref, sin_ref, cw_ref, mk_ref, mv_ref,
                   conv_ref, mem_ref, q_ref, k_ref, v_ref, kh_ref, vh_ref, qi_ref, ki_ref, kib_ref, wi_ref,
                   cnew_ref, zprev_ref):
    j = pl.program_id(1)
    tm = x_ref.shape[1]
    xb = x_ref[0].astype(BF16)
    cos = cos_ref[...]
    sin = sin_ref[...]

    def proj(a, b):
        return _dot(xb, w_ref[:, a:b])

    @pl.when(j == 0)
    def _():
        zprev_ref[...] = jnp.zeros_like(zprev_ref)

    ucc = proj(C_U, C_Q)
    z = ucc[:, C_C:C_Q] * ucc[:, C_U:C_B]
    prev = zprev_ref[...]
    row = lax.broadcasted_iota(I32, z.shape, 0)
    z1 = jnp.where(row == 0, prev[7:8, :], pltpu.roll(z, 1, 0))
    z2 = jnp.where(row == 0, prev[6:7, :], jnp.where(row == 1, prev[7:8, :], pltpu.roll(z, 2, 0)))
    cw = cw_ref[...]
    y = z2 * cw[0:1, :] + z1 * cw[1:2, :] + z * cw[2:3, :]
    conv_ref[0] = (ucc[:, C_B:C_C] * y).astype(BF16)
    zprev_ref[...] = z[tm - 8:tm, :]
    cnew_ref[0] = z[tm - 2:tm, :]

    q = _rope(proj(C_Q, C_K), cos, sin) * (HEAD_DIM ** -0.5)
    k = _rope(proj(C_K, C_V), cos, sin)
    v = proj(C_V, C_QI)
    qi = _rope(proj(C_QI, C_QM), cos, sin) * (IDX_DIM ** -0.5)
    k_ref[0] = k
    v_ref[0] = v
    for h in range(N_HEADS):
        sl = slice(h * HEAD_DIM, (h + 1) * HEAD_DIM)
        q_ref[0, h] = q[:, sl].astype(BF16)
        kh_ref[0, h] = k[:, sl].astype(BF16)
        vh_ref[0, h] = v[:, sl].astype(BF16)
        qi_ref[0, h] = qi[:, sl].astype(BF16)

    kiw = proj(C_KI, C_END)
    ki = _rope(kiw, cos, sin)[:, :IDX_DIM]
    ki_ref[0] = ki
    kib_ref[0] = ki.astype(BF16)
    wi_ref[0] = kiw[:, IDX_DIM:IDX_DIM + N_IDX_HEADS] * (N_IDX_HEADS ** -0.5)

    qm = (proj(C_QM, C_KI) * (HEAD_DIM ** -0.5)).astype(BF16)
    mk = mk_ref[0].astype(BF16)
    mv = mv_ref[0].astype(BF16)
    outs = []
    for h in range(N_MEM_HEADS):
        sl = slice(h * HEAD_DIM, (h + 1) * HEAD_DIM)
        p = _softmax_rows(_dot_nt(qm[:, sl], mk[:, sl]))
        outs.append(_dot(p.astype(BF16), mv[:, sl]))
    mem_ref[0] = jnp.concatenate(outs, axis=1).astype(BF16)


def _inproj(x, wcat, cos, sin, conv_w, mk, mv, tm):
    b, t, d = x.shape
    n_mem = mk.shape[1]
    grid = (b, t // tm)
    tok = lambda w: pl.BlockSpec((1, tm, w), lambda i, j: (i, j, 0))
    hm = pl.BlockSpec((1, N_HEADS, tm, HEAD_DIM), lambda i, j: (i, 0, j, 0))
    const2 = lambda r, c: pl.BlockSpec((r, c), lambda i, j: (0, 0))
    hm_shape = jax.ShapeDtypeStruct((b, N_HEADS, t, HEAD_DIM), BF16)
    out_shape = [
        jax.ShapeDtypeStruct((b, t, CONV_W), BF16),
        jax.ShapeDtypeStruct((b, t, MEM_W), BF16),
        hm_shape,
        jax.ShapeDtypeStruct((b, t, ATTN_W), F32),
        jax.ShapeDtypeStruct((b, t, ATTN_W), F32),
        hm_shape, hm_shape, hm_shape,
        jax.ShapeDtypeStruct((b, t, IDX_DIM), F32),
        jax.ShapeDtypeStruct((b, t, IDX_DIM), BF16),
        jax.ShapeDtypeStruct((b, t, N_IDX_HEADS), F32),
        jax.ShapeDtypeStruct((b, CONV_K - 1, CONV_W), F32),
    ]
    out_specs = [tok(CONV_W), tok(MEM_W), hm, tok(ATTN_W), tok(ATTN_W), hm, hm, hm, tok(IDX_DIM), tok(IDX_DIM),
                 tok(N_IDX_HEADS), pl.BlockSpec((1, CONV_K - 1, CONV_W), lambda i, j: (i, 0, 0))]
    return pl.pallas_call(
        _inproj_kernel,
        grid=grid,
        in_specs=[tok(d), const2(d, C_END),
                  pl.BlockSpec((tm, LANES), lambda i, j: (j, 0)), pl.BlockSpec((tm, LANES), lambda i, j: (j, 0)),
                  const2(CONV_K, CONV_W),
                  pl.BlockSpec((1, n_mem, MEM_W), lambda i, j: (i, 0, 0)),
                  pl.BlockSpec((1, n_mem, MEM_W), lambda i, j: (i, 0, 0))],
        out_specs=out_specs,
        out_shape=out_shape,
        scratch_shapes=[pltpu.VMEM((8, CONV_W), F32)],
        compiler_params=_params("parallel", "arbitrary"),
        name="inproj",
    )(x, wcat, cos, sin, conv_w, mk, mv)


def _dsa_kernel(qi_ref, wi_ref, ki_ref, q_ref, k_ref, v_ref, o_ref, key_ref, bias_ref, j_ref, *, topk):
    tq = q_ref.shape[2]
    kc = key_ref.shape[2]
    i = pl.program_id(1)
    nk = i + 1
    t0 = i * tq
    w = wi_ref[0]
    row = lax.broadcasted_iota(I32, (tq, kc), 0) + t0
    col = lax.broadcasted_iota(I32, (tq, kc), 1)

    def chunk_start(c):
        return pl.multiple_of(c * kc, kc)

    def score_body(c, carry):
        kic = ki_ref[0, pl.ds(chunk_start(c), kc), :]
        acc = jnp.zeros((tq, kc), F32)
        for h in range(N_IDX_HEADS):
            acc = acc + w[:, h:h + 1] * jnp.maximum(_dot_nt(qi_ref[0, h], kic), 0.0)
        acc = jnp.where(col + c * kc <= row, acc, -jnp.inf)
        key_ref[c] = _order_key(acc)
        return carry

    lax.fori_loop(0, nk, score_body, 0)

    def count(pred):
        def body(c, cnt):
            one = jnp.where(pred(key_ref[c], col + c * kc), 1.0, 0.0)
            for s in range(kc // LANES):
                cnt = cnt + one[:, s * LANES:(s + 1) * LANES]
            return cnt
        cnt = lax.fori_loop(0, nk, body, jnp.zeros((tq, LANES), F32))
        return jnp.sum(cnt, axis=1, keepdims=True)

    def bit_body(it, carry):
        thr, cnt_thr = carry
        cand = thr ^ lax.shift_left(jnp.int32(1), 31 - it)
        cnt = count(lambda kk, idx: kk >= cand)
        ok = cnt >= topk
        return jnp.where(ok, cand, thr), jnp.where(ok, cnt, cnt_thr)

    thr, cnt_thr = lax.fori_loop(
        0, 32, bit_body, (jnp.full((tq, 1), INT_MIN, I32), jnp.full((tq, 1), float(topk), F32)))

    j_ref[...] = jnp.full((tq, 1), BIG_IDX, I32)
    tie_rows = (thr > KEY_OF_NEG_INF) & (cnt_thr > topk)

    @pl.when(jnp.max(jnp.where(tie_rows, 1.0, 0.0)) > 0.0)
    def _():
        need = topk - count(lambda kk, idx: kk > thr)

        def idx_body(it, jj):
            cand = jj | lax.shift_left(jnp.int32(1), 29 - it)
            below = count(lambda kk, idx: (kk == thr) & (idx < cand))
            return jnp.where(below < need, cand, jj)

        j_ref[...] = lax.fori_loop(0, 30, idx_body, jnp.zeros((tq, 1), I32))

    jlast = j_ref[...]

    def bias_body(c, carry):
        kk = key_ref[c]
        idx = col + c * kc
        sel = ((kk > thr) | ((kk == thr) & (idx <= jlast))) & (idx <= row)
        bias_ref[c] = jnp.where(sel, 0.0, NEG)
        return carry

    lax.fori_loop(0, nk, bias_body, 0)

    for h in range(N_HEADS):
        qh = q_ref[0, h]

        def att_body(c, carry):
            m, l, acc = carry
            ks = pl.ds(chunk_start(c), kc)
            s = _dot_nt(qh, k_ref[0, h, ks, :]) + bias_ref[c]
            m_new = jnp.maximum(m, jnp.max(s, axis=1, keepdims=True))
            alpha = jnp.exp(m - m_new)
            p = jnp.exp(s - m_new)
            l = alpha * l + jnp.sum(p, axis=1, keepdims=True)
            acc = alpha * acc + _dot(p.astype(BF16), v_ref[0, h, ks, :])
            return m_new, l, acc

        m, l, acc = lax.fori_loop(
            0, nk, att_body,
            (jnp.full((tq, 1), NEG, F32), jnp.zeros((tq, 1), F32), jnp.zeros((tq, HEAD_DIM), F32)))
        o_ref[0, :, h * HEAD_DIM:(h + 1) * HEAD_DIM] = (acc / l).astype(BF16)


def _dsa(qi, wi, kib, q, kh, vh, tq, topk):
    b, _, t, _ = q.shape
    nq = t // tq
    hm_q = pl.BlockSpec((1, N_HEADS, tq, HEAD_DIM), lambda i, j: (i, 0, j, 0))
    hm_kv = pl.BlockSpec((1, N_HEADS, t, HEAD_DIM), lambda i, j: (i, 0, 0, 0))
    return pl.pallas_call(
        functools.partial(_dsa_kernel, topk=topk),
        grid=(b, nq),
        in_specs=[hm_q,
                  pl.BlockSpec((1, tq, N_IDX_HEADS), lambda i, j: (i, j, 0)),
                  pl.BlockSpec((1, t, IDX_DIM), lambda i, j: (i, 0, 0)),
                  hm_q, hm_kv, hm_kv],
        out_specs=pl.BlockSpec((1, tq, ATTN_W), lambda i, j: (i, j, 0)),
        out_shape=jax.ShapeDtypeStruct((b, t, ATTN_W), BF16),
        scratch_shapes=[pltpu.VMEM((nq, tq, tq), I32), pltpu.VMEM((nq, tq, tq), F32), pltpu.VMEM((tq, 1), I32)],
        compiler_params=_params("parallel", "arbitrary"),
        name="dsa_prompt",
    )(qi, wi, kib, q, kh, vh)


def _merge_kernel(x_ref, conv_ref, attn_ref, mem_ref, wg_ref, bg_ref, wc_ref, wa_ref, wm_ref, wo_ref,
                  g1_ref, b1_ref, wrh_ref, wrl_ref, br_ref, h_ref, gate_ref, *, alpha):
    x = x_ref[...]
    xb = x.astype(BF16)
    d = x.shape[1]
    m = None
    for br, (src, wref) in enumerate(((conv_ref, wc_ref), (attn_ref, wa_ref), (mem_ref, wm_ref))):
        g = jax.nn.sigmoid(_dot(xb, wg_ref[:, br * d:(br + 1) * d]) + bg_ref[:, br * d:(br + 1) * d])
        term = g * _dot(src[...], wref[...])
        m = term if m is None else m + term
    h = _layernorm(alpha * x + _dot(m.astype(BF16), wo_ref[...]), g1_ref[...], b1_ref[...])
    h_ref[...] = h

    hh = h.astype(BF16)
    hl = (h - hh.astype(F32)).astype(BF16)
    lo = _dot(hh, wrh_ref[...]) + _dot(hl, wrh_ref[...]) + _dot(hh, wrl_ref[...]) + br_ref[...]
    lane = lax.broadcasted_iota(I32, lo.shape, 1)
    lanef = lane.astype(F32)
    is_g = (lane >= N_EXPERTS) & (lane < N_EXPERTS + N_GROUPS)
    mg = jnp.max(jnp.where(is_g, lo, -jnp.inf), axis=1, keepdims=True)
    sg = jnp.sum(jnp.where(is_g, jnp.exp(lo - mg), 0.0), axis=1, keepdims=True)
    gw = 1.0 / sg
    gsel = jnp.min(jnp.where(is_g & (lo == mg), lanef, 1e9), axis=1, keepdims=True) - N_EXPERTS
    in_g = (lane < N_EXPERTS) & ((lane >> 3).astype(F32) == gsel)
    me = jnp.max(jnp.where(in_g, lo, -jnp.inf), axis=1, keepdims=True)
    ee = jnp.where(in_g, jnp.exp(lo - me), 0.0)
    pe = jnp.where(in_g, ee / jnp.sum(ee, axis=1, keepdims=True), -1.0)
    p1 = jnp.max(pe, axis=1, keepdims=True)
    i1 = jnp.min(jnp.where(pe == p1, lanef, 1e9), axis=1, keepdims=True)
    pe2 = jnp.where(lanef == i1, -1.0, pe)
    p2 = jnp.max(pe2, axis=1, keepdims=True)
    i2 = jnp.min(jnp.where(pe2 == p2, lanef, 1e9), axis=1, keepdims=True)
    nrm = p1 + p2
    gate = gw * jnp.where(lanef == i1, p1 / nrm, jnp.where(lanef == i2, p2 / nrm, 0.0))
    gate_ref[...] = gate[:, :N_EXPERTS]


def _merge(x, conv, attn, mem, wts, tm, alpha):
    n, d = x.shape
    wg, bg, wc, wa, wm, wo, g1, b1, wrh, wrl, br = wts
    tok = lambda w: pl.BlockSpec((tm, w), lambda i: (i, 0))
    const = lambda a: pl.BlockSpec(a.shape, lambda i: (0, 0))
    return pl.pallas_call(
        functools.partial(_merge_kernel, alpha=alpha),
        grid=(n // tm,),
        in_specs=[tok(d), tok(CONV_W), tok(ATTN_W), tok(MEM_W)] + [const(a) for a in wts],
        out_specs=[tok(d), tok(N_EXPERTS)],
        out_shape=[jax.ShapeDtypeStruct((n, d), F32), jax.ShapeDtypeStruct((n, N_EXPERTS), F32)],
        compiler_params=_params("parallel"),
        name="merge_ln_router",
    )(x, conv, attn, mem, *wts)


def _moe_kernel(h_ref, gate_ref, wu_ref, wd_ref, g2_ref, b2_ref, o_ref, hb_ref, y_ref, *, alpha):
    e = pl.program_id(1)
    de = wd_ref.shape[1]

    @pl.when(e == 0)
    def _():
        hb_ref[...] = h_ref[...].astype(BF16)
        y_ref[...] = jnp.zeros_like(y_ref)

    hu = _dot(hb_ref[...], wu_ref[0])
    a = hu[:, :de]
    act = (a * (1.0 / (1.0 + jnp.exp(-a))) * hu[:, de:]).astype(BF16)
    gate = gate_ref[...]
    lane = lax.broadcasted_iota(I32, gate.shape, 1)
    ge = jnp.sum(jnp.where(lane == e, gate, 0.0), axis=1, keepdims=True)
    y_ref[...] += ge * _dot(act, wd_ref[0])

    @pl.when(e == pl.num_programs(1) - 1)
    def _():
        o_ref[...] = _layernorm(alpha * h_ref[...] + y_ref[...], g2_ref[...], b2_ref[...])


def _moe(h, gate, wu, wd, g2, b2, tm, alpha):
    n, d = h.shape
    ne, _, du = wu.shape
    de = wd.shape[1]
    return pl.pallas_call(
        functools.partial(_moe_kernel, alpha=alpha),
        grid=(n // tm, ne),
        in_specs=[pl.BlockSpec((tm, d), lambda i, e: (i, 0)),
                  pl.BlockSpec((tm, ne), lambda i, e: (i, 0)),
                  pl.BlockSpec((1, d, du), lambda i, e: (e, 0, 0)),
                  pl.BlockSpec((1, de, d), lambda i, e: (e, 0, 0)),
                  pl.BlockSpec((1, d), lambda i, e: (0, 0)),
                  pl.BlockSpec((1, d), lambda i, e: (0, 0))],
        out_specs=pl.BlockSpec((tm, d), lambda i, e: (i, 0)),
        out_shape=jax.ShapeDtypeStruct((n, d), F32),
        scratch_shapes=[pltpu.VMEM((tm, d), BF16), pltpu.VMEM((tm, d), F32)],
        compiler_params=_params("parallel", "arbitrary"),
        name="moe_ln",
    )(h, gate, wu, wd, g2, b2)


def _sample_inproj_kernel(x_ref, w_ref, cos_ref, sin_ref, cw_ref, p0_ref, p1_ref,
                          conv_ref, z_ref, q_ref, k_ref, v_ref, qi_ref, ki_ref, wi_ref, qm_ref):
    xb = x_ref[...].astype(BF16)
    cos = cos_ref[...]
    sin = sin_ref[...]

    def proj(a, b):
        return _dot(xb, w_ref[:, a:b])

    ucc = proj(C_U, C_Q)
    z = ucc[:, C_C:C_Q] * ucc[:, C_U:C_B]
    cw = cw_ref[...]
    y = p0_ref[...] * cw[0:1, :] + p1_ref[...] * cw[1:2, :] + z * cw[2:3, :]
    conv_ref[...] = (ucc[:, C_B:C_C] * y).astype(BF16)
    z_ref[...] = z
    q_ref[...] = _rope(proj(C_Q, C_K), cos, sin) * (HEAD_DIM ** -0.5)
    k_ref[...] = _rope(proj(C_K, C_V), cos, sin)
    v_ref[...] = proj(C_V, C_QI)
    qi_ref[...] = _rope(proj(C_QI, C_QM), cos, sin) * (IDX_DIM ** -0.5)
    kiw = proj(C_KI, C_END)
    ki_ref[...] = _rope(kiw, cos, sin)[:, :IDX_DIM]
    wi_ref[...] = kiw[:, IDX_DIM:IDX_DIM + N_IDX_HEADS] * (N_IDX_HEADS ** -0.5)
    qm_ref[...] = proj(C_QM, C_KI) * (HEAD_DIM ** -0.5)


def _sample_inproj(x, wcat, cos, sin, conv_w, p0, p1):
    n, d = x.shape
    args = (x, wcat, cos, sin, conv_w, p0, p1)
    widths = (CONV_W, CONV_W, ATTN_W, ATTN_W, ATTN_W, N_IDX_HEADS * IDX_DIM, IDX_DIM, N_IDX_HEADS, MEM_W)
    dtypes = (BF16,) + (F32,) * 8
    full = lambda shape: pl.BlockSpec(shape, lambda i: (0,) * len(shape))
    return pl.pallas_call(
        _sample_inproj_kernel,
        grid=(1,),
        in_specs=[full(a.shape) for a in args],
        out_specs=[full((n, w)) for w in widths],
        out_shape=[jax.ShapeDtypeStruct((n, w), dt) for w, dt in zip(widths, dtypes)],
        compiler_params=_params("arbitrary"),
        name="sample_inproj",
    )(*args)


def _head_mask(rows, width):
    r = lax.broadcasted_iota(I32, (rows, width), 0)
    c = lax.broadcasted_iota(I32, (rows, width), 1)
    return (c >> 6) == r


def _sample_mem_kernel(qm_ref, mk_ref, mv_ref, o_ref):
    hmask = _head_mask(8, MEM_W)
    qbd = jnp.where(hmask, jnp.broadcast_to(qm_ref[0], (8, MEM_W)), 0.0).astype(BF16)
    p = _softmax_rows(_dot_nt(qbd, mk_ref[0].astype(BF16)))
    o = _dot(p.astype(BF16), mv_ref[0].astype(BF16))
    o_ref[0] = jnp.sum(jnp.where(hmask, o, 0.0), axis=0, keepdims=True).astype(BF16)


def _sample_mem(qm, mk, mv):
    n, n_mem, w = mk.shape
    return pl.pallas_call(
        _sample_mem_kernel,
        grid=(n,),
        in_specs=[pl.BlockSpec((1, 1, w), lambda i: (i, 0, 0)),
                  pl.BlockSpec((1, n_mem, w), lambda i: (i, 0, 0)),
                  pl.BlockSpec((1, n_mem, w), lambda i: (i, 0, 0))],
        out_specs=pl.BlockSpec((1, 1, w), lambda i: (i, 0, 0)),
        out_shape=jax.ShapeDtypeStruct((n, 1, w), BF16),
        compiler_params=_params("parallel"),
        name="sample_mem_attn",
    )(qm, mk, mv)


def _sample_scores_kernel(pt_ref, *refs, pages):
    page_refs = refs[:pages]
    qi_ref, wi_ref, o_ref = refs[pages:]
    keys = jnp.concatenate([r[0] for r in page_refs], axis=0).astype(BF16)
    s = _dot_nt(qi_ref[0].astype(BF16), keys)
    o_ref[0, 0] = jnp.sum(wi_ref[0] * jnp.maximum(s, 0.0), axis=0, keepdims=True)


def _sample_scores(page_table, cki, qi, wi, pages):
    n, n_pages = page_table.shape
    ng = n_pages // pages
    width = pages * PAGE_SIZE

    def page_spec(j):
        return pl.BlockSpec((1, PAGE_SIZE, IDX_DIM), lambda s, pt: (pt[s // ng, (s % ng) * pages + j], 0, 0))

    grid_spec = pltpu.PrefetchScalarGridSpec(
        num_scalar_prefetch=1,
        grid=(n * ng,),
        in_specs=[page_spec(j) for j in range(pages)] + [
            pl.BlockSpec((1, N_IDX_HEADS, IDX_DIM), lambda s, pt: (s // ng, 0, 0)),
            pl.BlockSpec((1, N_IDX_HEADS, 1), lambda s, pt: (s // ng, 0, 0))],
        out_specs=pl.BlockSpec((1, 1, 1, width), lambda s, pt: (s // ng, s % ng, 0, 0)),
    )
    out = pl.pallas_call(
        functools.partial(_sample_scores_kernel, pages=pages),
        grid_spec=grid_spec,
        out_shape=jax.ShapeDtypeStruct((n, ng, 1, width), F32),
        compiler_params=_params("arbitrary"),
        name="sample_idx_scores",
    )(page_table, *([cki] * pages), qi, wi)
    return out.reshape(n, n_pages * PAGE_SIZE)


def _sample_select_kernel(sc_ref, qi_ref, ki_ref, wi_ref, bias_ref, bnew_ref, key_ref, j_ref, *, topk):
    n, past = sc_ref.shape
    qi = qi_ref[...].astype(BF16).astype(F32)
    ki = ki_ref[...].astype(BF16).astype(F32)
    s_new = jnp.maximum(jnp.sum(qi * ki, axis=2), 0.0)
    knew = _order_key(jnp.sum(wi_ref[...] * s_new, axis=1, keepdims=True))
    key_ref[...] = _order_key(sc_ref[...])
    idx = lax.broadcasted_iota(I32, (n, past), 1)

    def count(pred):
        one = jnp.where(pred(key_ref[...], idx), 1.0, 0.0)
        return jnp.sum(one, axis=1, keepdims=True) + jnp.where(pred(knew, past), 1.0, 0.0)

    def bit_body(it, carry):
        thr, cnt_thr = carry
        cand = thr ^ lax.shift_left(jnp.int32(1), 31 - it)
        cnt = count(lambda kk, ii: kk >= cand)
        ok = cnt >= topk
        return jnp.where(ok, cand, thr), jnp.where(ok, cnt, cnt_thr)

    thr, cnt_thr = lax.fori_loop(
        0, 32, bit_body, (jnp.full((n, 1), INT_MIN, I32), jnp.full((n, 1), float(topk), F32)))
    need = topk - count(lambda kk, ii: kk > thr)

    def idx_body(it, jj):
        cand = jj | lax.shift_left(jnp.int32(1), 29 - it)
        below = count(lambda kk, ii: (kk == thr) & (ii < cand))
        return jnp.where(below < need, cand, jj)

    j_ref[...] = jnp.full((n, 1), BIG_IDX, I32)

    @pl.when(jnp.max(jnp.where(cnt_thr > topk, 1.0, 0.0)) > 0.0)
    def _():
        j_ref[...] = lax.fori_loop(0, 30, idx_body, jnp.zeros((n, 1), I32))

    jlast = j_ref[...]
    kk = key_ref[...]
    bias_ref[...] = jnp.where((kk > thr) | ((kk == thr) & (idx <= jlast)), 0.0, NEG)
    bnew_ref[...] = jnp.where((knew > thr) | ((knew == thr) & (past <= jlast)), 0.0, NEG)


def _sample_select(scores, qi, ki, wi, topk):
    n, past = scores.shape
    args = (scores, qi, ki, wi)
    full = lambda shape: pl.BlockSpec(shape, lambda i: (0,) * len(shape))
    return pl.pallas_call(
        functools.partial(_sample_select_kernel, topk=topk),
        grid=(1,),
        in_specs=[full(a.shape) for a in args],
        out_specs=[full((n, past)), full((n, 1))],
        out_shape=[jax.ShapeDtypeStruct((n, past), F32), jax.ShapeDtypeStruct((n, 1), F32)],
        scratch_shapes=[pltpu.VMEM((n, past), I32), pltpu.VMEM((n, 1), I32)],
        compiler_params=_params("arbitrary"),
        name="sample_topk_select",
    )(*args)


def _sample_attn_kernel(pt_ref, *refs, pages, nch):
    k_refs = refs[:pages]
    v_refs = refs[pages:2 * pages]
    q_ref, bias_ref, bnew_ref, kn_ref, vn_ref, o_ref, m_ref, l_ref, acc_ref = refs[2 * pages:]
    c = pl.program_id(0) % nch
    hmask = _head_mask(N_HEADS, ATTN_W)
    qbd = jnp.where(hmask, jnp.broadcast_to(q_ref[0], (N_HEADS, ATTN_W)), 0.0).astype(BF16)

    @pl.when(c == 0)
    def _():
        m_ref[...] = jnp.full_like(m_ref, NEG)
        l_ref[...] = jnp.zeros_like(l_ref)
        acc_ref[...] = jnp.zeros_like(acc_ref)

    kk = jnp.concatenate([r[0] for r in k_refs], axis=0).astype(BF16)
    vv = jnp.concatenate([r[0] for r in v_refs], axis=0).astype(BF16)
    s = _dot_nt(qbd, kk) + bias_ref[0, 0]
    m = m_ref[...]
    m_new = jnp.maximum(m, jnp.max(s, axis=1, keepdims=True))
    alpha = jnp.exp(m - m_new)
    p = jnp.exp(s - m_new)
    l_ref[...] = alpha * l_ref[...] + jnp.sum(p, axis=1, keepdims=True)
    acc_ref[...] = alpha * acc_ref[...] + _dot(p.astype(BF16), vv)
    m_ref[...] = m_new

    @pl.when(c == nch - 1)
    def _():
        kn = kn_ref[0].astype(BF16).astype(F32)
        vn = vn_ref[0].astype(BF16).astype(F32)
        s_n = jnp.sum(qbd.astype(F32) * kn, axis=1, keepdims=True) + bnew_ref[0]
        m0 = m_ref[...]
        m1 = jnp.maximum(m0, s_n)
        a1 = jnp.exp(m0 - m1)
        p_n = jnp.exp(s_n - m1).astype(BF16).astype(F32)
        l1 = a1 * l_ref[...] + p_n
        acc1 = a1 * acc_ref[...] + p_n * vn
        o_ref[0] = jnp.sum(jnp.where(hmask, acc1 / l1, 0.0), axis=0, keepdims=True).astype(BF16)


def _sample_attn(page_table, ck, cv, q, bias, bnew, kn, vn, pages):
    n, n_pages = page_table.shape
    nch = n_pages // pages
    width = pages * PAGE_SIZE

    def page_spec(j):
        return pl.BlockSpec((1, PAGE_SIZE, ATTN_W), lambda s, pt: (pt[s // nch, (s % nch) * pages + j], 0, 0))

    per_seq = lambda w: pl.BlockSpec((1, 1, w), lambda s, pt: (s // nch, 0, 0))
    grid_spec = pltpu.PrefetchScalarGridSpec(
        num_scalar_prefetch=1,
        grid=(n * nch,),
        in_specs=[page_spec(j) for j in range(pages)] * 2 + [
            per_seq(ATTN_W),
            pl.BlockSpec((1, 1, 1, width), lambda s, pt: (s // nch, s % nch, 0, 0)),
            per_seq(1), per_seq(ATTN_W), per_seq(ATTN_W)],
        out_specs=per_seq(ATTN_W),
        scratch_shapes=[pltpu.VMEM((N_HEADS, 1), F32), pltpu.VMEM((N_HEADS, 1), F32),
                        pltpu.VMEM((N_HEADS, ATTN_W), F32)],
    )
    return pl.pallas_call(
        functools.partial(_sample_attn_kernel, pages=pages, nch=nch),
        grid_spec=grid_spec,
        out_shape=jax.ShapeDtypeStruct((n, 1, ATTN_W), BF16),
        compiler_params=_params("arbitrary"),
        name="sample_attn",
    )(page_table, *([ck] * pages), *([cv] * pages), q, bias.reshape(n, nch, 1, width), bnew, kn, vn)


def _rope_tables(pos):
    half = HEAD_DIM // 2
    inv = ROPE_THETA ** (-jnp.arange(half, dtype=F32) / half)
    ang = pos.astype(F32)[:, None] * inv[None, :]
    cos = jnp.cos(ang)
    sin = jnp.sin(ang)
    return jnp.concatenate([cos] * 4, axis=1), jnp.concatenate([-sin, sin] * 2, axis=1)


def _largest_divisor(n, cap):
    for c in range(min(cap, n), 0, -1):
        if n % c == 0:
            return c
    return 1


def _pack_layer_weights(l, w_in, b_gate, w_br_conv, w_br_attn, w_br_mem, w_o, ln1_g, ln1_b, w_group, b_group,
                        w_expert, b_expert):
    d = w_in.shape[1]
    sizes = (CONV_W, CONV_W, CONV_W, ATTN_W, ATTN_W, ATTN_W, N_IDX_HEADS * IDX_DIM, IDX_DIM, N_IDX_HEADS, MEM_W,
             N_BRANCH * d)
    offs = [0]
    for s in sizes:
        offs.append(offs[-1] + s)
    col = lambda i: w_in[l][:, offs[i]:offs[i + 1]]
    pad = jnp.zeros((d, LANES - IDX_DIM - N_IDX_HEADS), F32)
    wcat = jnp.concatenate([col(0), col(1), col(2), col(3), col(4), col(5), col(6), col(9), col(7), col(8), pad],
                           axis=1).astype(BF16)
    wr = jnp.concatenate([w_expert[l], w_group[l], jnp.zeros((d, LANES - N_EXPERTS - N_GROUPS), F32)], axis=1)
    wrh = wr.astype(BF16)
    wrl = (wr - wrh.astype(F32)).astype(BF16)
    br = jnp.concatenate([b_expert[l], b_group[l], jnp.zeros((LANES - N_EXPERTS - N_GROUPS,), F32)])[None, :]
    merge_w = (col(10).astype(BF16), b_gate[l][None, :], w_br_conv[l].astype(BF16), w_br_attn[l].astype(BF16),
               w_br_mem[l].astype(BF16), w_o[l].astype(BF16), ln1_g[l][None, :], ln1_b[l][None, :], wrh, wrl, br)
    return wcat, merge_w


def kernel(x_prompt, x_sample, mem_prompt, cache_k, cache_v, cache_k_idx, state_conv, cache_mem_k, cache_mem_v,
           page_table, w_in, b_gate, conv_w, w_br_conv, w_br_attn, w_br_mem, w_o, w_mem_k, w_mem_v, ln1_g, ln1_b,
           w_group, b_group, w_expert, b_expert, w_up, w_down, ln2_g, ln2_b):
    depth = w_in.shape[0]
    b, s_len, d = x_prompt.shape
    bd, t_new, _ = x_sample.shape
    assert t_new == 1, "the sample group decodes one token per sequence"
    n_pages = page_table.shape[1]
    past = n_pages * PAGE_SIZE
    n_pool = cache_k.shape[1]
    alpha = (2.0 * depth) ** 0.25

    tq = _largest_divisor(s_len, 256)
    tm_in = _largest_divisor(s_len, 512)
    n_tok = b * s_len
    tm_merge = _largest_divisor(n_tok, 512)
    tm_moe = _largest_divisor(n_tok, 1024)
    topk_p = min(TOPK_MAX, s_len // 4)
    topk_s = min(TOPK_MAX, (past + t_new) // 4)
    pages_idx = _largest_divisor(n_pages, 16)
    pages_att = _largest_divisor(n_pages, 8)

    cos_p, sin_p = _rope_tables(jnp.arange(s_len, dtype=jnp.int32))
    cos_s, sin_s = _rope_tables(past + jnp.arange(t_new, dtype=jnp.int32))

    hp = x_prompt
    hs = x_sample.reshape(bd, d)
    outs = [[] for _ in range(10)]
    for l in range(depth):
        wcat, merge_w = _pack_layer_weights(l, w_in, b_gate, w_br_conv, w_br_attn, w_br_mem, w_o, ln1_g, ln1_b,
                                            w_group, b_group, w_expert, b_expert)
        wu = w_up[l].astype(BF16)
        wd = w_down[l].astype(BF16)
        g2, b2 = ln2_g[l][None, :], ln2_b[l][None, :]

        mk, mv = _memkv(mem_prompt, w_mem_k[l].astype(BF16), w_mem_v[l].astype(BF16))
        (conv_o, mem_o, q_h, k_p, v_p, k_h, v_h, qi_h, ki_p, ki_b, wi_p, conv_new) = _inproj(
            hp, wcat, cos_p, sin_p, conv_w[l], mk, mv, tm_in)
        attn_o = _dsa(qi_h, wi_p, ki_b, q_h, k_h, v_h, tq, topk_p)
        h1, gate = _merge(hp.reshape(n_tok, d), conv_o.reshape(n_tok, CONV_W), attn_o.reshape(n_tok, ATTN_W),
                          mem_o.reshape(n_tok, MEM_W), merge_w, tm_merge, alpha)
        hp = _moe(h1, gate, wu, wd, g2, b2, tm_moe, alpha).reshape(b, s_len, d)
        for lst, val in zip(outs[:6], (k_p.reshape(b, s_len, N_HEADS, HEAD_DIM),
                                       v_p.reshape(b, s_len, N_HEADS, HEAD_DIM), ki_p, conv_new,
                                       mk.reshape(b, -1, N_MEM_HEADS, HEAD_DIM),
                                       mv.reshape(b, -1, N_MEM_HEADS, HEAD_DIM))):
            lst.append(val)

        st = state_conv[l].astype(F32)
        (conv_s, z_s, q_s, k_s, v_s, qi_s, ki_s, wi_s, qm_s) = _sample_inproj(
            hs, wcat, cos_s, sin_s, conv_w[l], st[:, 0, :], st[:, 1, :])
        mem_s = _sample_mem(qm_s.reshape(bd, 1, MEM_W), cache_mem_k[l].reshape(bd, -1, MEM_W),
                            cache_mem_v[l].reshape(bd, -1, MEM_W))
        qi3 = qi_s.reshape(bd, N_IDX_HEADS, IDX_DIM)
        wi3 = wi_s.reshape(bd, N_IDX_HEADS, 1)
        scores = _sample_scores(page_table, cache_k_idx[l], qi3, wi3, pages_idx)
        bias, bnew = _sample_select(scores, qi3, ki_s.reshape(bd, 1, IDX_DIM), wi_s, topk_s)
        attn_s = _sample_attn(page_table, cache_k[l].reshape(n_pool, PAGE_SIZE, ATTN_W),
                              cache_v[l].reshape(n_pool, PAGE_SIZE, ATTN_W), q_s.reshape(bd, 1, ATTN_W), bias,
                              bnew.reshape(bd, 1, 1), k_s.reshape(bd, 1, ATTN_W), v_s.reshape(bd, 1, ATTN_W),
                              pages_att)
        h1s, gate_s = _merge(hs, conv_s, attn_s.reshape(bd, ATTN_W), mem_s.reshape(bd, MEM_W), merge_w, bd, alpha)
        hs = _moe(h1s, gate_s, wu, wd, g2, b2, bd, alpha)
        for lst, val in zip(outs[6:], (k_s.reshape(bd, t_new, N_HEADS, HEAD_DIM),
                                       v_s.reshape(bd, t_new, N_HEADS, HEAD_DIM), ki_s.reshape(bd, t_new, IDX_DIM),
                                       jnp.stack([st[:, 1, :], z_s], axis=1))):
            lst.append(val)

    return (hp, hs.reshape(bd, t_new, d)) + tuple(jnp.stack(o) for o in outs)
```

```python
import functools

import jax
import jax.numpy as jnp
from jax import lax
from jax.experimental import pallas as pl
from jax.experimental.pallas import tpu as pltpu

F32 = jnp.float32
BF16 = jnp.bfloat16
I32 = jnp.int32

HEAD_DIM = 64
HALF = HEAD_DIM // 2
N_HEADS = 8
ATTN_W = N_HEADS * HEAD_DIM
N_IDX_HEADS = 8
IDX_DIM = 64
TOPK_MAX = 256
N_MEM_HEADS = 4
MEM_W = N_MEM_HEADS * HEAD_DIM
CONV_W = 256
CONV_K = 3
N_BRANCH = 3
N_GROUPS = 4
EXPERTS_PER_GROUP = 8
N_EXPERTS = N_GROUPS * EXPERTS_PER_GROUP
PAGE_SIZE = 128
ROPE_THETA = 10000.0
LN_EPS = 1e-5
NEG = -1e30
LANES = 128

C_U, C_B, C_C = 0, CONV_W, 2 * CONV_W
C_Q = 3 * CONV_W
C_V = C_Q + ATTN_W
C_QI = C_V + ATTN_W
C_QM = C_QI + N_IDX_HEADS * IDX_DIM
C_WI = C_QM + MEM_W
C_END = C_WI + LANES
R_K = 0
R_V = R_K + ATTN_W
R_KI = R_V + ATTN_W
R_END = R_KI + IDX_DIM

INT_MIN = -(2 ** 31)
KEY_OF_NEG_INF = -2139095041
BIG_IDX = 2 ** 30
FIELD = 10
VMEM_LIMIT = 56 * 1024 * 1024

_NT = (((1,), (1,)), ((), ()))


def _dot(a, b):
    return jnp.dot(a, b, preferred_element_type=F32)


def _dot_nt(a, b):
    return lax.dot_general(a, b, _NT, preferred_element_type=F32)


def _params(*sem):
    return pltpu.CompilerParams(dimension_semantics=sem, vmem_limit_bytes=VMEM_LIMIT)


def _rope(x, cos, sin):
    w = x.shape[1]
    lane = lax.broadcasted_iota(I32, x.shape, 1)
    swapped = jnp.where((lane & 63) < 32, pltpu.roll(x, w - 32, 1), pltpu.roll(x, 32, 1))
    reps = w // LANES
    if reps > 1:
        cos = jnp.concatenate([cos] * reps, axis=1)
        sin = jnp.concatenate([sin] * reps, axis=1)
    return x * cos + swapped * sin


def _rope_t(xt, cos_t, sin_t):
    x1, x2 = xt[:HALF, :], xt[HALF:, :]
    return x1 * cos_t - x2 * sin_t, x1 * sin_t + x2 * cos_t


def _order_key(x):
    b = lax.bitcast_convert_type(x, I32)
    return b ^ ((b >> 31) & 0x7FFFFFFF)


def _layernorm(x, g, b):
    mu = jnp.mean(x, axis=-1, keepdims=True)
    xc = x - mu
    var = jnp.mean(xc * xc, axis=-1, keepdims=True)
    return xc * lax.rsqrt(var + LN_EPS) * g + b


def _softmax_rows(s):
    m = jnp.max(s, axis=-1, keepdims=True)
    e = jnp.exp(s - m)
    return e / jnp.sum(e, axis=-1, keepdims=True)


def _memkv_kernel(mem_ref, wk_ref, wv_ref, mk_ref, mv_ref):
    m = mem_ref[0].astype(BF16)
    mk_ref[0] = _dot(m, wk_ref[...])
    mv_ref[0] = _dot(m, wv_ref[...])


def _memkv(mem, wk, wv):
    b, n_mem, d = mem.shape
    return pl.pallas_call(
        _memkv_kernel,
        grid=(b,),
        in_specs=[pl.BlockSpec((1, n_mem, d), lambda i: (i, 0, 0)),
                  pl.BlockSpec((d, MEM_W), lambda i: (0, 0)),
                  pl.BlockSpec((d, MEM_W), lambda i: (0, 0))],
        out_specs=[pl.BlockSpec((1, n_mem, MEM_W), lambda i: (i, 0, 0))] * 2,
        out_shape=[jax.ShapeDtypeStruct((b, n_mem, MEM_W), F32)] * 2,
        compiler_params=_params("parallel"),
        name="memkv",
    )(mem, wk, wv)


def _inproj_kernel(x_ref, w_ref, wt_ref, cos_ref, sin_ref, cost_ref, sint_ref, cw_ref, mk_ref, mv_ref,
                   conv_ref, mem_ref, q_ref, kt_ref, vt_ref, ktb_ref, vh_ref, qi_ref, kit_ref, kitb_ref, wi_ref,
                   cnew_ref, zprev_ref):
    j = pl.program_id(1)
    tm = x_ref.shape[1]
    kc = ktb_ref.shape[4]
    xb = x_ref[0].astype(BF16)
    cos = cos_ref[...]
    sin = sin_ref[...]
    cos_t = cost_ref[...]
    sin_t = sint_ref[...]

    def proj(a, b):
        return _dot(xb, w_ref[:, a:b])

    def proj_t(a, b):
        return _dot_nt(wt_ref[a:b, :], xb)

    @pl.when(j == 0)
    def _():
        zprev_ref[...] = jnp.zeros_like(zprev_ref)

    ucc = proj(C_U, C_Q)
    z = ucc[:, C_C:C_Q] * ucc[:, C_U:C_B]
    prev = zprev_ref[...]
    row = lax.broadcasted_iota(I32, z.shape, 0)
    z1 = jnp.where(row == 0, prev[7:8, :], pltpu.roll(z, 1, 0))
    z2 = jnp.where(row == 0, prev[6:7, :], jnp.where(row == 1, prev[7:8, :], pltpu.roll(z, 2, 0)))
    cw = cw_ref[...]
    y = z2 * cw[0:1, :] + z1 * cw[1:2, :] + z * cw[2:3, :]
    conv_ref[0] = (ucc[:, C_B:C_C] * y).astype(BF16)
    zprev_ref[...] = z[tm - 8:tm, :]
    cnew_ref[0] = z[tm - 2:tm, :]

    q = _rope(proj(C_Q, C_V), cos, sin) * (HEAD_DIM ** -0.5)
    v = proj(C_V, C_QI)
    qi = _rope(proj(C_QI, C_QM), cos, sin) * (IDX_DIM ** -0.5)
    for h in range(N_HEADS):
        sl = slice(h * HEAD_DIM, (h + 1) * HEAD_DIM)
        q_ref[0, h] = q[:, sl].astype(BF16)
        vh_ref[0, h] = v[:, sl].astype(BF16)
        qi_ref[0, h] = qi[:, sl].astype(BF16)
    wi_ref[0] = proj(C_WI, C_END)[:, :N_IDX_HEADS] * (N_IDX_HEADS ** -0.5)

    kt = proj_t(R_K, R_V)
    vt = proj_t(R_V, R_KI)
    for h in range(N_HEADS):
        vt_ref[0, h] = vt[h * HEAD_DIM:(h + 1) * HEAD_DIM, :]
        o1, o2 = _rope_t(kt[h * HEAD_DIM:(h + 1) * HEAD_DIM, :], cos_t, sin_t)
        kt_ref[0, h, :HALF, :] = o1
        kt_ref[0, h, HALF:, :] = o2
        for cc in range(tm // kc):
            cs = slice(cc * kc, (cc + 1) * kc)
            ktb_ref[0, h, cc, :HALF, :] = o1[:, cs].astype(BF16)
            ktb_ref[0, h, cc, HALF:, :] = o2[:, cs].astype(BF16)
    o1, o2 = _rope_t(proj_t(R_KI, R_END), cos_t, sin_t)
    kit_ref[0, :HALF, :] = o1
    kit_ref[0, HALF:, :] = o2
    for cc in range(tm // kc):
        cs = slice(cc * kc, (cc + 1) * kc)
        kitb_ref[0, cc, :HALF, :] = o1[:, cs].astype(BF16)
        kitb_ref[0, cc, HALF:, :] = o2[:, cs].astype(BF16)

    qm = (proj(C_QM, C_WI) * (HEAD_DIM ** -0.5)).astype(BF16)
    mk = mk_ref[0].astype(BF16)
    mv = mv_ref[0].astype(BF16)
    outs = []
    for h in range(N_MEM_HEADS):
        sl = slice(h * HEAD_DIM, (h + 1) * HEAD_DIM)
        p = _softmax_rows(_dot_nt(qm[:, sl], mk[:, sl]))
        outs.append(_dot(p.astype(BF16), mv[:, sl]))
    mem_ref[0] = jnp.concatenate(outs, axis=1).astype(BF16)


def _inproj(x, wcat, wt, cos, sin, cos_t, sin_t, conv_w, mk, mv, tm, kc):
    b, t, d = x.shape
    n_mem = mk.shape[1]
    grid = (b, t // tm)
    tok = lambda w: pl.BlockSpec((1, tm, w), lambda i, j: (i, j, 0))
    hm = pl.BlockSpec((1, N_HEADS, tm, HEAD_DIM), lambda i, j: (i, 0, j, 0))
    hm_t = pl.BlockSpec((1, N_HEADS, HEAD_DIM, tm), lambda i, j: (i, 0, 0, j))
    const2 = lambda r, c: pl.BlockSpec((r, c), lambda i, j: (0, 0))
    hm_shape = jax.ShapeDtypeStruct((b, N_HEADS, t, HEAD_DIM), BF16)
    hm_t_shape = jax.ShapeDtypeStruct((b, N_HEADS, HEAD_DIM, t), F32)
    out_shape = [
        jax.ShapeDtypeStruct((b, t, CONV_W), BF16),
        jax.ShapeDtypeStruct((b, t, MEM_W), BF16),
        hm_shape,
        hm_t_shape, hm_t_shape,
        jax.ShapeDtypeStruct((b, N_HEADS, t // kc, HEAD_DIM, kc), BF16),
        hm_shape, hm_shape,
        jax.ShapeDtypeStruct((b, IDX_DIM, t), F32),
        jax.ShapeDtypeStruct((b, t // kc, IDX_DIM, kc), BF16),
        jax.ShapeDtypeStruct((b, t, N_IDX_HEADS), F32),
        jax.ShapeDtypeStruct((b, CONV_K - 1, CONV_W), F32),
    ]
    out_specs = [tok(CONV_W), tok(MEM_W), hm, hm_t, hm_t,
                 pl.BlockSpec((1, N_HEADS, tm // kc, HEAD_DIM, kc), lambda i, j: (i, 0, j, 0, 0)),
                 hm, hm,
                 pl.BlockSpec((1, IDX_DIM, tm), lambda i, j: (i, 0, j)),
                 pl.BlockSpec((1, tm // kc, IDX_DIM, kc), lambda i, j: (i, j, 0, 0)),
                 tok(N_IDX_HEADS), pl.BlockSpec((1, CONV_K - 1, CONV_W), lambda i, j: (i, 0, 0))]
    return pl.pallas_call(
        _inproj_kernel,
        grid=grid,
        in_specs=[tok(d), const2(d, C_END), const2(R_END, d),
                  pl.BlockSpec((tm, LANES), lambda i, j: (j, 0)), pl.BlockSpec((tm, LANES), lambda i, j: (j, 0)),
                  pl.BlockSpec((HALF, tm), lambda i, j: (0, j)), pl.BlockSpec((HALF, tm), lambda i, j: (0, j)),
                  const2(CONV_K, CONV_W),
                  pl.BlockSpec((1, n_mem, MEM_W), lambda i, j: (i, 0, 0)),
                  pl.BlockSpec((1, n_mem, MEM_W), lambda i, j: (i, 0, 0))],
        out_specs=out_specs,
        out_shape=out_shape,
        scratch_shapes=[pltpu.VMEM((8, CONV_W), F32)],
        compiler_params=_params("parallel", "arbitrary"),
        name="inproj",
    )(x, wcat, wt, cos, sin, cos_t, sin_t, conv_w, mk, mv)


def _dsa_kernel(qi_ref, wi_ref, kit_ref, q_ref, kt_ref, v_ref, o_ref,
                key_ref, thr_ref, j_ref, m_ref, l_ref, acc_ref, *, topk, rows_per_pass):
    tq = q_ref.shape[2]
    kc = key_ref.shape[2]
    i = pl.program_id(1)
    nk = i + 1
    t0 = i * tq
    w = wi_ref[0]
    row = lax.broadcasted_iota(I32, (tq, kc), 0) + t0
    col = lax.broadcasted_iota(I32, (tq, kc), 1)

    def score_body(c, carry):
        kic = kit_ref[0, c]
        acc = jnp.zeros((tq, kc), F32)
        for h in range(N_IDX_HEADS):
            acc = acc + w[:, h:h + 1] * jnp.maximum(_dot(qi_ref[0, h], kic), 0.0)
        acc = jnp.where(col + c * kc <= row, acc, -jnp.inf)
        key_ref[c] = _order_key(acc)
        return carry

    lax.fori_loop(0, nk, score_body, 0)

    rp = rows_per_pass
    nb = tq // rp
    colr = lax.broadcasted_iota(I32, (rp, kc), 1)

    def count(preds):
        cnts = []
        for blk in range(nb):
            def body(c, cnt, r0=blk * rp, pred=preds[blk]):
                one = jnp.where(pred(key_ref[c, r0:r0 + rp, :], colr + c * kc), 1.0, 0.0)
                for s in range(kc // LANES):
                    cnt = cnt + one[:, s * LANES:(s + 1) * LANES]
                return cnt
            cnts.append(lax.fori_loop(0, nk, body, jnp.zeros((rp, LANES), F32)))
        return [jnp.sum(cnt, axis=1, keepdims=True) for cnt in cnts]

    def count3(cands):
        accs = []
        for blk in range(nb):
            def body(c, acc, r0=blk * rp, cand=cands[blk]):
                kk = key_ref[c, r0:r0 + rp, :]
                v = jnp.where(kk >= cand[2], 1 + (1 << FIELD) + (1 << 2 * FIELD),
                              jnp.where(kk >= cand[1], 1 + (1 << FIELD), jnp.where(kk >= cand[0], 1, 0)))
                for s in range(kc // LANES):
                    acc = acc + v[:, s * LANES:(s + 1) * LANES]
                return acc
            accs.append(lax.fori_loop(0, nk, body, jnp.zeros((rp, LANES), I32)))
        mask = (1 << FIELD) - 1
        return [[jnp.sum(f.astype(F32), axis=1, keepdims=True)
                 for f in (acc & mask, (acc >> FIELD) & mask, acc >> 2 * FIELD)] for acc in accs]

    def bit_body(it, carry):
        hi = lax.shift_left(jnp.int32(1), 31 - 2 * it)
        lo = lax.shift_left(jnp.int32(1), 30 - 2 * it)
        cands = [(thr ^ lo, thr ^ hi, thr ^ (hi | lo)) for thr, _ in carry]
        cnts = count3(cands)
        out = []
        for (thr, cnt_thr), cand, cnt in zip(carry, cands, cnts):
            for cj, nj in zip(cand, cnt):
                ok = nj >= topk
                thr = jnp.where(ok, cj, thr)
                cnt_thr = jnp.where(ok, nj, cnt_thr)
            out.append((thr, cnt_thr))
        return tuple(out)

    found = lax.fori_loop(
        0, 16, bit_body,
        tuple((jnp.full((rp, 1), INT_MIN, I32), jnp.full((rp, 1), float(topk), F32)) for _ in range(nb)))
    thrs = [thr for thr, _ in found]

    j_ref[...] = jnp.full_like(j_ref, BIG_IDX)
    tie_rows = [jnp.where((thr > KEY_OF_NEG_INF) & (cnt_thr > topk), 1.0, 0.0) for thr, cnt_thr in found]

    @pl.when(jnp.max(functools.reduce(jnp.maximum, tie_rows)) > 0.0)
    def _():
        gts = count([lambda kk, idx, thr=thr: kk > thr for thr in thrs])
        needs = [topk - gt for gt in gts]

        def idx_body(it, jjs):
            bit = lax.shift_left(jnp.int32(1), 29 - it)
            cands = [jj | bit for jj in jjs]
            belows = count([lambda kk, idx, thr=thr, cand=cand: (kk == thr) & (idx < cand)
                            for thr, cand in zip(thrs, cands)])
            return tuple(jnp.where(below < need, cand, jj)
                         for below, need, cand, jj in zip(belows, needs, cands, jjs))

        jjs = lax.fori_loop(0, 30, idx_body, tuple(jnp.zeros((rp, 1), I32) for _ in range(nb)))
        for blk in range(nb):
            j_ref[blk * rp:(blk + 1) * rp, :] = jnp.broadcast_to(jjs[blk], (rp, LANES))

    for blk in range(nb):
        thr_ref[blk * rp:(blk + 1) * rp, :] = jnp.broadcast_to(thrs[blk], (rp, LANES))

    m_ref[...] = jnp.full_like(m_ref, NEG)
    l_ref[...] = jnp.zeros_like(l_ref)
    acc_ref[...] = jnp.zeros_like(acc_ref)
    row_l = lax.broadcasted_iota(I32, (tq, LANES), 0) + t0
    col_l = lax.broadcasted_iota(I32, (tq, LANES), 1)
    n_sub = kc // LANES

    def att_body(c, carry):
        thr = thr_ref[...]
        jlast = j_ref[...]
        biases = []
        for s in range(n_sub):
            kk = key_ref[c, :, s * LANES:(s + 1) * LANES]
            idx = col_l + (c * kc + s * LANES)
            sel = ((kk > thr) | ((kk == thr) & (idx <= jlast))) & (idx <= row_l)
            biases.append(jnp.where(sel, 0.0, NEG))
        ks = pl.ds(pl.multiple_of(c * kc, kc), kc)
        for h in range(N_HEADS):
            sc = _dot(q_ref[0, h], kt_ref[0, h, c])
            parts = [sc[:, s * LANES:(s + 1) * LANES] + biases[s] for s in range(n_sub)]
            m_old = m_ref[h]
            m_new = jnp.maximum(m_old, jnp.max(functools.reduce(jnp.maximum, parts), axis=1, keepdims=True))
            alpha = jnp.exp(m_old - m_new)
            ps = [jnp.exp(part - m_new) for part in parts]
            l_ref[h] = alpha * l_ref[h] + functools.reduce(jnp.add, ps)
            pb = jnp.concatenate([p.astype(BF16) for p in ps], axis=1)
            acc_ref[h] = alpha[:, :HEAD_DIM] * acc_ref[h] + _dot(pb, v_ref[0, h, ks, :])
            m_ref[h] = m_new
        return carry

    lax.fori_loop(0, nk, att_body, 0)
    for h in range(N_HEADS):
        l = jnp.sum(l_ref[h], axis=1, keepdims=True)
        o_ref[0, :, h * HEAD_DIM:(h + 1) * HEAD_DIM] = (acc_ref[h] / l).astype(BF16)


def _dsa(qi, wi, kit, q, kt, vh, tq, topk):
    b, _, t, _ = q.shape
    nq = t // tq
    rp = _largest_divisor(tq, 128)
    assert tq % LANES == 0 and (tq // LANES) * nq < (1 << FIELD), "per-lane key counts must fit a packed field"
    hm_q = pl.BlockSpec((1, N_HEADS, tq, HEAD_DIM), lambda i, j: (i, 0, j, 0))
    return pl.pallas_call(
        functools.partial(_dsa_kernel, topk=topk, rows_per_pass=rp),
        grid=(b, nq),
        in_specs=[hm_q,
                  pl.BlockSpec((1, tq, N_IDX_HEADS), lambda i, j: (i, j, 0)),
                  pl.BlockSpec((1, nq, IDX_DIM, tq), lambda i, j: (i, 0, 0, 0)),
                  hm_q,
                  pl.BlockSpec((1, N_HEADS, nq, HEAD_DIM, tq), lambda i, j: (i, 0, 0, 0, 0)),
                  pl.BlockSpec((1, N_HEADS, t, HEAD_DIM), lambda i, j: (i, 0, 0, 0))],
        out_specs=pl.BlockSpec((1, tq, ATTN_W), lambda i, j: (i, j, 0)),
        out_shape=jax.ShapeDtypeStruct((b, t, ATTN_W), BF16),
        scratch_shapes=[pltpu.VMEM((nq, tq, tq), I32),
                        pltpu.VMEM((tq, LANES), I32), pltpu.VMEM((tq, LANES), I32),
                        pltpu.VMEM((N_HEADS, tq, LANES), F32), pltpu.VMEM((N_HEADS, tq, LANES), F32),
                        pltpu.VMEM((N_HEADS, tq, HEAD_DIM), F32)],
        compiler_params=_params("parallel", "arbitrary"),
        name="dsa_prompt",
    )(qi, wi, kit, q, kt, vh)


def _merge_kernel(x_ref, conv_ref, attn_ref, mem_ref, wg_ref, bg_ref, wc_ref, wa_ref, wm_ref, wo_ref,
                  g1_ref, b1_ref, wrh_ref, wrl_ref, br_ref, h_ref, gate_ref, *, alpha):
    x = x_ref[...]
    xb = x.astype(BF16)
    d = x.shape[1]
    m = None
    for br, (src, wref) in enumerate(((conv_ref, wc_ref), (attn_ref, wa_ref), (mem_ref, wm_ref))):
        g = jax.nn.sigmoid(_dot(xb, wg_ref[:, br * d:(br + 1) * d]) + bg_ref[:, br * d:(br + 1) * d])
        term = g * _dot(src[...], wref[...])
        m = term if m is None else m + term
    h = _layernorm(alpha * x + _dot(m.astype(BF16), wo_ref[...]), g1_ref[...], b1_ref[...])
    h_ref[...] = h

    hh = h.astype(BF16)
    hl = (h - hh.astype(F32)).astype(BF16)
    lo = _dot(hh, wrh_ref[...]) + _dot(hl, wrh_ref[...]) + _dot(hh, wrl_ref[...]) + br_ref[...]
    lane = lax.broadcasted_iota(I32, lo.shape, 1)
    lanef = lane.astype(F32)
    is_g = (lane >= N_EXPERTS) & (lane < N_EXPERTS + N_GROUPS)
    mg = jnp.max(jnp.where(is_g, lo, -jnp.inf), axis=1, keepdims=True)
    sg = jnp.sum(jnp.where(is_g, jnp.exp(lo - mg), 0.0), axis=1, keepdims=True)
    gw = 1.0 / sg
    gsel = jnp.min(jnp.where(is_g & (lo == mg), lanef, 1e9), axis=1, keepdims=True) - N_EXPERTS
    in_g = (lane < N_EXPERTS) & ((lane >> 3).astype(F32) == gsel)
    me = jnp.max(jnp.where(in_g, lo, -jnp.inf), axis=1, keepdims=True)
    ee = jnp.where(in_g, jnp.exp(lo - me), 0.0)
    pe = jnp.where(in_g, ee / jnp.sum(ee, axis=1, keepdims=True), -1.0)
    p1 = jnp.max(pe, axis=1, keepdims=True)
    i1 = jnp.min(jnp.where(pe == p1, lanef, 1e9), axis=1, keepdims=True)
    pe2 = jnp.where(lanef == i1, -1.0, pe)
    p2 = jnp.max(pe2, axis=1, keepdims=True)
    i2 = jnp.min(jnp.where(pe2 == p2, lanef, 1e9), axis=1, keepdims=True)
    nrm = p1 + p2
    gate = gw * jnp.where(lanef == i1, p1 / nrm, jnp.where(lanef == i2, p2 / nrm, 0.0))
    gate_ref[...] = gate[:, :N_EXPERTS]


def _merge(x, conv, attn, mem, wts, tm, alpha):
    n, d = x.shape
    tok = lambda w: pl.BlockSpec((tm, w), lambda i: (i, 0))
    const = lambda a: pl.BlockSpec(a.shape, lambda i: (0, 0))
    return pl.pallas_call(
        functools.partial(_merge_kernel, alpha=alpha),
        grid=(n // tm,),
        in_specs=[tok(d), tok(CONV_W), tok(ATTN_W), tok(MEM_W)] + [const(a) for a in wts],
        out_specs=[tok(d), tok(N_EXPERTS)],
        out_shape=[jax.ShapeDtypeStruct((n, d), F32), jax.ShapeDtypeStruct((n, N_EXPERTS), F32)],
        compiler_params=_params("parallel"),
        name="merge_ln_router",
    )(x, conv, attn, mem, *wts)


def _moe_kernel(h_ref, gate_ref, wu_ref, wd_ref, g2_ref, b2_ref, o_ref, hb_ref, y_ref, *, alpha):
    e = pl.program_id(1)
    de = wd_ref.shape[1]

    @pl.when(e == 0)
    def _():
        hb_ref[...] = h_ref[...].astype(BF16)
        y_ref[...] = jnp.zeros_like(y_ref)

    hu = _dot(hb_ref[...], wu_ref[0])
    a = hu[:, :de]
    act = (a * (1.0 / (1.0 + jnp.exp(-a))) * hu[:, de:]).astype(BF16)
    gate = gate_ref[...]
    lane = lax.broadcasted_iota(I32, gate.shape, 1)
    ge = jnp.sum(jnp.where(lane == e, gate, 0.0), axis=1, keepdims=True)
    y_ref[...] += ge * _dot(act, wd_ref[0])

    @pl.when(e == pl.num_programs(1) - 1)
    def _():
        o_ref[...] = _layernorm(alpha * h_ref[...] + y_ref[...], g2_ref[...], b2_ref[...])


def _moe(h, gate, wu, wd, g2, b2, tm, alpha):
    n, d = h.shape
    ne, _, du = wu.shape
    de = wd.shape[1]
    return pl.pallas_call(
        functools.partial(_moe_kernel, alpha=alpha),
        grid=(n // tm, ne),
        in_specs=[pl.BlockSpec((tm, d), lambda i, e: (i, 0)),
                  pl.BlockSpec((tm, ne), lambda i, e: (i, 0)),
                  pl.BlockSpec((1, d, du), lambda i, e: (e, 0, 0)),
                  pl.BlockSpec((1, de, d), lambda i, e: (e, 0, 0)),
                  pl.BlockSpec((1, d), lambda i, e: (0, 0)),
                  pl.BlockSpec((1, d), lambda i, e: (0, 0))],
        out_specs=pl.BlockSpec((tm, d), lambda i, e: (i, 0)),
        out_shape=jax.ShapeDtypeStruct((n, d), F32),
        scratch_shapes=[pltpu.VMEM((tm, d), BF16), pltpu.VMEM((tm, d), F32)],
        compiler_params=_params("parallel", "arbitrary"),
        name="moe_ln",
    )(h, gate, wu, wd, g2, b2)


def _sample_inproj_kernel(x_ref, w_ref, wt_ref, cos_ref, sin_ref, cw_ref, p0_ref, p1_ref,
                          conv_ref, z_ref, q_ref, k_ref, v_ref, qi_ref, ki_ref, wi_ref, qm_ref):
    xb = x_ref[...].astype(BF16)
    cos = cos_ref[...]
    sin = sin_ref[...]

    def proj(a, b):
        return _dot(xb, w_ref[:, a:b])

    def proj_rows(a, b):
        return _dot_nt(xb, wt_ref[a:b, :])

    ucc = proj(C_U, C_Q)
    z = ucc[:, C_C:C_Q] * ucc[:, C_U:C_B]
    cw = cw_ref[...]
    y = p0_ref[...] * cw[0:1, :] + p1_ref[...] * cw[1:2, :] + z * cw[2:3, :]
    conv_ref[...] = (ucc[:, C_B:C_C] * y).astype(BF16)
    z_ref[...] = z
    q_ref[...] = _rope(proj(C_Q, C_V), cos, sin) * (HEAD_DIM ** -0.5)
    k_ref[...] = _rope(proj_rows(R_K, R_V), cos, sin)
    v_ref[...] = proj(C_V, C_QI)
    qi_ref[...] = _rope(proj(C_QI, C_QM), cos, sin) * (IDX_DIM ** -0.5)
    kiw = jnp.concatenate([proj_rows(R_KI, R_END), jnp.zeros((xb.shape[0], LANES - IDX_DIM), F32)], axis=1)
    ki_ref[...] = _rope(kiw, cos, sin)[:, :IDX_DIM]
    wi_ref[...] = proj(C_WI, C_END)[:, :N_IDX_HEADS] * (N_IDX_HEADS ** -0.5)
    qm_ref[...] = proj(C_QM, C_WI) * (HEAD_DIM ** -0.5)


def _sample_inproj(x, wcat, wt, cos, sin, conv_w, p0, p1):
    n, d = x.shape
    args = (x, wcat, wt, cos, sin, conv_w, p0, p1)
    widths = (CONV_W, CONV_W, ATTN_W, ATTN_W, ATTN_W, N_IDX_HEADS * IDX_DIM, IDX_DIM, N_IDX_HEADS, MEM_W)
    dtypes = (BF16,) + (F32,) * 8
    full = lambda shape: pl.BlockSpec(shape, lambda i: (0,) * len(shape))
    return pl.pallas_call(
        _sample_inproj_kernel,
        grid=(1,),
        in_specs=[full(a.shape) for a in args],
        out_specs=[full((n, w)) for w in widths],
        out_shape=[jax.ShapeDtypeStruct((n, w), dt) for w, dt in zip(widths, dtypes)],
        compiler_params=_params("arbitrary"),
        name="sample_inproj",
    )(*args)


def _head_mask(rows, width):
    r = lax.broadcasted_iota(I32, (rows, width), 0)
    c = lax.broadcasted_iota(I32, (rows, width), 1)
    return (c >> 6) == r


def _sample_mem_kernel(qm_ref, mk_ref, mv_ref, o_ref):
    hmask = _head_mask(8, MEM_W)
    qbd = jnp.where(hmask, jnp.broadcast_to(qm_ref[0], (8, MEM_W)), 0.0).astype(BF16)
    p = _softmax_rows(_dot_nt(qbd, mk_ref[0].astype(BF16)))
    o = _dot(p.astype(BF16), mv_ref[0].astype(BF16))
    o_ref[0] = jnp.sum(jnp.where(hmask, o, 0.0), axis=0, keepdims=True).astype(BF16)


def _sample_mem(qm, mk, mv):
    n, n_mem, w = mk.shape
    return pl.pallas_call(
        _sample_mem_kernel,
        grid=(n,),
        in_specs=[pl.BlockSpec((1, 1, w), lambda i: (i, 0, 0)),
                  pl.BlockSpec((1, n_mem, w), lambda i: (i, 0, 0)),
                  pl.BlockSpec((1, n_mem, w), lambda i: (i, 0, 0))],
        out_specs=pl.BlockSpec((1, 1, w), lambda i: (i, 0, 0)),
        out_shape=jax.ShapeDtypeStruct((n, 1, w), BF16),
        compiler_params=_params("parallel"),
        name="sample_mem_attn",
    )(qm, mk, mv)


def _sample_scores_kernel(pt_ref, *refs, pages):
    page_refs = refs[:pages]
    qi_ref, wi_ref, o_ref = refs[pages:]
    keys_t = jnp.concatenate([r[0] for r in page_refs], axis=1).astype(BF16)
    s = _dot(qi_ref[0].astype(BF16), keys_t)
    o_ref[0, 0] = jnp.sum(wi_ref[0] * jnp.maximum(s, 0.0), axis=0, keepdims=True)


def _sample_scores(page_table, cki_t, qi, wi, pages):
    n, n_pages = page_table.shape
    ng = n_pages // pages
    width = pages * PAGE_SIZE

    def page_spec(j):
        return pl.BlockSpec((1, IDX_DIM, PAGE_SIZE), lambda s, pt: (pt[s // ng, (s % ng) * pages + j], 0, 0))

    grid_spec = pltpu.PrefetchScalarGridSpec(
        num_scalar_prefetch=1,
        grid=(n * ng,),
        in_specs=[page_spec(j) for j in range(pages)] + [
            pl.BlockSpec((1, N_IDX_HEADS, IDX_DIM), lambda s, pt: (s // ng, 0, 0)),
            pl.BlockSpec((1, N_IDX_HEADS, 1), lambda s, pt: (s // ng, 0, 0))],
        out_specs=pl.BlockSpec((1, 1, 1, width), lambda s, pt: (s // ng, s % ng, 0, 0)),
    )
    out = pl.pallas_call(
        functools.partial(_sample_scores_kernel, pages=pages),
        grid_spec=grid_spec,
        out_shape=jax.ShapeDtypeStruct((n, ng, 1, width), F32),
        compiler_params=_params("arbitrary"),
        name="sample_idx_scores",
    )(page_table, *([cki_t] * pages), qi, wi)
    return out.reshape(n, n_pages * PAGE_SIZE)


def _sample_select_kernel(sc_ref, qi_ref, ki_ref, wi_ref, bias_ref, bnew_ref, key_ref, j_ref, *, topk):
    n, past = sc_ref.shape
    qi = qi_ref[...].astype(BF16).astype(F32)
    ki = ki_ref[...].astype(BF16).astype(F32)
    s_new = jnp.maximum(jnp.sum(qi * ki, axis=2), 0.0)
    knew = _order_key(jnp.sum(wi_ref[...] * s_new, axis=1, keepdims=True))
    key_ref[...] = _order_key(sc_ref[...])
    idx = lax.broadcasted_iota(I32, (n, past), 1)

    def count(pred):
        one = jnp.where(pred(key_ref[...], idx), 1.0, 0.0)
        return jnp.sum(one, axis=1, keepdims=True) + jnp.where(pred(knew, past), 1.0, 0.0)

    def bit_body(it, carry):
        thr, cnt_thr = carry
        cand = thr ^ lax.shift_left(jnp.int32(1), 31 - it)
        cnt = count(lambda kk, ii: kk >= cand)
        ok = cnt >= topk
        return jnp.where(ok, cand, thr), jnp.where(ok, cnt, cnt_thr)

    thr, cnt_thr = lax.fori_loop(
        0, 32, bit_body, (jnp.full((n, 1), INT_MIN, I32), jnp.full((n, 1), float(topk), F32)))
    need = topk - count(lambda kk, ii: kk > thr)

    def idx_body(it, jj):
        cand = jj | lax.shift_left(jnp.int32(1), 29 - it)
        below = count(lambda kk, ii: (kk == thr) & (ii < cand))
        return jnp.where(below < need, cand, jj)

    j_ref[...] = jnp.full((n, 1), BIG_IDX, I32)

    @pl.when(jnp.max(jnp.where(cnt_thr > topk, 1.0, 0.0)) > 0.0)
    def _():
        j_ref[...] = lax.fori_loop(0, 30, idx_body, jnp.zeros((n, 1), I32))

    jlast = j_ref[...]
    kk = key_ref[...]
    bias_ref[...] = jnp.where((kk > thr) | ((kk == thr) & (idx <= jlast)), 0.0, NEG)
    bnew_ref[...] = jnp.where((knew > thr) | ((knew == thr) & (past <= jlast)), 0.0, NEG)


def _sample_select(scores, qi, ki, wi, topk):
    n, past = scores.shape
    args = (scores, qi, ki, wi)
    full = lambda shape: pl.BlockSpec(shape, lambda i: (0,) * len(shape))
    return pl.pallas_call(
        functools.partial(_sample_select_kernel, topk=topk),
        grid=(1,),
        in_specs=[full(a.shape) for a in args],
        out_specs=[full((n, past)), full((n, 1))],
        out_shape=[jax.ShapeDtypeStruct((n, past), F32), jax.ShapeDtypeStruct((n, 1), F32)],
        scratch_shapes=[pltpu.VMEM((n, past), I32), pltpu.VMEM((n, 1), I32)],
        compiler_params=_params("arbitrary"),
        name="sample_topk_select",
    )(*args)


def _sample_attn_kernel(pt_ref, *refs, pages, nch):
    k_refs = refs[:pages]
    v_refs = refs[pages:2 * pages]
    q_ref, bias_ref, bnew_ref, kn_ref, vn_ref, o_ref, m_ref, l_ref, acc_ref = refs[2 * pages:]
    c = pl.program_id(0) % nch

    @pl.when(c == 0)
    def _():
        m_ref[...] = jnp.full_like(m_ref, NEG)
        l_ref[...] = jnp.zeros_like(l_ref)
        acc_ref[...] = jnp.zeros_like(acc_ref)

    bias = bias_ref[0, 0]
    for h in range(N_HEADS):
        qh = q_ref[0, h]
        s = jnp.concatenate([jnp.sum(r[0, h] * qh, axis=0, keepdims=True) for r in k_refs], axis=1) + bias
        m_old = m_ref[h]
        m_new = jnp.maximum(m_old, jnp.max(s, axis=1, keepdims=True))
        alpha = jnp.exp(m_old - m_new)
        p = jnp.exp(s - m_new)
        l_ref[h] = alpha * l_ref[h] + jnp.sum(p, axis=1, keepdims=True)
        acc = alpha * acc_ref[h]
        for j, r in enumerate(v_refs):
            acc = acc + r[0, h] * p[:, j * PAGE_SIZE:(j + 1) * PAGE_SIZE]
        acc_ref[h] = acc
        m_ref[h] = m_new

    @pl.when(c == nch - 1)
    def _():
        bn = bnew_ref[0]
        for h in range(N_HEADS):
            s_n = jnp.sum(q_ref[0, h] * kn_ref[0, h], axis=0, keepdims=True) + bn
            m0 = m_ref[h]
            m1 = jnp.maximum(m0, s_n)
            a1 = jnp.exp(m0 - m1)
            p_n = jnp.exp(s_n - m1)
            l1 = a1 * l_ref[h] + p_n
            out = a1 * jnp.sum(acc_ref[h], axis=1, keepdims=True) + p_n * vn_ref[0, h]
            o_ref[0, h] = out / l1


def _sample_attn(page_table, ck_t, cv_t, q, bias, bnew, kn, vn, pages):
    n, n_pages = page_table.shape
    nch = n_pages // pages
    width = pages * PAGE_SIZE

    def page_spec(j):
        return pl.BlockSpec((1, N_HEADS, HEAD_DIM, PAGE_SIZE),
                            lambda s, pt: (pt[s // nch, (s % nch) * pages + j], 0, 0, 0))

    per_seq = pl.BlockSpec((1, N_HEADS, HEAD_DIM, 1), lambda s, pt: (s // nch, 0, 0, 0))
    grid_spec = pltpu.PrefetchScalarGridSpec(
        num_scalar_prefetch=1,
        grid=(n * nch,),
        in_specs=[page_spec(j) for j in range(pages)] * 2 + [
            per_seq,
            pl.BlockSpec((1, 1, 1, width), lambda s, pt: (s // nch, s % nch, 0, 0)),
            pl.BlockSpec((1, 1, 1), lambda s, pt: (s // nch, 0, 0)),
            per_seq, per_seq],
        out_specs=per_seq,
        scratch_shapes=[pltpu.VMEM((N_HEADS, 1, 1), F32), pltpu.VMEM((N_HEADS, 1, 1), F32),
                        pltpu.VMEM((N_HEADS, HEAD_DIM, PAGE_SIZE), F32)],
    )
    return pl.pallas_call(
        functools.partial(_sample_attn_kernel, pages=pages, nch=nch),
        grid_spec=grid_spec,
        out_shape=jax.ShapeDtypeStruct((n, N_HEADS, HEAD_DIM, 1), F32),
        compiler_params=_params("arbitrary"),
        name="sample_attn",
    )(page_table, *([ck_t] * pages), *([cv_t] * pages), q, bias.reshape(n, nch, 1, width), bnew, kn, vn)


def _rope_tables(pos):
    inv = ROPE_THETA ** (-jnp.arange(HALF, dtype=F32) / HALF)
    ang = pos.astype(F32)[:, None] * inv[None, :]
    cos = jnp.cos(ang)
    sin = jnp.sin(ang)
    return jnp.concatenate([cos] * 4, axis=1), jnp.concatenate([-sin, sin] * 2, axis=1), cos.T, sin.T


def _largest_divisor(n, cap):
    for c in range(min(cap, n), 0, -1):
        if n % c == 0:
            return c
    return 1


def _pack_layer_weights(l, w_in, b_gate, w_br_conv, w_br_attn, w_br_mem, w_o, ln1_g, ln1_b, w_group, b_group,
                        w_expert, b_expert):
    d = w_in.shape[1]
    sizes = (CONV_W, CONV_W, CONV_W, ATTN_W, ATTN_W, ATTN_W, N_IDX_HEADS * IDX_DIM, IDX_DIM, N_IDX_HEADS, MEM_W,
             N_BRANCH * d)
    offs = [0]
    for s in sizes:
        offs.append(offs[-1] + s)
    col = lambda i: w_in[l][:, offs[i]:offs[i + 1]]
    u, c_b, c_c, q, k, v, qi, ki, wi, qm, gates = (col(i) for i in range(len(sizes)))
    pad = jnp.zeros((d, LANES - N_IDX_HEADS), F32)
    wcat = jnp.concatenate([u, c_b, c_c, q, v, qi, qm, wi, pad], axis=1).astype(BF16)
    wt = jnp.concatenate([k, v, ki], axis=1).T.astype(BF16)
    wr = jnp.concatenate([w_expert[l], w_group[l], jnp.zeros((d, LANES - N_EXPERTS - N_GROUPS), F32)], axis=1)
    wrh = wr.astype(BF16)
    wrl = (wr - wrh.astype(F32)).astype(BF16)
    br = jnp.concatenate([b_expert[l], b_group[l], jnp.zeros((LANES - N_EXPERTS - N_GROUPS,), F32)])[None, :]
    merge_w = (gates.astype(BF16), b_gate[l][None, :], w_br_conv[l].astype(BF16), w_br_attn[l].astype(BF16),
               w_br_mem[l].astype(BF16), w_o[l].astype(BF16), ln1_g[l][None, :], ln1_b[l][None, :], wrh, wrl, br)
    return wcat, wt, merge_w


def kernel(x_prompt, x_sample, mem_prompt, cache_k, cache_v, cache_k_idx, state_conv, cache_mem_k, cache_mem_v,
           page_table, w_in, b_gate, conv_w, w_br_conv, w_br_attn, w_br_mem, w_o, w_mem_k, w_mem_v, ln1_g, ln1_b,
           w_group, b_group, w_expert, b_expert, w_up, w_down, ln2_g, ln2_b):
    depth = w_in.shape[0]
    b, s_len, d = x_prompt.shape
    bd, t_new, _ = x_sample.shape
    assert t_new == 1, "the sample group decodes one token per sequence"
    n_pages = page_table.shape[1]
    past = n_pages * PAGE_SIZE
    alpha = (2.0 * depth) ** 0.25

    tq = _largest_divisor(s_len, 256)
    tm_in = _largest_divisor(s_len, 512)
    assert tm_in % tq == 0
    n_tok = b * s_len
    tm_merge = _largest_divisor(n_tok, 512)
    tm_moe = _largest_divisor(n_tok, 1024)
    topk_p = min(TOPK_MAX, s_len // 4)
    topk_s = min(TOPK_MAX, (past + t_new) // 4)
    pages_idx = _largest_divisor(n_pages, 16)
    pages_att = _largest_divisor(n_pages, 8)

    cos_p, sin_p, cost_p, sint_p = _rope_tables(jnp.arange(s_len, dtype=jnp.int32))
    cos_s, sin_s, _, _ = _rope_tables(past + jnp.arange(t_new, dtype=jnp.int32))

    hp = x_prompt
    hs = x_sample.reshape(bd, d)
    outs = [[] for _ in range(10)]
    for l in range(depth):
        wcat, wt, merge_w = _pack_layer_weights(l, w_in, b_gate, w_br_conv, w_br_attn, w_br_mem, w_o, ln1_g, ln1_b,
                                                w_group, b_group, w_expert, b_expert)
        wu = w_up[l].astype(BF16)
        wd = w_down[l].astype(BF16)
        g2, b2 = ln2_g[l][None, :], ln2_b[l][None, :]

        mk, mv = _memkv(mem_prompt, w_mem_k[l].astype(BF16), w_mem_v[l].astype(BF16))
        (conv_o, mem_o, q_h, kt_p, vt_p, kt_b, v_h, qi_h, kit_p, kit_b, wi_p, conv_new) = _inproj(
            hp, wcat, wt, cos_p, sin_p, cost_p, sint_p, conv_w[l], mk, mv, tm_in, tq)
        attn_o = _dsa(qi_h, wi_p, kit_b, q_h, kt_b, v_h, tq, topk_p)
        h1, gate = _merge(hp.reshape(n_tok, d), conv_o.reshape(n_tok, CONV_W), attn_o.reshape(n_tok, ATTN_W),
                          mem_o.reshape(n_tok, MEM_W), merge_w, tm_merge, alpha)
        hp = _moe(h1, gate, wu, wd, g2, b2, tm_moe, alpha).reshape(b, s_len, d)
        for lst, val in zip(outs[:6], (jnp.transpose(kt_p, (0, 3, 1, 2)), jnp.transpose(vt_p, (0, 3, 1, 2)),
                                       jnp.transpose(kit_p, (0, 2, 1)), conv_new,
                                       mk.reshape(b, -1, N_MEM_HEADS, HEAD_DIM),
                                       mv.reshape(b, -1, N_MEM_HEADS, HEAD_DIM))):
            lst.append(val)

        st = state_conv[l].astype(F32)
        (conv_s, z_s, q_s, k_s, v_s, qi_s, ki_s, wi_s, qm_s) = _sample_inproj(
            hs, wcat, wt, cos_s, sin_s, conv_w[l], st[:, 0, :], st[:, 1, :])
        mem_s = _sample_mem(qm_s.reshape(bd, 1, MEM_W), cache_mem_k[l].reshape(bd, -1, MEM_W),
                            cache_mem_v[l].reshape(bd, -1, MEM_W))
        qi3 = qi_s.reshape(bd, N_IDX_HEADS, IDX_DIM)
        wi3 = wi_s.reshape(bd, N_IDX_HEADS, 1)
        scores = _sample_scores(page_table, jnp.transpose(cache_k_idx[l], (0, 2, 1)), qi3, wi3, pages_idx)
        bias, bnew = _sample_select(scores, qi3, ki_s.reshape(bd, 1, IDX_DIM), wi_s, topk_s)
        col4 = lambda a: a.reshape(bd, N_HEADS, HEAD_DIM, 1)
        attn_s = _sample_attn(page_table, jnp.transpose(cache_k[l], (0, 2, 3, 1)),
                              jnp.transpose(cache_v[l], (0, 2, 3, 1)), col4(q_s), bias, bnew.reshape(bd, 1, 1),
                              col4(k_s), col4(v_s), pages_att)
        h1s, gate_s = _merge(hs, conv_s, attn_s.reshape(bd, ATTN_W).astype(BF16), mem_s.reshape(bd, MEM_W), merge_w,
                             bd, alpha)
        hs = _moe(h1s, gate_s, wu, wd, g2, b2, bd, alpha)
        for lst, val in zip(outs[6:], (k_s.reshape(bd, t_new, N_HEADS, HEAD_DIM),
                                       v_s.reshape(bd, t_new, N_HEADS, HEAD_DIM), ki_s.reshape(bd, t_new, IDX_DIM),
                                       jnp.stack([st[:, 1, :], z_s], axis=1))):
            lst.append(val)

    return (hp, hs.reshape(bd, t_new, d)) + tuple(jnp.stack(o) for o in outs)
```

```python
import functools

import jax
import jax.numpy as jnp
from jax import lax
from jax.experimental import pallas as pl
from jax.experimental.pallas import tpu as pltpu

F32 = jnp.float32
BF16 = jnp.bfloat16
I32 = jnp.int32

HEAD_DIM = 64
HALF = HEAD_DIM // 2
N_HEADS = 8
ATTN_W = N_HEADS * HEAD_DIM
N_IDX_HEADS = 8
IDX_DIM = 64
TOPK_MAX = 256
N_MEM_HEADS = 4
MEM_W = N_MEM_HEADS * HEAD_DIM
CONV_W = 256
CONV_K = 3
N_BRANCH = 3
N_GROUPS = 4
EXPERTS_PER_GROUP = 8
N_EXPERTS = N_GROUPS * EXPERTS_PER_GROUP
PAGE_SIZE = 128
ROPE_THETA = 10000.0
LN_EPS = 1e-5
NEG = -1e30
LANES = 128
SUBLANES = 8

C_U, C_B, C_C = 0, CONV_W, 2 * CONV_W
C_K = 3 * CONV_W
C_KI = C_K + ATTN_W
C_QM = C_KI + LANES
C_END = C_QM + MEM_W
R_Q = 0
R_QI = R_Q + ATTN_W
R_K = R_QI + N_IDX_HEADS * IDX_DIM
R_V = R_K + ATTN_W
R_KI = R_V + ATTN_W
R_WI = R_KI + IDX_DIM
R_END = R_WI + 16

INT_MIN = -(2 ** 31)
KEY_OF_NEG_INF = -2139095041
KEY_OF_POS_INF = 2139095040
BIG_IDX = 2 ** 30
FIELD = 10
Q_SCALE_LOG2 = HEAD_DIM ** -0.5 * 1.4426950408889634
VMEM_LIMIT = 56 * 1024 * 1024

_NT = (((1,), (1,)), ((), ()))


def _dot(a, b):
    return jnp.dot(a, b, preferred_element_type=F32)


def _dot_nt(a, b):
    return lax.dot_general(a, b, _NT, preferred_element_type=F32)


def _params(*sem):
    return pltpu.CompilerParams(dimension_semantics=sem, vmem_limit_bytes=VMEM_LIMIT)


def _rope(x, cos, sin):
    w = x.shape[1]
    lane = lax.broadcasted_iota(I32, x.shape, 1)
    swapped = jnp.where((lane & 63) < 32, pltpu.roll(x, w - 32, 1), pltpu.roll(x, 32, 1))
    reps = w // LANES
    if reps > 1:
        cos = jnp.concatenate([cos] * reps, axis=1)
        sin = jnp.concatenate([sin] * reps, axis=1)
    return x * cos + swapped * sin


def _rope_t(xt, cos_t, sin_t):
    x1, x2 = xt[:HALF, :], xt[HALF:, :]
    return x1 * cos_t - x2 * sin_t, x1 * sin_t + x2 * cos_t


def _order_key(x):
    b = lax.bitcast_convert_type(x, I32)
    return b ^ ((b >> 31) & 0x7FFFFFFF)


def _key_to_float(key):
    key = jnp.clip(key, KEY_OF_NEG_INF, KEY_OF_POS_INF)
    return lax.bitcast_convert_type(key ^ ((key >> 31) & 0x7FFFFFFF), F32)


def _layernorm(x, g, b):
    mu = jnp.mean(x, axis=-1, keepdims=True)
    xc = x - mu
    var = jnp.mean(xc * xc, axis=-1, keepdims=True)
    return xc * lax.rsqrt(var + LN_EPS) * g + b


def _softmax_rows(s):
    m = jnp.max(s, axis=-1, keepdims=True)
    e = jnp.exp(s - m)
    return e / jnp.sum(e, axis=-1, keepdims=True)


def _memkv_kernel(mem_ref, wk_ref, wv_ref, mk_ref, mv_ref):
    m = mem_ref[0].astype(BF16)
    mk_ref[0] = _dot(m, wk_ref[...])
    mv_ref[0] = _dot(m, wv_ref[...])


def _memkv(mem, wk, wv):
    b, n_mem, d = mem.shape
    return pl.pallas_call(
        _memkv_kernel,
        grid=(b,),
        in_specs=[pl.BlockSpec((1, n_mem, d), lambda i: (i, 0, 0)),
                  pl.BlockSpec((d, MEM_W), lambda i: (0, 0)),
                  pl.BlockSpec((d, MEM_W), lambda i: (0, 0))],
        out_specs=[pl.BlockSpec((1, n_mem, MEM_W), lambda i: (i, 0, 0))] * 2,
        out_shape=[jax.ShapeDtypeStruct((b, n_mem, MEM_W), F32)] * 2,
        compiler_params=_params("parallel"),
        name="memkv",
    )(mem, wk, wv)


def _inproj_kernel(x_ref, w_ref, wt_ref, cos_ref, sin_ref, cost_ref, sint_ref, cw_ref, mk_ref, mv_ref,
                   conv_ref, mem_ref, qt_ref, qit_ref, wit_ref, kt_ref, vt_ref, kb_ref, vtb_ref, kit_ref, kib_ref,
                   cnew_ref, zprev_ref):
    j = pl.program_id(1)
    tm = x_ref.shape[1]
    kc = vtb_ref.shape[4]
    xb = x_ref[0].astype(BF16)
    cos = cos_ref[...]
    sin = sin_ref[...]
    cos_t = cost_ref[...]
    sin_t = sint_ref[...]

    def proj(a, b):
        return _dot(xb, w_ref[:, a:b])

    def proj_t(a, b):
        return _dot_nt(wt_ref[a:b, :], xb)

    @pl.when(j == 0)
    def _():
        zprev_ref[...] = jnp.zeros_like(zprev_ref)

    ucc = proj(C_U, C_K)
    z = ucc[:, C_C:C_K] * ucc[:, C_U:C_B]
    prev = zprev_ref[...]
    row = lax.broadcasted_iota(I32, z.shape, 0)
    z1 = jnp.where(row == 0, prev[7:8, :], pltpu.roll(z, 1, 0))
    z2 = jnp.where(row == 0, prev[6:7, :], jnp.where(row == 1, prev[7:8, :], pltpu.roll(z, 2, 0)))
    cw = cw_ref[...]
    y = z2 * cw[0:1, :] + z1 * cw[1:2, :] + z * cw[2:3, :]
    conv_ref[0] = (ucc[:, C_B:C_C] * y).astype(BF16)
    zprev_ref[...] = z[tm - 8:tm, :]
    cnew_ref[0] = z[tm - 2:tm, :]

    k = _rope(proj(C_K, C_KI), cos, sin)
    for h in range(N_HEADS):
        kb_ref[0, h] = k[:, h * HEAD_DIM:(h + 1) * HEAD_DIM].astype(BF16)
    kib_ref[0] = _rope(proj(C_KI, C_QM), cos, sin)[:, :IDX_DIM].astype(BF16)

    qt = proj_t(R_Q, R_QI)
    qit = proj_t(R_QI, R_K)
    kt = proj_t(R_K, R_V)
    vt = proj_t(R_V, R_KI)
    for h in range(N_HEADS):
        hs = slice(h * HEAD_DIM, (h + 1) * HEAD_DIM)
        o1, o2 = _rope_t(qt[hs, :], cos_t, sin_t)
        qt_ref[0, h, :HALF, :] = (o1 * Q_SCALE_LOG2).astype(BF16)
        qt_ref[0, h, HALF:, :] = (o2 * Q_SCALE_LOG2).astype(BF16)
        o1, o2 = _rope_t(qit[hs, :], cos_t, sin_t)
        qit_ref[0, h, :HALF, :] = (o1 * (IDX_DIM ** -0.5)).astype(BF16)
        qit_ref[0, h, HALF:, :] = (o2 * (IDX_DIM ** -0.5)).astype(BF16)
        o1, o2 = _rope_t(kt[hs, :], cos_t, sin_t)
        kt_ref[0, h, :HALF, :] = o1
        kt_ref[0, h, HALF:, :] = o2
        vt_ref[0, h] = vt[hs, :]
        for cc in range(tm // kc):
            vtb_ref[0, h, cc] = vt[hs, cc * kc:(cc + 1) * kc].astype(BF16)
    o1, o2 = _rope_t(proj_t(R_KI, R_WI), cos_t, sin_t)
    kit_ref[0, :HALF, :] = o1
    kit_ref[0, HALF:, :] = o2
    wit_ref[0] = proj_t(R_WI, R_END)[:N_IDX_HEADS, :] * (N_IDX_HEADS ** -0.5)

    qm = (proj(C_QM, C_END) * (HEAD_DIM ** -0.5)).astype(BF16)
    mk = mk_ref[0].astype(BF16)
    mv = mv_ref[0].astype(BF16)
    outs = []
    for h in range(N_MEM_HEADS):
        sl = slice(h * HEAD_DIM, (h + 1) * HEAD_DIM)
        p = _softmax_rows(_dot_nt(qm[:, sl], mk[:, sl]))
        outs.append(_dot(p.astype(BF16), mv[:, sl]))
    mem_ref[0] = jnp.concatenate(outs, axis=1).astype(BF16)


def _inproj(x, wcat, wt, cos, sin, cos_t, sin_t, conv_w, mk, mv, tm, kc):
    b, t, d = x.shape
    n_mem = mk.shape[1]
    grid = (b, t // tm)
    tok = lambda w: pl.BlockSpec((1, tm, w), lambda i, j: (i, j, 0))
    hm_t = pl.BlockSpec((1, N_HEADS, HEAD_DIM, tm), lambda i, j: (i, 0, 0, j))
    const2 = lambda r, c: pl.BlockSpec((r, c), lambda i, j: (0, 0))
    hm_t_shape = lambda dt: jax.ShapeDtypeStruct((b, N_HEADS, HEAD_DIM, t), dt)
    out_shape = [
        jax.ShapeDtypeStruct((b, t, CONV_W), BF16),
        jax.ShapeDtypeStruct((b, t, MEM_W), BF16),
        hm_t_shape(BF16), hm_t_shape(BF16),
        jax.ShapeDtypeStruct((b, N_IDX_HEADS, t), F32),
        hm_t_shape(F32), hm_t_shape(F32),
        jax.ShapeDtypeStruct((b, N_HEADS, t, HEAD_DIM), BF16),
        jax.ShapeDtypeStruct((b, N_HEADS, t // kc, HEAD_DIM, kc), BF16),
        jax.ShapeDtypeStruct((b, IDX_DIM, t), F32),
        jax.ShapeDtypeStruct((b, t, IDX_DIM), BF16),
        jax.ShapeDtypeStruct((b, CONV_K - 1, CONV_W), F32),
    ]
    out_specs = [tok(CONV_W), tok(MEM_W), hm_t, hm_t,
                 pl.BlockSpec((1, N_IDX_HEADS, tm), lambda i, j: (i, 0, j)),
                 hm_t, hm_t,
                 pl.BlockSpec((1, N_HEADS, tm, HEAD_DIM), lambda i, j: (i, 0, j, 0)),
                 pl.BlockSpec((1, N_HEADS, tm // kc, HEAD_DIM, kc), lambda i, j: (i, 0, j, 0, 0)),
                 pl.BlockSpec((1, IDX_DIM, tm), lambda i, j: (i, 0, j)),
                 tok(IDX_DIM),
                 pl.BlockSpec((1, CONV_K - 1, CONV_W), lambda i, j: (i, 0, 0))]
    return pl.pallas_call(
        _inproj_kernel,
        grid=grid,
        in_specs=[tok(d), const2(d, C_END), const2(R_END, d),
                  pl.BlockSpec((tm, LANES), lambda i, j: (j, 0)), pl.BlockSpec((tm, LANES), lambda i, j: (j, 0)),
                  pl.BlockSpec((HALF, tm), lambda i, j: (0, j)), pl.BlockSpec((HALF, tm), lambda i, j: (0, j)),
                  const2(CONV_K, CONV_W),
                  pl.BlockSpec((1, n_mem, MEM_W), lambda i, j: (i, 0, 0)),
                  pl.BlockSpec((1, n_mem, MEM_W), lambda i, j: (i, 0, 0))],
        out_specs=out_specs,
        out_shape=out_shape,
        scratch_shapes=[pltpu.VMEM((8, CONV_W), F32)],
        compiler_params=_params("parallel", "arbitrary"),
        name="inproj",
    )(x, wcat, wt, cos, sin, cos_t, sin_t, conv_w, mk, mv)


def _dsa_kernel(qit_ref, wit_ref, ki_ref, qt_ref, k_ref, vt_ref, o_ref,
                sc_ref, j_ref, m_ref, l_ref, acc_ref, *, topk):
    tq = qt_ref.shape[3]
    kc = sc_ref.shape[1]
    i = pl.program_id(1)
    nk = i + 1
    t0 = i * tq
    wt = wit_ref[0]
    key_pos = lax.broadcasted_iota(I32, (kc, tq), 0)
    q_pos = lax.broadcasted_iota(I32, (kc, tq), 1) + t0

    def keys_of(c):
        return pl.ds(pl.multiple_of(c * kc, kc), kc)

    def score_body(c, carry):
        kic = ki_ref[0, keys_of(c), :]
        acc = jnp.zeros((kc, tq), F32)
        for h in range(N_IDX_HEADS):
            acc = acc + wt[h:h + 1, :] * jnp.maximum(_dot(kic, qit_ref[0, h]), 0.0)
        sc_ref[c] = jnp.where(key_pos + c * kc <= q_pos, acc, -jnp.inf)
        return carry

    lax.fori_loop(0, nk, score_body, 0)

    def fold(v):
        return jnp.sum(v.reshape(kc // SUBLANES, SUBLANES, tq), axis=0)

    def count(pred):
        def body(c, cnt):
            return cnt + fold(jnp.where(pred(sc_ref[c], key_pos + c * kc), 1.0, 0.0))
        cnt = lax.fori_loop(0, nk, body, jnp.zeros((SUBLANES, tq), F32))
        return jnp.sum(cnt, axis=0, keepdims=True)

    def count3(c1, c2, c3):
        def body(c, acc):
            sc = sc_ref[c]
            v = jnp.where(sc >= c3, 1 + (1 << FIELD) + (1 << 2 * FIELD),
                          jnp.where(sc >= c2, 1 + (1 << FIELD), jnp.where(sc >= c1, 1, 0)))
            return acc + fold(v)
        acc = lax.fori_loop(0, nk, body, jnp.zeros((SUBLANES, tq), I32))
        mask = (1 << FIELD) - 1
        return [jnp.sum(f.astype(F32), axis=0, keepdims=True)
                for f in (acc & mask, (acc >> FIELD) & mask, acc >> 2 * FIELD)]

    def bit_body(it, carry):
        thr, cnt_thr = carry
        hi = lax.shift_left(jnp.int32(1), 31 - 2 * it)
        lo = lax.shift_left(jnp.int32(1), 30 - 2 * it)
        cands = (thr ^ lo, thr ^ hi, thr ^ (hi | lo))
        cnts = count3(*[_key_to_float(cj) for cj in cands])
        for cj, nj in zip(cands, cnts):
            ok = nj >= topk
            thr = jnp.where(ok, cj, thr)
            cnt_thr = jnp.where(ok, nj, cnt_thr)
        return thr, cnt_thr

    thr_key, cnt_thr = lax.fori_loop(
        0, 16, bit_body, (jnp.full((1, tq), INT_MIN, I32), jnp.full((1, tq), float(topk), F32)))
    thr = _key_to_float(thr_key)

    j_ref[...] = jnp.full_like(j_ref, BIG_IDX)
    tie_q = jnp.where((thr_key > KEY_OF_NEG_INF) & (cnt_thr > topk), 1.0, 0.0)

    @pl.when(jnp.max(tie_q) > 0.0)
    def _():
        need = topk - count(lambda sc, pos: sc > thr)

        def idx_body(it, jj):
            cand = jj | lax.shift_left(jnp.int32(1), 29 - it)
            below = count(lambda sc, pos: (sc == thr) & (pos < cand))
            return jnp.where(below < need, cand, jj)

        j_ref[...] = lax.fori_loop(0, 30, idx_body, jnp.zeros((1, tq), I32))

    jlast = j_ref[...]

    m_ref[...] = jnp.full_like(m_ref, NEG)
    l_ref[...] = jnp.zeros_like(l_ref)
    acc_ref[...] = jnp.zeros_like(acc_ref)

    def att_body(c, carry):
        sc = sc_ref[c]
        pos = key_pos + c * kc
        sel = ((sc > thr) | ((sc == thr) & (pos <= jlast))) & (pos <= q_pos)
        bias = jnp.where(sel, 0.0, NEG)
        s = jnp.einsum('hkd,hdt->hkt', k_ref[0, :, keys_of(c), :], qt_ref[0],
                       preferred_element_type=F32) + bias[None]
        m_old = m_ref[...]
        m_new = jnp.maximum(m_old, jnp.max(s, axis=1, keepdims=True))
        alpha = jnp.exp2(m_old - m_new)
        p = jnp.exp2(s - m_new)
        l_ref[...] = alpha * l_ref[...] + jnp.sum(p, axis=1, keepdims=True)
        pv = jnp.einsum('hdk,hkt->hdt', vt_ref[0, :, c], p.astype(BF16), preferred_element_type=F32)
        acc_ref[...] = alpha * acc_ref[...] + pv
        m_ref[...] = m_new
        return carry

    lax.fori_loop(0, nk, att_body, 0)
    for h in range(N_HEADS):
        o_ref[0, :, h * HEAD_DIM:(h + 1) * HEAD_DIM] = (acc_ref[h] / l_ref[h]).T.astype(BF16)


def _dsa(qit, wit, kib, qt, kb, vtb, tq, topk):
    b, _, _, t = qt.shape
    nq = t // tq
    assert tq % LANES == 0 and (tq // SUBLANES) * nq < (1 << FIELD), "per-slot key counts must fit a packed field"
    hm_t = pl.BlockSpec((1, N_HEADS, HEAD_DIM, tq), lambda i, j: (i, 0, 0, j))
    return pl.pallas_call(
        functools.partial(_dsa_kernel, topk=topk),
        grid=(b, nq),
        in_specs=[hm_t,
                  pl.BlockSpec((1, N_IDX_HEADS, tq), lambda i, j: (i, 0, j)),
                  pl.BlockSpec((1, t, IDX_DIM), lambda i, j: (i, 0, 0)),
                  hm_t,
                  pl.BlockSpec((1, N_HEADS, t, HEAD_DIM), lambda i, j: (i, 0, 0, 0)),
                  pl.BlockSpec((1, N_HEADS, nq, HEAD_DIM, tq), lambda i, j: (i, 0, 0, 0, 0))],
        out_specs=pl.BlockSpec((1, tq, ATTN_W), lambda i, j: (i, j, 0)),
        out_shape=jax.ShapeDtypeStruct((b, t, ATTN_W), BF16),
        scratch_shapes=[pltpu.VMEM((nq, tq, tq), F32), pltpu.VMEM((1, tq), I32),
                        pltpu.VMEM((N_HEADS, 1, tq), F32), pltpu.VMEM((N_HEADS, 1, tq), F32),
                        pltpu.VMEM((N_HEADS, HEAD_DIM, tq), F32)],
        compiler_params=_params("parallel", "arbitrary"),
        name="dsa_prompt",
    )(qit, wit, kib, qt, kb, vtb)


def _merge_kernel(x_ref, conv_ref, attn_ref, mem_ref, wg_ref, bg_ref, wc_ref, wa_ref, wm_ref, wo_ref,
                  g1_ref, b1_ref, wrh_ref, wrl_ref, br_ref, h_ref, gate_ref, *, alpha):
    x = x_ref[...]
    xb = x.astype(BF16)
    d = x.shape[1]
    cw = 2 * LANES
    blocks = []
    for c0 in range(0, d, cw):
        m = None
        for br, (src, wref) in enumerate(((conv_ref, wc_ref), (attn_ref, wa_ref), (mem_ref, wm_ref))):
            gc = slice(br * d + c0, br * d + c0 + cw)
            g = jax.nn.sigmoid(_dot(xb, wg_ref[:, gc]) + bg_ref[:, gc])
            term = g * _dot(src[...], wref[:, c0:c0 + cw])
            m = term if m is None else m + term
        blocks.append(m.astype(BF16))
    mb = jnp.concatenate(blocks, axis=1)
    h = _layernorm(alpha * x + _dot(mb, wo_ref[...]), g1_ref[...], b1_ref[...])
    h_ref[...] = h

    hh = h.astype(BF16)
    hl = (h - hh.astype(F32)).astype(BF16)
    lo = _dot(hh, wrh_ref[...]) + _dot(hl, wrh_ref[...]) + _dot(hh, wrl_ref[...]) + br_ref[...]
    lane = lax.broadcasted_iota(I32, lo.shape, 1)
    lanef = lane.astype(F32)
    is_g = (lane >= N_EXPERTS) & (lane < N_EXPERTS + N_GROUPS)
    mg = jnp.max(jnp.where(is_g, lo, -jnp.inf), axis=1, keepdims=True)
    sg = jnp.sum(jnp.where(is_g, jnp.exp(lo - mg), 0.0), axis=1, keepdims=True)
    gw = 1.0 / sg
    gsel = jnp.min(jnp.where(is_g & (lo == mg), lanef, 1e9), axis=1, keepdims=True) - N_EXPERTS
    in_g = (lane < N_EXPERTS) & ((lane >> 3).astype(F32) == gsel)
    me = jnp.max(jnp.where(in_g, lo, -jnp.inf), axis=1, keepdims=True)
    ee = jnp.where(in_g, jnp.exp(lo - me), 0.0)
    pe = jnp.where(in_g, ee / jnp.sum(ee, axis=1, keepdims=True), -1.0)
    p1 = jnp.max(pe, axis=1, keepdims=True)
    i1 = jnp.min(jnp.where(pe == p1, lanef, 1e9), axis=1, keepdims=True)
    pe2 = jnp.where(lanef == i1, -1.0, pe)
    p2 = jnp.max(pe2, axis=1, keepdims=True)
    i2 = jnp.min(jnp.where(pe2 == p2, lanef, 1e9), axis=1, keepdims=True)
    nrm = p1 + p2
    gate = gw * jnp.where(lanef == i1, p1 / nrm, jnp.where(lanef == i2, p2 / nrm, 0.0))
    gate_ref[...] = gate[:, :N_EXPERTS]


def _merge(x, conv, attn, mem, wts, tm, alpha):
    n, d = x.shape
    tok = lambda w: pl.BlockSpec((tm, w), lambda i: (i, 0))
    const = lambda a: pl.BlockSpec(a.shape, lambda i: (0, 0))
    return pl.pallas_call(
        functools.partial(_merge_kernel, alpha=alpha),
        grid=(n // tm,),
        in_specs=[tok(d), tok(CONV_W), tok(ATTN_W), tok(MEM_W)] + [const(a) for a in wts],
        out_specs=[tok(d), tok(N_EXPERTS)],
        out_shape=[jax.ShapeDtypeStruct((n, d), F32), jax.ShapeDtypeStruct((n, N_EXPERTS), F32)],
        compiler_params=_params("parallel"),
        name="merge_ln_router",
    )(x, conv, attn, mem, *wts)


def _moe_kernel(h_ref, gate_ref, wu_ref, wd_ref, g2_ref, b2_ref, o_ref, hb_ref, y_ref, *, alpha):
    e = pl.program_id(1)
    de = wd_ref.shape[1]

    @pl.when(e == 0)
    def _():
        hb_ref[...] = h_ref[...].astype(BF16)
        y_ref[...] = jnp.zeros_like(y_ref)

    hu = _dot(hb_ref[...], wu_ref[0])
    a = hu[:, :de]
    act = (a * (1.0 / (1.0 + jnp.exp(-a))) * hu[:, de:]).astype(BF16)
    gate = gate_ref[...]
    lane = lax.broadcasted_iota(I32, gate.shape, 1)
    ge = jnp.sum(jnp.where(lane == e, gate, 0.0), axis=1, keepdims=True)
    y_ref[...] += ge * _dot(act, wd_ref[0])

    @pl.when(e == pl.num_programs(1) - 1)
    def _():
        o_ref[...] = _layernorm(alpha * h_ref[...] + y_ref[...], g2_ref[...], b2_ref[...])


def _moe(h, gate, wu, wd, g2, b2, tm, alpha):
    n, d = h.shape
    ne, _, du = wu.shape
    de = wd.shape[1]
    return pl.pallas_call(
        functools.partial(_moe_kernel, alpha=alpha),
        grid=(n // tm, ne),
        in_specs=[pl.BlockSpec((tm, d), lambda i, e: (i, 0)),
                  pl.BlockSpec((tm, ne), lambda i, e: (i, 0)),
                  pl.BlockSpec((1, d, du), lambda i, e: (e, 0, 0)),
                  pl.BlockSpec((1, de, d), lambda i, e: (e, 0, 0)),
                  pl.BlockSpec((1, d), lambda i, e: (0, 0)),
                  pl.BlockSpec((1, d), lambda i, e: (0, 0))],
        out_specs=pl.BlockSpec((tm, d), lambda i, e: (i, 0)),
        out_shape=jax.ShapeDtypeStruct((n, d), F32),
        scratch_shapes=[pltpu.VMEM((tm, d), BF16), pltpu.VMEM((tm, d), F32)],
        compiler_params=_params("parallel", "arbitrary"),
        name="moe_ln",
    )(h, gate, wu, wd, g2, b2)


def _sample_inproj_kernel(x_ref, w_ref, wt_ref, cos_ref, sin_ref, cw_ref, p0_ref, p1_ref,
                          conv_ref, z_ref, q_ref, k_ref, v_ref, qi_ref, ki_ref, wi_ref, qm_ref):
    xb = x_ref[...].astype(BF16)
    cos = cos_ref[...]
    sin = sin_ref[...]

    def proj(a, b):
        return _dot(xb, w_ref[:, a:b])

    def proj_rows(a, b):
        return _dot_nt(xb, wt_ref[a:b, :])

    ucc = proj(C_U, C_K)
    z = ucc[:, C_C:C_K] * ucc[:, C_U:C_B]
    cw = cw_ref[...]
    y = p0_ref[...] * cw[0:1, :] + p1_ref[...] * cw[1:2, :] + z * cw[2:3, :]
    conv_ref[...] = (ucc[:, C_B:C_C] * y).astype(BF16)
    z_ref[...] = z
    q_ref[...] = _rope(proj_rows(R_Q, R_QI), cos, sin) * (HEAD_DIM ** -0.5)
    k_ref[...] = _rope(proj(C_K, C_KI), cos, sin)
    v_ref[...] = proj_rows(R_V, R_KI)
    qi_ref[...] = _rope(proj_rows(R_QI, R_K), cos, sin) * (IDX_DIM ** -0.5)
    ki_ref[...] = _rope(proj(C_KI, C_QM), cos, sin)[:, :IDX_DIM]
    wi_ref[...] = proj_rows(R_WI, R_END)[:, :N_IDX_HEADS] * (N_IDX_HEADS ** -0.5)
    qm_ref[...] = proj(C_QM, C_END) * (HEAD_DIM ** -0.5)


def _sample_inproj(x, wcat, wt, cos, sin, conv_w, p0, p1):
    n, d = x.shape
    args = (x, wcat, wt, cos, sin, conv_w, p0, p1)
    widths = (CONV_W, CONV_W, ATTN_W, ATTN_W, ATTN_W, N_IDX_HEADS * IDX_DIM, IDX_DIM, N_IDX_HEADS, MEM_W)
    dtypes = (BF16,) + (F32,) * 8
    full = lambda shape: pl.BlockSpec(shape, lambda i: (0,) * len(shape))
    return pl.pallas_call(
        _sample_inproj_kernel,
        grid=(1,),
        in_specs=[full(a.shape) for a in args],
        out_specs=[full((n, w)) for w in widths],
        out_shape=[jax.ShapeDtypeStruct((n, w), dt) for w, dt in zip(widths, dtypes)],
        compiler_params=_params("arbitrary"),
        name="sample_inproj",
    )(*args)


def _head_mask(rows, width):
    r = lax.broadcasted_iota(I32, (rows, width), 0)
    c = lax.broadcasted_iota(I32, (rows, width), 1)
    return (c >> 6) == r


def _sample_mem_kernel(qm_ref, mk_ref, mv_ref, o_ref):
    hmask = _head_mask(8, MEM_W)
    qbd = jnp.where(hmask, jnp.broadcast_to(qm_ref[0], (8, MEM_W)), 0.0).astype(BF16)
    p = _softmax_rows(_dot_nt(qbd, mk_ref[0].astype(BF16)))
    o = _dot(p.astype(BF16), mv_ref[0].astype(BF16))
    o_ref[0] = jnp.sum(jnp.where(hmask, o, 0.0), axis=0, keepdims=True).astype(BF16)


def _sample_mem(qm, mk, mv):
    n, n_mem, w = mk.shape
    return pl.pallas_call(
        _sample_mem_kernel,
        grid=(n,),
        in_specs=[pl.BlockSpec((1, 1, w), lambda i: (i, 0, 0)),
                  pl.BlockSpec((1, n_mem, w), lambda i: (i, 0, 0)),
                  pl.BlockSpec((1, n_mem, w), lambda i: (i, 0, 0))],
        out_specs=pl.BlockSpec((1, 1, w), lambda i: (i, 0, 0)),
        out_shape=jax.ShapeDtypeStruct((n, 1, w), BF16),
        compiler_params=_params("parallel"),
        name="sample_mem_attn",
    )(qm, mk, mv)


def _sample_scores_kernel(pt_ref, *refs, pages):
    page_refs = refs[:pages]
    qi_ref, wi_ref, o_ref = refs[pages:]
    keys_t = jnp.concatenate([r[0] for r in page_refs], axis=1).astype(BF16)
    s = _dot(qi_ref[0].astype(BF16), keys_t)
    o_ref[0, 0] = jnp.sum(wi_ref[0] * jnp.maximum(s, 0.0), axis=0, keepdims=True)


def _sample_scores(page_table, cki_t, qi, wi, pages):
    n, n_pages = page_table.shape
    ng = n_pages // pages
    width = pages * PAGE_SIZE

    def page_spec(j):
        return pl.BlockSpec((1, IDX_DIM, PAGE_SIZE), lambda s, pt: (pt[s // ng, (s % ng) * pages + j], 0, 0))

    grid_spec = pltpu.PrefetchScalarGridSpec(
        num_scalar_prefetch=1,
        grid=(n * ng,),
        in_specs=[page_spec(j) for j in range(pages)] + [
            pl.BlockSpec((1, N_IDX_HEADS, IDX_DIM), lambda s, pt: (s // ng, 0, 0)),
            pl.BlockSpec((1, N_IDX_HEADS, 1), lambda s, pt: (s // ng, 0, 0))],
        out_specs=pl.BlockSpec((1, 1, 1, width), lambda s, pt: (s // ng, s % ng, 0, 0)),
    )
    out = pl.pallas_call(
        functools.partial(_sample_scores_kernel, pages=pages),
        grid_spec=grid_spec,
        out_shape=jax.ShapeDtypeStruct((n, ng, 1, width), F32),
        compiler_params=_params("arbitrary"),
        name="sample_idx_scores",
    )(page_table, *([cki_t] * pages), qi, wi)
    return out.reshape(n, n_pages * PAGE_SIZE)


def _sample_select_kernel(sc_ref, qi_ref, ki_ref, wi_ref, bias_ref, bnew_ref, j_ref, *, topk):
    n, past = sc_ref.shape
    qi = qi_ref[...].astype(BF16).astype(F32)
    ki = ki_ref[...].astype(BF16).astype(F32)
    s_new = jnp.maximum(jnp.sum(qi * ki, axis=2), 0.0)
    sc_new = jnp.sum(wi_ref[...] * s_new, axis=1, keepdims=True)
    idx = lax.broadcasted_iota(I32, (n, past), 1)

    def count(pred):
        one = jnp.where(pred(sc_ref[...], idx), 1.0, 0.0)
        return jnp.sum(one, axis=1, keepdims=True) + jnp.where(pred(sc_new, past), 1.0, 0.0)

    def bit_body(it, carry):
        thr, cnt_thr = carry
        cand = thr ^ lax.shift_left(jnp.int32(1), 31 - it)
        cand_f = _key_to_float(cand)
        cnt = count(lambda sc, ii: sc >= cand_f)
        ok = cnt >= topk
        return jnp.where(ok, cand, thr), jnp.where(ok, cnt, cnt_thr)

    thr_key, cnt_thr = lax.fori_loop(
        0, 32, bit_body, (jnp.full((n, 1), INT_MIN, I32), jnp.full((n, 1), float(topk), F32)))
    thr = _key_to_float(thr_key)

    j_ref[...] = jnp.full((n, 1), BIG_IDX, I32)

    @pl.when(jnp.max(jnp.where(cnt_thr > topk, 1.0, 0.0)) > 0.0)
    def _():
        need = topk - count(lambda sc, ii: sc > thr)

        def idx_body(it, jj):
            cand = jj | lax.shift_left(jnp.int32(1), 29 - it)
            below = count(lambda sc, ii: (sc == thr) & (ii < cand))
            return jnp.where(below < need, cand, jj)

        j_ref[...] = lax.fori_loop(0, 30, idx_body, jnp.zeros((n, 1), I32))

    jlast = j_ref[...]
    sc = sc_ref[...]
    bias_ref[...] = jnp.where((sc > thr) | ((sc == thr) & (idx <= jlast)), 0.0, NEG)
    bnew_ref[...] = jnp.where((sc_new > thr) | ((sc_new == thr) & (past <= jlast)), 0.0, NEG)


def _sample_select(scores, qi, ki, wi, topk):
    n, past = scores.shape
    args = (scores, qi, ki, wi)
    full = lambda shape: pl.BlockSpec(shape, lambda i: (0,) * len(shape))
    return pl.pallas_call(
        functools.partial(_sample_select_kernel, topk=topk),
        grid=(1,),
        in_specs=[full(a.shape) for a in args],
        out_specs=[full((n, past)), full((n, 1))],
        out_shape=[jax.ShapeDtypeStruct((n, past), F32), jax.ShapeDtypeStruct((n, 1), F32)],
        scratch_shapes=[pltpu.VMEM((n, 1), I32)],
        compiler_params=_params("arbitrary"),
        name="sample_topk_select",
    )(*args)


def _sample_attn_kernel(pt_ref, *refs, pages, nch):
    k_refs = refs[:pages]
    v_refs = refs[pages:2 * pages]
    q_ref, bias_ref, bnew_ref, kn_ref, vn_ref, o_ref, m_ref, l_ref, acc_ref = refs[2 * pages:]
    c = pl.program_id(0) % nch

    @pl.when(c == 0)
    def _():
        m_ref[...] = jnp.full_like(m_ref, NEG)
        l_ref[...] = jnp.zeros_like(l_ref)
        acc_ref[...] = jnp.zeros_like(acc_ref)

    rows = []
    for h in range(N_HEADS):
        qh = q_ref[0, h]
        rows.append(jnp.concatenate([jnp.sum(r[0, h] * qh, axis=0, keepdims=True) for r in k_refs], axis=1))
    s = jnp.concatenate(rows, axis=0) + bias_ref[0, 0]
    m_old = m_ref[...]
    m_new = jnp.maximum(m_old, jnp.max(s, axis=1, keepdims=True))
    alpha = jnp.exp(m_old - m_new)
    p = jnp.exp(s - m_new)
    l_ref[...] = alpha * l_ref[...] + jnp.sum(p, axis=1, keepdims=True)
    m_ref[...] = m_new
    for h in range(N_HEADS):
        acc = alpha[h:h + 1, :] * acc_ref[h]
        for j, r in enumerate(v_refs):
            acc = acc + r[0, h] * p[h:h + 1, j * PAGE_SIZE:(j + 1) * PAGE_SIZE]
        acc_ref[h] = acc

    @pl.when(c == nch - 1)
    def _():
        bn = bnew_ref[0]
        for h in range(N_HEADS):
            s_n = jnp.sum(q_ref[0, h] * kn_ref[0, h], axis=0, keepdims=True) + bn
            m0 = m_ref[h:h + 1, :]
            m1 = jnp.maximum(m0, s_n)
            a1 = jnp.exp(m0 - m1)
            p_n = jnp.exp(s_n - m1)
            l1 = a1 * l_ref[h:h + 1, :] + p_n
            out = a1 * jnp.sum(acc_ref[h], axis=1, keepdims=True) + p_n * vn_ref[0, h]
            o_ref[0, h] = out / l1


def _sample_attn(page_table, ck_t, cv_t, q, bias, bnew, kn, vn, pages):
    n, n_pages = page_table.shape
    nch = n_pages // pages
    width = pages * PAGE_SIZE

    def page_spec(j):
        return pl.BlockSpec((1, N_HEADS, HEAD_DIM, PAGE_SIZE),
                            lambda s, pt: (pt[s // nch, (s % nch) * pages + j], 0, 0, 0))

    per_seq = pl.BlockSpec((1, N_HEADS, HEAD_DIM, 1), lambda s, pt: (s // nch, 0, 0, 0))
    grid_spec = pltpu.PrefetchScalarGridSpec(
        num_scalar_prefetch=1,
        grid=(n * nch,),
        in_specs=[page_spec(j) for j in range(pages)] * 2 + [
            per_seq,
            pl.BlockSpec((1, 1, 1, width), lambda s, pt: (s // nch, s % nch, 0, 0)),
            pl.BlockSpec((1, 1, 1), lambda s, pt: (s // nch, 0, 0)),
            per_seq, per_seq],
        out_specs=per_seq,
        scratch_shapes=[pltpu.VMEM((N_HEADS, 1), F32), pltpu.VMEM((N_HEADS, 1), F32),
                        pltpu.VMEM((N_HEADS, HEAD_DIM, PAGE_SIZE), F32)],
    )
    return pl.pallas_call(
        functools.partial(_sample_attn_kernel, pages=pages, nch=nch),
        grid_spec=grid_spec,
        out_shape=jax.ShapeDtypeStruct((n, N_HEADS, HEAD_DIM, 1), F32),
        compiler_params=_params("arbitrary"),
        name="sample_attn",
    )(page_table, *([ck_t] * pages), *([cv_t] * pages), q, bias.reshape(n, nch, 1, width), bnew, kn, vn)


def _rope_tables(pos):
    inv = ROPE_THETA ** (-jnp.arange(HALF, dtype=F32) / HALF)
    ang = pos.astype(F32)[:, None] * inv[None, :]
    cos = jnp.cos(ang)
    sin = jnp.sin(ang)
    return jnp.concatenate([cos] * 4, axis=1), jnp.concatenate([-sin, sin] * 2, axis=1), cos.T, sin.T


def _largest_divisor(n, cap):
    for c in range(min(cap, n), 0, -1):
        if n % c == 0:
            return c
    return 1


def _pack_layer_weights(l, w_in, b_gate, w_br_conv, w_br_attn, w_br_mem, w_o, ln1_g, ln1_b, w_group, b_group,
                        w_expert, b_expert):
    d = w_in.shape[1]
    sizes = (CONV_W, CONV_W, CONV_W, ATTN_W, ATTN_W, ATTN_W, N_IDX_HEADS * IDX_DIM, IDX_DIM, N_IDX_HEADS, MEM_W,
             N_BRANCH * d)
    offs = [0]
    for s in sizes:
        offs.append(offs[-1] + s)
    col = lambda i: w_in[l][:, offs[i]:offs[i + 1]]
    u, c_b, c_c, q, k, v, qi, ki, wi, qm, gates = (col(i) for i in range(len(sizes)))
    zeros = lambda n: jnp.zeros((d, n), F32)
    wcat = jnp.concatenate([u, c_b, c_c, k, ki, zeros(LANES - IDX_DIM), qm], axis=1).astype(BF16)
    wt = jnp.concatenate([q, qi, k, v, ki, wi, zeros(16 - N_IDX_HEADS)], axis=1).T.astype(BF16)
    wr = jnp.concatenate([w_expert[l], w_group[l], zeros(LANES - N_EXPERTS - N_GROUPS)], axis=1)
    wrh = wr.astype(BF16)
    wrl = (wr - wrh.astype(F32)).astype(BF16)
    br = jnp.concatenate([b_expert[l], b_group[l], jnp.zeros((LANES - N_EXPERTS - N_GROUPS,), F32)])[None, :]
    merge_w = (gates.astype(BF16), b_gate[l][None, :], w_br_conv[l].astype(BF16), w_br_attn[l].astype(BF16),
               w_br_mem[l].astype(BF16), w_o[l].astype(BF16), ln1_g[l][None, :], ln1_b[l][None, :], wrh, wrl, br)
    return wcat, wt, merge_w


def kernel(x_prompt, x_sample, mem_prompt, cache_k, cache_v, cache_k_idx, state_conv, cache_mem_k, cache_mem_v,
           page_table, w_in, b_gate, conv_w, w_br_conv, w_br_attn, w_br_mem, w_o, w_mem_k, w_mem_v, ln1_g, ln1_b,
           w_group, b_group, w_expert, b_expert, w_up, w_down, ln2_g, ln2_b):
    depth = w_in.shape[0]
    b, s_len, d = x_prompt.shape
    bd, t_new, _ = x_sample.shape
    assert t_new == 1, "the sample group decodes one token per sequence"
    n_pages = page_table.shape[1]
    past = n_pages * PAGE_SIZE
    alpha = (2.0 * depth) ** 0.25

    tq = _largest_divisor(s_len, 256)
    tm_in = _largest_divisor(s_len, 512)
    assert tm_in % tq == 0
    n_tok = b * s_len
    tm_merge = _largest_divisor(n_tok, 512)
    tm_moe = _largest_divisor(n_tok, 1024)
    topk_p = min(TOPK_MAX, s_len // 4)
    topk_s = min(TOPK_MAX, (past + t_new) // 4)
    pages_idx = _largest_divisor(n_pages, 16)
    pages_att = _largest_divisor(n_pages, 16)

    cos_p, sin_p, cost_p, sint_p = _rope_tables(jnp.arange(s_len, dtype=jnp.int32))
    cos_s, sin_s, _, _ = _rope_tables(past + jnp.arange(t_new, dtype=jnp.int32))

    hp = x_prompt
    hs = x_sample.reshape(bd, d)
    outs = [[] for _ in range(10)]
    for l in range(depth):
        wcat, wt, merge_w = _pack_layer_weights(l, w_in, b_gate, w_br_conv, w_br_attn, w_br_mem, w_o, ln1_g, ln1_b,
                                                w_group, b_group, w_expert, b_expert)
        wu = w_up[l].astype(BF16)
        wd = w_down[l].astype(BF16)
        g2, b2 = ln2_g[l][None, :], ln2_b[l][None, :]

        mk, mv = _memkv(mem_prompt, w_mem_k[l].astype(BF16), w_mem_v[l].astype(BF16))
        (conv_o, mem_o, qt_b, qit_b, wit_p, kt_p, vt_p, k_b, vt_b, kit_p, ki_b, conv_new) = _inproj(
            hp, wcat, wt, cos_p, sin_p, cost_p, sint_p, conv_w[l], mk, mv, tm_in, tq)
        attn_o = _dsa(qit_b, wit_p, ki_b, qt_b, k_b, vt_b, tq, topk_p)
        h1, gate = _merge(hp.reshape(n_tok, d), conv_o.reshape(n_tok, CONV_W), attn_o.reshape(n_tok, ATTN_W),
                          mem_o.reshape(n_tok, MEM_W), merge_w, tm_merge, alpha)
        hp = _moe(h1, gate, wu, wd, g2, b2, tm_moe, alpha).reshape(b, s_len, d)
        for lst, val in zip(outs[:6], (jnp.transpose(kt_p, (0, 3, 1, 2)), jnp.transpose(vt_p, (0, 3, 1, 2)),
                                       jnp.transpose(kit_p, (0, 2, 1)), conv_new,
                                       mk.reshape(b, -1, N_MEM_HEADS, HEAD_DIM),
                                       mv.reshape(b, -1, N_MEM_HEADS, HEAD_DIM))):
            lst.append(val)

        st = state_conv[l].astype(F32)
        (conv_s, z_s, q_s, k_s, v_s, qi_s, ki_s, wi_s, qm_s) = _sample_inproj(
            hs, wcat, wt, cos_s, sin_s, conv_w[l], st[:, 0, :], st[:, 1, :])
        mem_s = _sample_mem(qm_s.reshape(bd, 1, MEM_W), cache_mem_k[l].reshape(bd, -1, MEM_W),
                            cache_mem_v[l].reshape(bd, -1, MEM_W))
        qi3 = qi_s.reshape(bd, N_IDX_HEADS, IDX_DIM)
        wi3 = wi_s.reshape(bd, N_IDX_HEADS, 1)
        scores = _sample_scores(page_table, jnp.transpose(cache_k_idx[l], (0, 2, 1)), qi3, wi3, pages_idx)
        bias, bnew = _sample_select(scores, qi3, ki_s.reshape(bd, 1, IDX_DIM), wi_s, topk_s)
        col4 = lambda a: a.reshape(bd, N_HEADS, HEAD_DIM, 1)
        attn_s = _sample_attn(page_table, jnp.transpose(cache_k[l], (0, 2, 3, 1)),
                              jnp.transpose(cache_v[l], (0, 2, 3, 1)), col4(q_s), bias, bnew.reshape(bd, 1, 1),
                              col4(k_s), col4(v_s), pages_att)
        h1s, gate_s = _merge(hs, conv_s, attn_s.reshape(bd, ATTN_W).astype(BF16), mem_s.reshape(bd, MEM_W), merge_w,
                             bd, alpha)
        hs = _moe(h1s, gate_s, wu, wd, g2, b2, bd, alpha)
        for lst, val in zip(outs[6:], (k_s.reshape(bd, t_new, N_HEADS, HEAD_DIM),
                                       v_s.reshape(bd, t_new, N_HEADS, HEAD_DIM), ki_s.reshape(bd, t_new, IDX_DIM),
                                       jnp.stack([st[:, 1, :], z_s], axis=1))):
            lst.append(val)

    return (hp, hs.reshape(bd, t_new, d)) + tuple(jnp.stack(o) for o in outs)
```

```python
import functools

import jax
import jax.numpy as jnp
from jax import lax
from jax.experimental import pallas as pl
from jax.experimental.pallas import tpu as pltpu

F32 = jnp.float32
BF16 = jnp.bfloat16
I32 = jnp.int32

HEAD_DIM = 64
HALF = HEAD_DIM // 2
N_HEADS = 8
ATTN_W = N_HEADS * HEAD_DIM
N_IDX_HEADS = 8
IDX_DIM = 64
TOPK_MAX = 256
N_MEM_HEADS = 4
MEM_W = N_MEM_HEADS * HEAD_DIM
CONV_W = 256
CONV_K = 3
N_BRANCH = 3
N_GROUPS = 4
EXPERTS_PER_GROUP = 8
N_EXPERTS = N_GROUPS * EXPERTS_PER_GROUP
PAGE_SIZE = 128
ROPE_THETA = 10000.0
LN_EPS = 1e-5
NEG = -1e30
LANES = 128
SUBLANES = 8

C_U, C_B, C_C = 0, CONV_W, 2 * CONV_W
C_K = 3 * CONV_W
C_KI = C_K + ATTN_W
C_QM = C_KI + LANES
C_END = C_QM + MEM_W
R_Q = 0
R_QI = R_Q + ATTN_W
R_K = R_QI + N_IDX_HEADS * IDX_DIM
R_V = R_K + ATTN_W
R_KI = R_V + ATTN_W
R_WI = R_KI + IDX_DIM
R_END = R_WI + 16

INT_MIN = -(2 ** 31)
KEY_OF_NEG_INF = -2139095041
KEY_OF_POS_INF = 2139095040
BIG_IDX = 2 ** 30
FIELD = 10
Q_SCALE_LOG2 = HEAD_DIM ** -0.5 * 1.4426950408889634
VMEM_LIMIT = 56 * 1024 * 1024

_NT = (((1,), (1,)), ((), ()))


def _dot(a, b):
    return jnp.dot(a, b, preferred_element_type=F32)


def _dot_nt(a, b):
    return lax.dot_general(a, b, _NT, preferred_element_type=F32)


def _params(*sem):
    return pltpu.CompilerParams(dimension_semantics=sem, vmem_limit_bytes=VMEM_LIMIT)


def _rope(x, cos, sin):
    w = x.shape[1]
    lane = lax.broadcasted_iota(I32, x.shape, 1)
    swapped = jnp.where((lane & 63) < 32, pltpu.roll(x, w - 32, 1), pltpu.roll(x, 32, 1))
    reps = w // LANES
    if reps > 1:
        cos = jnp.concatenate([cos] * reps, axis=1)
        sin = jnp.concatenate([sin] * reps, axis=1)
    return x * cos + swapped * sin


def _rope_t(xt, cos_t, sin_t):
    x1, x2 = xt[:HALF, :], xt[HALF:, :]
    return x1 * cos_t - x2 * sin_t, x1 * sin_t + x2 * cos_t


def _order_key(x):
    b = lax.bitcast_convert_type(x, I32)
    return b ^ ((b >> 31) & 0x7FFFFFFF)


def _key_to_float(key):
    key = jnp.clip(key, KEY_OF_NEG_INF, KEY_OF_POS_INF)
    return lax.bitcast_convert_type(key ^ ((key >> 31) & 0x7FFFFFFF), F32)


def _layernorm(x, g, b):
    mu = jnp.mean(x, axis=-1, keepdims=True)
    xc = x - mu
    var = jnp.mean(xc * xc, axis=-1, keepdims=True)
    return xc * lax.rsqrt(var + LN_EPS) * g + b


def _softmax_rows(s):
    m = jnp.max(s, axis=-1, keepdims=True)
    e = jnp.exp(s - m)
    return e / jnp.sum(e, axis=-1, keepdims=True)


def _memkv_kernel(mem_ref, wk_ref, wv_ref, mk_ref, mv_ref):
    m = mem_ref[0].astype(BF16)
    mk_ref[0] = _dot(m, wk_ref[...])
    mv_ref[0] = _dot(m, wv_ref[...])


def _memkv(mem, wk, wv):
    b, n_mem, d = mem.shape
    return pl.pallas_call(
        _memkv_kernel,
        grid=(b,),
        in_specs=[pl.BlockSpec((1, n_mem, d), lambda i: (i, 0, 0)),
                  pl.BlockSpec((d, MEM_W), lambda i: (0, 0)),
                  pl.BlockSpec((d, MEM_W), lambda i: (0, 0))],
        out_specs=[pl.BlockSpec((1, n_mem, MEM_W), lambda i: (i, 0, 0))] * 2,
        out_shape=[jax.ShapeDtypeStruct((b, n_mem, MEM_W), F32)] * 2,
        compiler_params=_params("parallel"),
        name="memkv",
    )(mem, wk, wv)


def _inproj_kernel(x_ref, w_ref, wt_ref, cos_ref, sin_ref, cost_ref, sint_ref, cw_ref, mk_ref, mv_ref,
                   conv_ref, mem_ref, qt_ref, qit_ref, wit_ref, kt_ref, vt_ref, kb_ref, vtb_ref, kit_ref, kib_ref,
                   cnew_ref, zprev_ref):
    j = pl.program_id(1)
    tm = x_ref.shape[1]
    kc = vtb_ref.shape[4]
    xb = x_ref[0].astype(BF16)
    cos = cos_ref[...]
    sin = sin_ref[...]
    cos_t = cost_ref[...]
    sin_t = sint_ref[...]

    def proj(a, b):
        return _dot(xb, w_ref[:, a:b])

    def proj_t(a, b):
        return _dot_nt(wt_ref[a:b, :], xb)

    @pl.when(j == 0)
    def _():
        zprev_ref[...] = jnp.zeros_like(zprev_ref)

    ucc = proj(C_U, C_K)
    z = ucc[:, C_C:C_K] * ucc[:, C_U:C_B]
    prev = zprev_ref[...]
    row = lax.broadcasted_iota(I32, z.shape, 0)
    z1 = jnp.where(row == 0, prev[7:8, :], pltpu.roll(z, 1, 0))
    z2 = jnp.where(row == 0, prev[6:7, :], jnp.where(row == 1, prev[7:8, :], pltpu.roll(z, 2, 0)))
    cw = cw_ref[...]
    y = z2 * cw[0:1, :] + z1 * cw[1:2, :] + z * cw[2:3, :]
    conv_ref[0] = (ucc[:, C_B:C_C] * y).astype(BF16)
    zprev_ref[...] = z[tm - 8:tm, :]
    cnew_ref[0] = z[tm - 2:tm, :]

    k = _rope(proj(C_K, C_KI), cos, sin)
    for h in range(N_HEADS):
        kb_ref[0, h] = k[:, h * HEAD_DIM:(h + 1) * HEAD_DIM].astype(BF16)
    kib_ref[0] = _rope(proj(C_KI, C_QM), cos, sin)[:, :IDX_DIM].astype(BF16)

    qt = proj_t(R_Q, R_QI)
    qit = proj_t(R_QI, R_K)
    kt = proj_t(R_K, R_V)
    vt = proj_t(R_V, R_KI)
    for h in range(N_HEADS):
        hs = slice(h * HEAD_DIM, (h + 1) * HEAD_DIM)
        o1, o2 = _rope_t(qt[hs, :], cos_t, sin_t)
        qt_ref[0, h, :HALF, :] = (o1 * Q_SCALE_LOG2).astype(BF16)
        qt_ref[0, h, HALF:, :] = (o2 * Q_SCALE_LOG2).astype(BF16)
        o1, o2 = _rope_t(qit[hs, :], cos_t, sin_t)
        qit_ref[0, h, :HALF, :] = (o1 * (IDX_DIM ** -0.5)).astype(BF16)
        qit_ref[0, h, HALF:, :] = (o2 * (IDX_DIM ** -0.5)).astype(BF16)
        o1, o2 = _rope_t(kt[hs, :], cos_t, sin_t)
        kt_ref[0, h, :HALF, :] = o1
        kt_ref[0, h, HALF:, :] = o2
        vt_ref[0, h] = vt[hs, :]
        for cc in range(tm // kc):
            vtb_ref[0, h, cc] = vt[hs, cc * kc:(cc + 1) * kc].astype(BF16)
    o1, o2 = _rope_t(proj_t(R_KI, R_WI), cos_t, sin_t)
    kit_ref[0, :HALF, :] = o1
    kit_ref[0, HALF:, :] = o2
    wit_ref[0] = proj_t(R_WI, R_END)[:N_IDX_HEADS, :] * (N_IDX_HEADS ** -0.5)

    qm = (proj(C_QM, C_END) * (HEAD_DIM ** -0.5)).astype(BF16)
    mk = mk_ref[0].astype(BF16)
    mv = mv_ref[0].astype(BF16)
    outs = []
    for h in range(N_MEM_HEADS):
        sl = slice(h * HEAD_DIM, (h + 1) * HEAD_DIM)
        p = _softmax_rows(_dot_nt(qm[:, sl], mk[:, sl]))
        outs.append(_dot(p.astype(BF16), mv[:, sl]))
    mem_ref[0] = jnp.concatenate(outs, axis=1).astype(BF16)


def _inproj(x, wcat, wt, cos, sin, cos_t, sin_t, conv_w, mk, mv, tm, kc):
    b, t, d = x.shape
    n_mem = mk.shape[1]
    grid = (b, t // tm)
    tok = lambda w: pl.BlockSpec((1, tm, w), lambda i, j: (i, j, 0))
    hm_t = pl.BlockSpec((1, N_HEADS, HEAD_DIM, tm), lambda i, j: (i, 0, 0, j))
    const2 = lambda r, c: pl.BlockSpec((r, c), lambda i, j: (0, 0))
    hm_t_shape = lambda dt: jax.ShapeDtypeStruct((b, N_HEADS, HEAD_DIM, t), dt)
    out_shape = [
        jax.ShapeDtypeStruct((b, t, CONV_W), BF16),
        jax.ShapeDtypeStruct((b, t, MEM_W), BF16),
        hm_t_shape(BF16), hm_t_shape(BF16),
        jax.ShapeDtypeStruct((b, N_IDX_HEADS, t), F32),
        hm_t_shape(F32), hm_t_shape(F32),
        jax.ShapeDtypeStruct((b, N_HEADS, t, HEAD_DIM), BF16),
        jax.ShapeDtypeStruct((b, N_HEADS, t // kc, HEAD_DIM, kc), BF16),
        jax.ShapeDtypeStruct((b, IDX_DIM, t), F32),
        jax.ShapeDtypeStruct((b, t, IDX_DIM), BF16),
        jax.ShapeDtypeStruct((b, CONV_K - 1, CONV_W), F32),
    ]
    out_specs = [tok(CONV_W), tok(MEM_W), hm_t, hm_t,
                 pl.BlockSpec((1, N_IDX_HEADS, tm), lambda i, j: (i, 0, j)),
                 hm_t, hm_t,
                 pl.BlockSpec((1, N_HEADS, tm, HEAD_DIM), lambda i, j: (i, 0, j, 0)),
                 pl.BlockSpec((1, N_HEADS, tm // kc, HEAD_DIM, kc), lambda i, j: (i, 0, j, 0, 0)),
                 pl.BlockSpec((1, IDX_DIM, tm), lambda i, j: (i, 0, j)),
                 tok(IDX_DIM),
                 pl.BlockSpec((1, CONV_K - 1, CONV_W), lambda i, j: (i, 0, 0))]
    return pl.pallas_call(
        _inproj_kernel,
        grid=grid,
        in_specs=[tok(d), const2(d, C_END), const2(R_END, d),
                  pl.BlockSpec((tm, LANES), lambda i, j: (j, 0)), pl.BlockSpec((tm, LANES), lambda i, j: (j, 0)),
                  pl.BlockSpec((HALF, tm), lambda i, j: (0, j)), pl.BlockSpec((HALF, tm), lambda i, j: (0, j)),
                  const2(CONV_K, CONV_W),
                  pl.BlockSpec((1, n_mem, MEM_W), lambda i, j: (i, 0, 0)),
                  pl.BlockSpec((1, n_mem, MEM_W), lambda i, j: (i, 0, 0))],
        out_specs=out_specs,
        out_shape=out_shape,
        scratch_shapes=[pltpu.VMEM((8, CONV_W), F32)],
        compiler_params=_params("parallel", "arbitrary"),
        name="inproj",
    )(x, wcat, wt, cos, sin, cos_t, sin_t, conv_w, mk, mv)


def _dsa_kernel(qit_ref, wit_ref, ki_ref, qt_ref, k_ref, vt_ref, o_ref,
                sc_ref, j_ref, m_ref, l_ref, acc_ref, *, topk):
    tq = qt_ref.shape[3]
    kc = sc_ref.shape[1]
    i = pl.program_id(1)
    nk = i + 1
    t0 = i * tq
    wt = wit_ref[0]
    key_pos = lax.broadcasted_iota(I32, (kc, tq), 0)
    q_pos = lax.broadcasted_iota(I32, (kc, tq), 1) + t0

    def keys_of(c):
        return pl.ds(pl.multiple_of(c * kc, kc), kc)

    def score_body(c, carry):
        kic = ki_ref[0, keys_of(c), :]
        acc = jnp.zeros((kc, tq), F32)
        for h in range(N_IDX_HEADS):
            acc = acc + wt[h:h + 1, :] * jnp.maximum(_dot(kic, qit_ref[0, h]), 0.0)
        sc_ref[c] = jnp.where(key_pos + c * kc <= q_pos, acc, -jnp.inf)
        return carry

    lax.fori_loop(0, nk, score_body, 0)

    def fold(v):
        return jnp.sum(v.reshape(kc // SUBLANES, SUBLANES, tq), axis=0)

    def count(pred):
        def body(c, cnt):
            return cnt + fold(jnp.where(pred(sc_ref[c], key_pos + c * kc), 1.0, 0.0))
        cnt = lax.fori_loop(0, nk, body, jnp.zeros((SUBLANES, tq), F32))
        return jnp.sum(cnt, axis=0, keepdims=True)

    def count3(c1, c2, c3):
        def body(c, acc):
            sc = sc_ref[c]
            v = jnp.where(sc >= c3, 1 + (1 << FIELD) + (1 << 2 * FIELD),
                          jnp.where(sc >= c2, 1 + (1 << FIELD), jnp.where(sc >= c1, 1, 0)))
            return acc + fold(v)
        acc = lax.fori_loop(0, nk, body, jnp.zeros((SUBLANES, tq), I32))
        mask = (1 << FIELD) - 1
        return [jnp.sum(f.astype(F32), axis=0, keepdims=True)
                for f in (acc & mask, (acc >> FIELD) & mask, acc >> 2 * FIELD)]

    def bit_body(it, carry):
        thr, cnt_thr = carry
        hi = lax.shift_left(jnp.int32(1), 31 - 2 * it)
        lo = lax.shift_left(jnp.int32(1), 30 - 2 * it)
        cands = (thr ^ lo, thr ^ hi, thr ^ (hi | lo))
        cnts = count3(*[_key_to_float(cj) for cj in cands])
        for cj, nj in zip(cands, cnts):
            ok = nj >= topk
            thr = jnp.where(ok, cj, thr)
            cnt_thr = jnp.where(ok, nj, cnt_thr)
        return thr, cnt_thr

    thr_key, cnt_thr = lax.fori_loop(
        0, 16, bit_body, (jnp.full((1, tq), INT_MIN, I32), jnp.full((1, tq), float(topk), F32)))
    thr = _key_to_float(thr_key)

    j_ref[...] = jnp.full_like(j_ref, BIG_IDX)
    tie_q = jnp.where((thr_key > KEY_OF_NEG_INF) & (cnt_thr > topk), 1.0, 0.0)

    @pl.when(jnp.max(tie_q) > 0.0)
    def _():
        need = topk - count(lambda sc, pos: sc > thr)

        def idx_body(it, jj):
            cand = jj | lax.shift_left(jnp.int32(1), 29 - it)
            below = count(lambda sc, pos: (sc == thr) & (pos < cand))
            return jnp.where(below < need, cand, jj)

        j_ref[...] = lax.fori_loop(0, 30, idx_body, jnp.zeros((1, tq), I32))

    jlast = j_ref[...]

    m_ref[...] = jnp.full_like(m_ref, NEG)
    l_ref[...] = jnp.zeros_like(l_ref)
    acc_ref[...] = jnp.zeros_like(acc_ref)

    def att_body(c, carry):
        sc = sc_ref[c]
        pos = key_pos + c * kc
        sel = ((sc > thr) | ((sc == thr) & (pos <= jlast))) & (pos <= q_pos)
        bias = jnp.where(sel, 0.0, NEG)
        s = jnp.einsum('hkd,hdt->hkt', k_ref[0, :, keys_of(c), :], qt_ref[0],
                       preferred_element_type=F32) + bias[None]
        m_old = m_ref[...]
        m_new = jnp.maximum(m_old, jnp.max(s, axis=1, keepdims=True))
        alpha = jnp.exp2(m_old - m_new)
        p = jnp.exp2(s - m_new)
        l_ref[...] = alpha * l_ref[...] + jnp.sum(p, axis=1, keepdims=True)
        pv = jnp.einsum('hdk,hkt->hdt', vt_ref[0, :, c], p.astype(BF16), preferred_element_type=F32)
        acc_ref[...] = alpha * acc_ref[...] + pv
        m_ref[...] = m_new
        return carry

    lax.fori_loop(0, nk, att_body, 0)
    for h in range(N_HEADS):
        o_ref[0, :, h * HEAD_DIM:(h + 1) * HEAD_DIM] = (acc_ref[h] / l_ref[h]).T.astype(BF16)


def _dsa(qit, wit, kib, qt, kb, vtb, tq, topk):
    b, _, _, t = qt.shape
    nq = t // tq
    assert tq % LANES == 0 and (tq // SUBLANES) * nq < (1 << FIELD), "per-slot key counts must fit a packed field"
    hm_t = pl.BlockSpec((1, N_HEADS, HEAD_DIM, tq), lambda i, j: (i, 0, 0, j))
    return pl.pallas_call(
        functools.partial(_dsa_kernel, topk=topk),
        grid=(b, nq),
        in_specs=[hm_t,
                  pl.BlockSpec((1, N_IDX_HEADS, tq), lambda i, j: (i, 0, j)),
                  pl.BlockSpec((1, t, IDX_DIM), lambda i, j: (i, 0, 0)),
                  hm_t,
                  pl.BlockSpec((1, N_HEADS, t, HEAD_DIM), lambda i, j: (i, 0, 0, 0)),
                  pl.BlockSpec((1, N_HEADS, nq, HEAD_DIM, tq), lambda i, j: (i, 0, 0, 0, 0))],
        out_specs=pl.BlockSpec((1, tq, ATTN_W), lambda i, j: (i, j, 0)),
        out_shape=jax.ShapeDtypeStruct((b, t, ATTN_W), BF16),
        scratch_shapes=[pltpu.VMEM((nq, tq, tq), F32), pltpu.VMEM((1, tq), I32),
                        pltpu.VMEM((N_HEADS, 1, tq), F32), pltpu.VMEM((N_HEADS, 1, tq), F32),
                        pltpu.VMEM((N_HEADS, HEAD_DIM, tq), F32)],
        compiler_params=_params("parallel", "arbitrary"),
        name="dsa_prompt",
    )(qit, wit, kib, qt, kb, vtb)


def _merge_kernel(x_ref, conv_ref, attn_ref, mem_ref, wg_ref, bg_ref, wc_ref, wa_ref, wm_ref, wo_ref,
                  g1_ref, b1_ref, wrh_ref, wrl_ref, br_ref, hg_ref, *, alpha):
    x = x_ref[...]
    xb = x.astype(BF16)
    d = x.shape[1]
    cw = 2 * LANES
    blocks = []
    for c0 in range(0, d, cw):
        m = None
        for br, (src, wref) in enumerate(((conv_ref, wc_ref), (attn_ref, wa_ref), (mem_ref, wm_ref))):
            gc = slice(br * d + c0, br * d + c0 + cw)
            g = jax.nn.sigmoid(_dot(xb, wg_ref[:, gc]) + bg_ref[:, gc])
            term = g * _dot(src[...], wref[:, c0:c0 + cw])
            m = term if m is None else m + term
        blocks.append(m.astype(BF16))
    mb = jnp.concatenate(blocks, axis=1)
    h = _layernorm(alpha * x + _dot(mb, wo_ref[...]), g1_ref[...], b1_ref[...])
    hg_ref[:, :d] = h

    hh = h.astype(BF16)
    hl = (h - hh.astype(F32)).astype(BF16)
    lo = _dot(hh, wrh_ref[...]) + _dot(hl, wrh_ref[...]) + _dot(hh, wrl_ref[...]) + br_ref[...]
    lane = lax.broadcasted_iota(I32, lo.shape, 1)
    lanef = lane.astype(F32)
    is_g = (lane >= N_EXPERTS) & (lane < N_EXPERTS + N_GROUPS)
    mg = jnp.max(jnp.where(is_g, lo, -jnp.inf), axis=1, keepdims=True)
    sg = jnp.sum(jnp.where(is_g, jnp.exp(lo - mg), 0.0), axis=1, keepdims=True)
    gw = 1.0 / sg
    gsel = jnp.min(jnp.where(is_g & (lo == mg), lanef, 1e9), axis=1, keepdims=True) - N_EXPERTS
    in_g = (lane < N_EXPERTS) & ((lane >> 3).astype(F32) == gsel)
    me = jnp.max(jnp.where(in_g, lo, -jnp.inf), axis=1, keepdims=True)
    ee = jnp.where(in_g, jnp.exp(lo - me), 0.0)
    pe = jnp.where(in_g, ee / jnp.sum(ee, axis=1, keepdims=True), -1.0)
    p1 = jnp.max(pe, axis=1, keepdims=True)
    i1 = jnp.min(jnp.where(pe == p1, lanef, 1e9), axis=1, keepdims=True)
    pe2 = jnp.where(lanef == i1, -1.0, pe)
    p2 = jnp.max(pe2, axis=1, keepdims=True)
    i2 = jnp.min(jnp.where(pe2 == p2, lanef, 1e9), axis=1, keepdims=True)
    nrm = p1 + p2
    gate = gw * jnp.where(lanef == i1, p1 / nrm, jnp.where(lanef == i2, p2 / nrm, 0.0))
    hg_ref[:, d:] = jnp.where(lane == N_EXPERTS, gsel, gate)


def _merge(x, conv, attn, mem, wts, tm, alpha):
    n, d = x.shape
    tok = lambda w: pl.BlockSpec((tm, w), lambda i: (i, 0))
    const = lambda a: pl.BlockSpec(a.shape, lambda i: (0, 0))
    return pl.pallas_call(
        functools.partial(_merge_kernel, alpha=alpha),
        grid=(n // tm,),
        in_specs=[tok(d), tok(CONV_W), tok(ATTN_W), tok(MEM_W)] + [const(a) for a in wts],
        out_specs=tok(d + LANES),
        out_shape=jax.ShapeDtypeStruct((n, d + LANES), F32),
        compiler_params=_params("parallel"),
        name="merge_ln_router",
    )(x, conv, attn, mem, *wts)


def _route_kernel(rec_ref, pos_ref, tg_ref, cnt_ref, start_ref, carry_ref, *, tile_rows):
    phase = pl.program_id(0)
    i = pl.program_id(1)
    tm = rec_ref.shape[0]
    rec = rec_ref[...]
    lane = lax.broadcasted_iota(I32, rec.shape, 1)
    lanef = lane.astype(F32)
    gsel = jnp.sum(jnp.where(lane == N_EXPERTS, rec, 0.0), axis=1, keepdims=True)
    onehot = jnp.where(lanef == gsel, 1.0, 0.0)

    @pl.when((phase == 0) & (i == 0))
    def _():
        cnt_ref[...] = jnp.zeros_like(cnt_ref)

    @pl.when(phase == 0)
    def _():
        cnt_ref[...] += jnp.sum(onehot, axis=0, keepdims=True)

    @pl.when((phase == 1) & (i == 0))
    def _():
        n_tiles = jnp.ceil(cnt_ref[...] * (1.0 / tile_rows))
        n_tiles = jnp.where(lane[:1] < N_GROUPS, n_tiles, 0.0)
        r = lax.broadcasted_iota(I32, (LANES, LANES), 0)
        c = lax.broadcasted_iota(I32, (LANES, LANES), 1)
        before = jnp.where(r < c, 1.0, 0.0).astype(BF16)
        first = _dot(jnp.broadcast_to(n_tiles, (SUBLANES, LANES)).astype(BF16), before)[:1]
        start_ref[...] = first * tile_rows
        carry_ref[...] = jnp.zeros_like(carry_ref)
        tile = lane[:1].astype(F32)
        total = jnp.sum(n_tiles, axis=1, keepdims=True)
        grp = jnp.zeros((1, LANES), F32) - 1.0
        for g in range(N_GROUPS):
            first_g = jnp.sum(jnp.where(lane[:1] == g, first, 0.0), axis=1, keepdims=True)
            n_g = jnp.sum(jnp.where(lane[:1] == g, n_tiles, 0.0), axis=1, keepdims=True)
            grp = jnp.where((tile >= first_g) & (tile < first_g + n_g), float(g), grp)
        tg_ref[...] = jnp.where(tile < total, grp, -1.0).astype(I32)

    @pl.when(phase == 1)
    def _():
        r = lax.broadcasted_iota(I32, (tm, tm), 0)
        c = lax.broadcasted_iota(I32, (tm, tm), 1)
        earlier = jnp.where(c < r, 1.0, 0.0).astype(BF16)
        rank = _dot(earlier, onehot.astype(BF16))
        slot = jnp.sum(jnp.where(lanef == gsel, rank + carry_ref[...] + start_ref[...], 0.0), axis=1, keepdims=True)
        carry_ref[...] += jnp.sum(onehot, axis=0, keepdims=True)
        hi = jnp.floor(slot * (1.0 / 256.0))
        lo = slot - hi * 256.0
        ones = jnp.ones((SUBLANES, LANES), BF16)
        spread = lambda v: _dot_nt(ones, jnp.where(lane == 0, v, 0.0).astype(BF16))[:1]
        pos_ref[0] = (spread(hi) * 256.0 + spread(lo)).astype(I32)


def _route(hg, d, tm, tile_rows):
    n = hg.shape[0]
    nt = n // tm
    pos, tile_group = pl.pallas_call(
        functools.partial(_route_kernel, tile_rows=tile_rows),
        grid=(2, nt),
        in_specs=[pl.BlockSpec((tm, LANES), lambda p, i: (i, d // LANES))],
        out_specs=[pl.BlockSpec((1, 1, tm), lambda p, i: (i * p, 0, 0)),
                   pl.BlockSpec((1, LANES), lambda p, i: (0, 0))],
        out_shape=[jax.ShapeDtypeStruct((nt, 1, tm), I32), jax.ShapeDtypeStruct((1, LANES), I32)],
        scratch_shapes=[pltpu.VMEM((1, LANES), F32), pltpu.VMEM((1, LANES), F32), pltpu.VMEM((1, LANES), F32)],
        compiler_params=_params("arbitrary", "arbitrary"),
        name="moe_route",
    )(hg)
    return pos.reshape(n), tile_group[0]


def _row_copy(src_ref, dst_ref, sem, src_row, dst_row):
    return pltpu.make_async_copy(src_ref.at[pl.ds(src_row, 1)], dst_ref.at[pl.ds(dst_row, 1)], sem)


def _permute_kernel(idx_ref, src_ref, *rest, scatter):
    dst_ref, sem = rest[-2:]
    ch = idx_ref.shape[2]
    base = pl.program_id(0) * ch

    def rows(t):
        j = idx_ref[0, 0, t]
        return (base + t, j) if scatter else (j, base + t)

    def start(t, carry):
        _row_copy(src_ref, dst_ref, sem, *rows(t)).start()
        return carry

    def wait(t, carry):
        _row_copy(src_ref, dst_ref, sem, *rows(t)).wait()
        return carry

    lax.fori_loop(0, ch, start, 0)
    lax.fori_loop(0, ch, wait, 0)


def _permute_rows(src, idx, n_out, scatter, chunk):
    n_idx = idx.shape[0]
    width = src.shape[1]
    any_space = pl.BlockSpec(memory_space=pl.ANY)
    extra = (jnp.zeros((n_out, width), src.dtype),) if scatter else ()
    return pl.pallas_call(
        functools.partial(_permute_kernel, scatter=scatter),
        grid=(n_idx // chunk,),
        in_specs=[pl.BlockSpec((1, 1, chunk), lambda i: (i, 0, 0), memory_space=pltpu.SMEM), any_space]
                 + [any_space] * len(extra),
        out_specs=any_space,
        out_shape=jax.ShapeDtypeStruct((n_out, width), src.dtype),
        scratch_shapes=[pltpu.SemaphoreType.DMA],
        input_output_aliases={2: 0} if scatter else {},
        compiler_params=_params("arbitrary"),
        name="permute_rows_scatter" if scatter else "permute_rows_gather",
    )(idx.reshape(n_idx // chunk, 1, chunk), src, *extra)


def _moe_group_kernel(tg_ref, x_ref, wu_ref, wd_ref, g2_ref, b2_ref, o_ref, hb_ref, y_ref, *, alpha):
    i = pl.program_id(0)
    e = pl.program_id(1)
    d = o_ref.shape[1]
    de = wd_ref.shape[1]
    grp = tg_ref[i]

    @pl.when(grp < 0)
    def _():
        o_ref[...] = jnp.zeros_like(o_ref)

    @pl.when(grp >= 0)
    def _():
        @pl.when(e == 0)
        def _():
            hb_ref[...] = x_ref[:, :d].astype(BF16)
            y_ref[...] = jnp.zeros_like(y_ref)

        hu = _dot(hb_ref[...], wu_ref[0])
        a = hu[:, :de]
        act = (a * (1.0 / (1.0 + jnp.exp(-a))) * hu[:, de:]).astype(BF16)
        rec = x_ref[:, d:]
        lane = lax.broadcasted_iota(I32, rec.shape, 1)
        ge = jnp.sum(jnp.where(lane == grp * EXPERTS_PER_GROUP + e, rec, 0.0), axis=1, keepdims=True)
        y_ref[...] += ge * _dot(act, wd_ref[0])

        @pl.when(e == pl.num_programs(1) - 1)
        def _():
            o_ref[...] = _layernorm(alpha * x_ref[:, :d] + y_ref[...], g2_ref[...], b2_ref[...])


def _moe_grouped(xs, tile_group, wu, wd, g2, b2, tm, alpha):
    ns, width = xs.shape
    d = width - LANES
    _, _, du = wu.shape
    de = wd.shape[1]
    expert = lambda i, e, tg: (jnp.maximum(tg[i], 0) * EXPERTS_PER_GROUP + e, 0, 0)
    grid_spec = pltpu.PrefetchScalarGridSpec(
        num_scalar_prefetch=1,
        grid=(ns // tm, EXPERTS_PER_GROUP),
        in_specs=[pl.BlockSpec((tm, width), lambda i, e, tg: (i, 0)),
                  pl.BlockSpec((1, d, du), expert),
                  pl.BlockSpec((1, de, d), expert),
                  pl.BlockSpec((1, d), lambda i, e, tg: (0, 0)),
                  pl.BlockSpec((1, d), lambda i, e, tg: (0, 0))],
        out_specs=pl.BlockSpec((tm, d), lambda i, e, tg: (i, 0)),
        scratch_shapes=[pltpu.VMEM((tm, d), BF16), pltpu.VMEM((tm, d), F32)],
    )
    return pl.pallas_call(
        functools.partial(_moe_group_kernel, alpha=alpha),
        grid_spec=grid_spec,
        out_shape=jax.ShapeDtypeStruct((ns, d), F32),
        compiler_params=_params("arbitrary", "arbitrary"),
        name="moe_grouped_ln",
    )(tile_group, xs, wu, wd, g2, b2)


def _moe_kernel(h_ref, gate_ref, wu_ref, wd_ref, g2_ref, b2_ref, o_ref, hb_ref, y_ref, *, alpha):
    e = pl.program_id(1)
    de = wd_ref.shape[1]

    @pl.when(e == 0)
    def _():
        hb_ref[...] = h_ref[...].astype(BF16)
        y_ref[...] = jnp.zeros_like(y_ref)

    hu = _dot(hb_ref[...], wu_ref[0])
    a = hu[:, :de]
    act = (a * (1.0 / (1.0 + jnp.exp(-a))) * hu[:, de:]).astype(BF16)
    gate = gate_ref[...]
    lane = lax.broadcasted_iota(I32, gate.shape, 1)
    ge = jnp.sum(jnp.where(lane == e, gate, 0.0), axis=1, keepdims=True)
    y_ref[...] += ge * _dot(act, wd_ref[0])

    @pl.when(e == pl.num_programs(1) - 1)
    def _():
        o_ref[...] = _layernorm(alpha * h_ref[...] + y_ref[...], g2_ref[...], b2_ref[...])


def _moe(h, gate, wu, wd, g2, b2, tm, alpha):
    n, d = h.shape
    ne, _, du = wu.shape
    de = wd.shape[1]
    return pl.pallas_call(
        functools.partial(_moe_kernel, alpha=alpha),
        grid=(n // tm, ne),
        in_specs=[pl.BlockSpec((tm, d), lambda i, e: (i, 0)),
                  pl.BlockSpec((tm, ne), lambda i, e: (i, 0)),
                  pl.BlockSpec((1, d, du), lambda i, e: (e, 0, 0)),
                  pl.BlockSpec((1, de, d), lambda i, e: (e, 0, 0)),
                  pl.BlockSpec((1, d), lambda i, e: (0, 0)),
                  pl.BlockSpec((1, d), lambda i, e: (0, 0))],
        out_specs=pl.BlockSpec((tm, d), lambda i, e: (i, 0)),
        out_shape=jax.ShapeDtypeStruct((n, d), F32),
        scratch_shapes=[pltpu.VMEM((tm, d), BF16), pltpu.VMEM((tm, d), F32)],
        compiler_params=_params("parallel", "arbitrary"),
        name="moe_ln",
    )(h, gate, wu, wd, g2, b2)


def _sample_inproj_kernel(x_ref, w_ref, wt_ref, cos_ref, sin_ref, cw_ref, p0_ref, p1_ref,
                          conv_ref, z_ref, q_ref, k_ref, v_ref, qi_ref, ki_ref, wi_ref, qm_ref):
    xb = x_ref[...].astype(BF16)
    cos = cos_ref[...]
    sin = sin_ref[...]

    def proj(a, b):
        return _dot(xb, w_ref[:, a:b])

    def proj_rows(a, b):
        return _dot_nt(xb, wt_ref[a:b, :])

    ucc = proj(C_U, C_K)
    z = ucc[:, C_C:C_K] * ucc[:, C_U:C_B]
    cw = cw_ref[...]
    y = p0_ref[...] * cw[0:1, :] + p1_ref[...] * cw[1:2, :] + z * cw[2:3, :]
    conv_ref[...] = (ucc[:, C_B:C_C] * y).astype(BF16)
    z_ref[...] = z
    q_ref[...] = _rope(proj_rows(R_Q, R_QI), cos, sin) * (HEAD_DIM ** -0.5)
    k_ref[...] = _rope(proj(C_K, C_KI), cos, sin)
    v_ref[...] = proj_rows(R_V, R_KI)
    qi_ref[...] = _rope(proj_rows(R_QI, R_K), cos, sin) * (IDX_DIM ** -0.5)
    ki_ref[...] = _rope(proj(C_KI, C_QM), cos, sin)[:, :IDX_DIM]
    wi_ref[...] = proj_rows(R_WI, R_END)[:, :N_IDX_HEADS] * (N_IDX_HEADS ** -0.5)
    qm_ref[...] = proj(C_QM, C_END) * (HEAD_DIM ** -0.5)


def _sample_inproj(x, wcat, wt, cos, sin, conv_w, p0, p1):
    n, d = x.shape
    args = (x, wcat, wt, cos, sin, conv_w, p0, p1)
    widths = (CONV_W, CONV_W, ATTN_W, ATTN_W, ATTN_W, N_IDX_HEADS * IDX_DIM, IDX_DIM, N_IDX_HEADS, MEM_W)
    dtypes = (BF16,) + (F32,) * 8
    full = lambda shape: pl.BlockSpec(shape, lambda i: (0,) * len(shape))
    return pl.pallas_call(
        _sample_inproj_kernel,
        grid=(1,),
        in_specs=[full(a.shape) for a in args],
        out_specs=[full((n, w)) for w in widths],
        out_shape=[jax.ShapeDtypeStruct((n, w), dt) for w, dt in zip(widths, dtypes)],
        compiler_params=_params("arbitrary"),
        name="sample_inproj",
    )(*args)


def _head_mask(rows, width):
    r = lax.broadcasted_iota(I32, (rows, width), 0)
    c = lax.broadcasted_iota(I32, (rows, width), 1)
    return (c >> 6) == r


def _sample_mem_kernel(qm_ref, mk_ref, mv_ref, o_ref):
    hmask = _head_mask(8, MEM_W)
    qbd = jnp.where(hmask, jnp.broadcast_to(qm_ref[0], (8, MEM_W)), 0.0).astype(BF16)
    p = _softmax_rows(_dot_nt(qbd, mk_ref[0].astype(BF16)))
    o = _dot(p.astype(BF16), mv_ref[0].astype(BF16))
    o_ref[0] = jnp.sum(jnp.where(hmask, o, 0.0), axis=0, keepdims=True).astype(BF16)


def _sample_mem(qm, mk, mv):
    n, n_mem, w = mk.shape
    return pl.pallas_call(
        _sample_mem_kernel,
        grid=(n,),
        in_specs=[pl.BlockSpec((1, 1, w), lambda i: (i, 0, 0)),
                  pl.BlockSpec((1, n_mem, w), lambda i: (i, 0, 0)),
                  pl.BlockSpec((1, n_mem, w), lambda i: (i, 0, 0))],
        out_specs=pl.BlockSpec((1, 1, w), lambda i: (i, 0, 0)),
        out_shape=jax.ShapeDtypeStruct((n, 1, w), BF16),
        compiler_params=_params("parallel"),
        name="sample_mem_attn",
    )(qm, mk, mv)


def _sample_scores_kernel(pt_ref, *refs, pages):
    page_refs = refs[:pages]
    qi_ref, wi_ref, o_ref = refs[pages:]
    keys_t = jnp.concatenate([r[0] for r in page_refs], axis=1).astype(BF16)
    s = _dot(qi_ref[0].astype(BF16), keys_t)
    o_ref[0, 0] = jnp.sum(wi_ref[0] * jnp.maximum(s, 0.0), axis=0, keepdims=True)


def _sample_scores(page_table, cki_t, qi, wi, pages):
    n, n_pages = page_table.shape
    ng = n_pages // pages
    width = pages * PAGE_SIZE

    def page_spec(j):
        return pl.BlockSpec((1, IDX_DIM, PAGE_SIZE), lambda s, pt: (pt[s // ng, (s % ng) * pages + j], 0, 0))

    grid_spec = pltpu.PrefetchScalarGridSpec(
        num_scalar_prefetch=1,
        grid=(n * ng,),
        in_specs=[page_spec(j) for j in range(pages)] + [
            pl.BlockSpec((1, N_IDX_HEADS, IDX_DIM), lambda s, pt: (s // ng, 0, 0)),
            pl.BlockSpec((1, N_IDX_HEADS, 1), lambda s, pt: (s // ng, 0, 0))],
        out_specs=pl.BlockSpec((1, 1, 1, width), lambda s, pt: (s // ng, s % ng, 0, 0)),
    )
    out = pl.pallas_call(
        functools.partial(_sample_scores_kernel, pages=pages),
        grid_spec=grid_spec,
        out_shape=jax.ShapeDtypeStruct((n, ng, 1, width), F32),
        compiler_params=_params("arbitrary"),
        name="sample_idx_scores",
    )(page_table, *([cki_t] * pages), qi, wi)
    return out.reshape(n, n_pages * PAGE_SIZE)


def _sample_select_kernel(sc_ref, qi_ref, ki_ref, wi_ref, bias_ref, bnew_ref, j_ref, *, topk):
    n, past = sc_ref.shape
    qi = qi_ref[...].astype(BF16).astype(F32)
    ki = ki_ref[...].astype(BF16).astype(F32)
    s_new = jnp.maximum(jnp.sum(qi * ki, axis=2), 0.0)
    sc_new = jnp.sum(wi_ref[...] * s_new, axis=1, keepdims=True)
    idx = lax.broadcasted_iota(I32, (n, past), 1)

    def count(pred):
        one = jnp.where(pred(sc_ref[...], idx), 1.0, 0.0)
        return jnp.sum(one, axis=1, keepdims=True) + jnp.where(pred(sc_new, past), 1.0, 0.0)

    def bit_body(it, carry):
        thr, cnt_thr = carry
        cand = thr ^ lax.shift_left(jnp.int32(1), 31 - it)
        cand_f = _key_to_float(cand)
        cnt = count(lambda sc, ii: sc >= cand_f)
        ok = cnt >= topk
        return jnp.where(ok, cand, thr), jnp.where(ok, cnt, cnt_thr)

    thr_key, cnt_thr = lax.fori_loop(
        0, 32, bit_body, (jnp.full((n, 1), INT_MIN, I32), jnp.full((n, 1), float(topk), F32)))
    thr = _key_to_float(thr_key)

    j_ref[...] = jnp.full((n, 1), BIG_IDX, I32)

    @pl.when(jnp.max(jnp.where(cnt_thr > topk, 1.0, 0.0)) > 0.0)
    def _():
        need = topk - count(lambda sc, ii: sc > thr)

        def idx_body(it, jj):
            cand = jj | lax.shift_left(jnp.int32(1), 29 - it)
            below = count(lambda sc, ii: (sc == thr) & (ii < cand))
            return jnp.where(below < need, cand, jj)

        j_ref[...] = lax.fori_loop(0, 30, idx_body, jnp.zeros((n, 1), I32))

    jlast = j_ref[...]
    sc = sc_ref[...]
    bias_ref[...] = jnp.where((sc > thr) | ((sc == thr) & (idx <= jlast)), 0.0, NEG)
    bnew_ref[...] = jnp.where((sc_new > thr) | ((sc_new == thr) & (past <= jlast)), 0.0, NEG)


def _sample_select(scores, qi, ki, wi, topk):
    n, past = scores.shape
    args = (scores, qi, ki, wi)
    full = lambda shape: pl.BlockSpec(shape, lambda i: (0,) * len(shape))
    return pl.pallas_call(
        functools.partial(_sample_select_kernel, topk=topk),
        grid=(1,),
        in_specs=[full(a.shape) for a in args],
        out_specs=[full((n, past)), full((n, 1))],
        out_shape=[jax.ShapeDtypeStruct((n, past), F32), jax.ShapeDtypeStruct((n, 1), F32)],
        scratch_shapes=[pltpu.VMEM((n, 1), I32)],
        compiler_params=_params("arbitrary"),
        name="sample_topk_select",
    )(*args)


def _sample_attn_kernel(pt_ref, *refs, pages, nch):
    k_refs = refs[:pages]
    v_refs = refs[pages:2 * pages]
    q_ref, bias_ref, bnew_ref, kn_ref, vn_ref, o_ref, m_ref, l_ref, acc_ref = refs[2 * pages:]
    c = pl.program_id(0) % nch

    @pl.when(c == 0)
    def _():
        m_ref[...] = jnp.full_like(m_ref, NEG)
        l_ref[...] = jnp.zeros_like(l_ref)
        acc_ref[...] = jnp.zeros_like(acc_ref)

    rows = []
    for h in range(N_HEADS):
        qh = q_ref[0, h]
        rows.append(jnp.concatenate([jnp.sum(r[0, h] * qh, axis=0, keepdims=True) for r in k_refs], axis=1))
    s = jnp.concatenate(rows, axis=0) + bias_ref[0, 0]
    m_old = m_ref[...]
    m_new = jnp.maximum(m_old, jnp.max(s, axis=1, keepdims=True))
    alpha = jnp.exp(m_old - m_new)
    p = jnp.exp(s - m_new)
    l_ref[...] = alpha * l_ref[...] + jnp.sum(p, axis=1, keepdims=True)
    m_ref[...] = m_new
    for h in range(N_HEADS):
        acc = alpha[h:h + 1, :] * acc_ref[h]
        for j, r in enumerate(v_refs):
            acc = acc + r[0, h] * p[h:h + 1, j * PAGE_SIZE:(j + 1) * PAGE_SIZE]
        acc_ref[h] = acc

    @pl.when(c == nch - 1)
    def _():
        bn = bnew_ref[0]
        for h in range(N_HEADS):
            s_n = jnp.sum(q_ref[0, h] * kn_ref[0, h], axis=0, keepdims=True) + bn
            m0 = m_ref[h:h + 1, :]
            m1 = jnp.maximum(m0, s_n)
            a1 = jnp.exp(m0 - m1)
            p_n = jnp.exp(s_n - m1)
            l1 = a1 * l_ref[h:h + 1, :] + p_n
            out = a1 * jnp.sum(acc_ref[h], axis=1, keepdims=True) + p_n * vn_ref[0, h]
            o_ref[0, h] = out / l1


def _sample_attn(page_table, ck_t, cv_t, q, bias, bnew, kn, vn, pages):
    n, n_pages = page_table.shape
    nch = n_pages // pages
    width = pages * PAGE_SIZE

    def page_spec(j):
        return pl.BlockSpec((1, N_HEADS, HEAD_DIM, PAGE_SIZE),
                            lambda s, pt: (pt[s // nch, (s % nch) * pages + j], 0, 0, 0))

    per_seq = pl.BlockSpec((1, N_HEADS, HEAD_DIM, 1), lambda s, pt: (s // nch, 0, 0, 0))
    grid_spec = pltpu.PrefetchScalarGridSpec(
        num_scalar_prefetch=1,
        grid=(n * nch,),
        in_specs=[page_spec(j) for j in range(pages)] * 2 + [
            per_seq,
            pl.BlockSpec((1, 1, 1, width), lambda s, pt: (s // nch, s % nch, 0, 0)),
            pl.BlockSpec((1, 1, 1), lambda s, pt: (s // nch, 0, 0)),
            per_seq, per_seq],
        out_specs=per_seq,
        scratch_shapes=[pltpu.VMEM((N_HEADS, 1), F32), pltpu.VMEM((N_HEADS, 1), F32),
                        pltpu.VMEM((N_HEADS, HEAD_DIM, PAGE_SIZE), F32)],
    )
    return pl.pallas_call(
        functools.partial(_sample_attn_kernel, pages=pages, nch=nch),
        grid_spec=grid_spec,
        out_shape=jax.ShapeDtypeStruct((n, N_HEADS, HEAD_DIM, 1), F32),
        compiler_params=_params("arbitrary"),
        name="sample_attn",
    )(page_table, *([ck_t] * pages), *([cv_t] * pages), q, bias.reshape(n, nch, 1, width), bnew, kn, vn)


def _rope_tables(pos):
    inv = ROPE_THETA ** (-jnp.arange(HALF, dtype=F32) / HALF)
    ang = pos.astype(F32)[:, None] * inv[None, :]
    cos = jnp.cos(ang)
    sin = jnp.sin(ang)
    return jnp.concatenate([cos] * 4, axis=1), jnp.concatenate([-sin, sin] * 2, axis=1), cos.T, sin.T


def _largest_divisor(n, cap):
    for c in range(min(cap, n), 0, -1):
        if n % c == 0:
            return c
    return 1


def _pack_layer_weights(l, w_in, b_gate, w_br_conv, w_br_attn, w_br_mem, w_o, ln1_g, ln1_b, w_group, b_group,
                        w_expert, b_expert):
    d = w_in.shape[1]
    sizes = (CONV_W, CONV_W, CONV_W, ATTN_W, ATTN_W, ATTN_W, N_IDX_HEADS * IDX_DIM, IDX_DIM, N_IDX_HEADS, MEM_W,
             N_BRANCH * d)
    offs = [0]
    for s in sizes:
        offs.append(offs[-1] + s)
    col = lambda i: w_in[l][:, offs[i]:offs[i + 1]]
    u, c_b, c_c, q, k, v, qi, ki, wi, qm, gates = (col(i) for i in range(len(sizes)))
    zeros = lambda n: jnp.zeros((d, n), F32)
    wcat = jnp.concatenate([u, c_b, c_c, k, ki, zeros(LANES - IDX_DIM), qm], axis=1).astype(BF16)
    wt = jnp.concatenate([q, qi, k, v, ki, wi, zeros(16 - N_IDX_HEADS)], axis=1).T.astype(BF16)
    wr = jnp.concatenate([w_expert[l], w_group[l], zeros(LANES - N_EXPERTS - N_GROUPS)], axis=1)
    wrh = wr.astype(BF16)
    wrl = (wr - wrh.astype(F32)).astype(BF16)
    br = jnp.concatenate([b_expert[l], b_group[l], jnp.zeros((LANES - N_EXPERTS - N_GROUPS,), F32)])[None, :]
    merge_w = (gates.astype(BF16), b_gate[l][None, :], w_br_conv[l].astype(BF16), w_br_attn[l].astype(BF16),
               w_br_mem[l].astype(BF16), w_o[l].astype(BF16), ln1_g[l][None, :], ln1_b[l][None, :], wrh, wrl, br)
    return wcat, wt, merge_w


def kernel(x_prompt, x_sample, mem_prompt, cache_k, cache_v, cache_k_idx, state_conv, cache_mem_k, cache_mem_v,
           page_table, w_in, b_gate, conv_w, w_br_conv, w_br_attn, w_br_mem, w_o, w_mem_k, w_mem_v, ln1_g, ln1_b,
           w_group, b_group, w_expert, b_expert, w_up, w_down, ln2_g, ln2_b):
    depth = w_in.shape[0]
    b, s_len, d = x_prompt.shape
    bd, t_new, _ = x_sample.shape
    assert t_new == 1, "the sample group decodes one token per sequence"
    n_pages = page_table.shape[1]
    past = n_pages * PAGE_SIZE
    alpha = (2.0 * depth) ** 0.25

    tq = _largest_divisor(s_len, 256)
    tm_in = _largest_divisor(s_len, 512)
    assert tm_in % tq == 0
    n_tok = b * s_len
    tm_merge = _largest_divisor(n_tok, 512)
    tm_moe = _largest_divisor(n_tok, 1024)
    perm_chunk = _largest_divisor(n_tok, 2048)
    assert n_tok // tm_moe + N_GROUPS <= LANES, "the tile -> group map is one lane vector"
    topk_p = min(TOPK_MAX, s_len // 4)
    topk_s = min(TOPK_MAX, (past + t_new) // 4)
    pages_idx = _largest_divisor(n_pages, 16)
    pages_att = _largest_divisor(n_pages, 16)

    cos_p, sin_p, cost_p, sint_p = _rope_tables(jnp.arange(s_len, dtype=jnp.int32))
    cos_s, sin_s, _, _ = _rope_tables(past + jnp.arange(t_new, dtype=jnp.int32))

    hp = x_prompt
    hs = x_sample.reshape(bd, d)
    outs = [[] for _ in range(10)]
    for l in range(depth):
        wcat, wt, merge_w = _pack_layer_weights(l, w_in, b_gate, w_br_conv, w_br_attn, w_br_mem, w_o, ln1_g, ln1_b,
                                                w_group, b_group, w_expert, b_expert)
        wu = w_up[l].astype(BF16)
        wd = w_down[l].astype(BF16)
        g2, b2 = ln2_g[l][None, :], ln2_b[l][None, :]

        mk, mv = _memkv(mem_prompt, w_mem_k[l].astype(BF16), w_mem_v[l].astype(BF16))
        (conv_o, mem_o, qt_b, qit_b, wit_p, kt_p, vt_p, k_b, vt_b, kit_p, ki_b, conv_new) = _inproj(
            hp, wcat, wt, cos_p, sin_p, cost_p, sint_p, conv_w[l], mk, mv, tm_in, tq)
        attn_o = _dsa(qit_b, wit_p, ki_b, qt_b, k_b, vt_b, tq, topk_p)
        hg = _merge(hp.reshape(n_tok, d), conv_o.reshape(n_tok, CONV_W), attn_o.reshape(n_tok, ATTN_W),
                    mem_o.reshape(n_tok, MEM_W), merge_w, tm_merge, alpha)
        slot, tile_group = _route(hg, d, tm_moe, tm_moe)
        n_sorted = n_tok + N_GROUPS * tm_moe
        xs = _permute_rows(hg, slot, n_sorted, True, perm_chunk)
        ys = _moe_grouped(xs, tile_group[:n_sorted // tm_moe], wu, wd, g2, b2, tm_moe, alpha)
        hp = _permute_rows(ys, slot, n_tok, False, perm_chunk).reshape(b, s_len, d)
        for lst, val in zip(outs[:6], (jnp.transpose(kt_p, (0, 3, 1, 2)), jnp.transpose(vt_p, (0, 3, 1, 2)),
                                       jnp.transpose(kit_p, (0, 2, 1)), conv_new,
                                       mk.reshape(b, -1, N_MEM_HEADS, HEAD_DIM),
                                       mv.reshape(b, -1, N_MEM_HEADS, HEAD_DIM))):
            lst.append(val)

        st = state_conv[l].astype(F32)
        (conv_s, z_s, q_s, k_s, v_s, qi_s, ki_s, wi_s, qm_s) = _sample_inproj(
            hs, wcat, wt, cos_s, sin_s, conv_w[l], st[:, 0, :], st[:, 1, :])
        mem_s = _sample_mem(qm_s.reshape(bd, 1, MEM_W), cache_mem_k[l].reshape(bd, -1, MEM_W),
                            cache_mem_v[l].reshape(bd, -1, MEM_W))
        qi3 = qi_s.reshape(bd, N_IDX_HEADS, IDX_DIM)
        wi3 = wi_s.reshape(bd, N_IDX_HEADS, 1)
        scores = _sample_scores(page_table, jnp.transpose(cache_k_idx[l], (0, 2, 1)), qi3, wi3, pages_idx)
        bias, bnew = _sample_select(scores, qi3, ki_s.reshape(bd, 1, IDX_DIM), wi_s, topk_s)
        col4 = lambda a: a.reshape(bd, N_HEADS, HEAD_DIM, 1)
        attn_s = _sample_attn(page_table, jnp.transpose(cache_k[l], (0, 2, 3, 1)),
                              jnp.transpose(cache_v[l], (0, 2, 3, 1)), col4(q_s), bias, bnew.reshape(bd, 1, 1),
                              col4(k_s), col4(v_s), pages_att)
        hgs = _merge(hs, conv_s, attn_s.reshape(bd, ATTN_W).astype(BF16), mem_s.reshape(bd, MEM_W), merge_w,
                     bd, alpha)
        hs = _moe(hgs[:, :d], hgs[:, d:d + N_EXPERTS], wu, wd, g2, b2, bd, alpha)
        for lst, val in zip(outs[6:], (k_s.reshape(bd, t_new, N_HEADS, HEAD_DIM),
                                       v_s.reshape(bd, t_new, N_HEADS, HEAD_DIM), ki_s.reshape(bd, t_new, IDX_DIM),
                                       jnp.stack([st[:, 1, :], z_s], axis=1))):
            lst.append(val)

    return (hp, hs.reshape(bd, t_new, d)) + tuple(jnp.stack(o) for o in outs)
```

```python
import functools

import jax
import jax.numpy as jnp
from jax import lax
from jax.experimental import pallas as pl
from jax.experimental.pallas import tpu as pltpu

F32 = jnp.float32
BF16 = jnp.bfloat16
I32 = jnp.int32

HEAD_DIM = 64
HALF = HEAD_DIM // 2
N_HEADS = 8
ATTN_W = N_HEADS * HEAD_DIM
N_IDX_HEADS = 8
IDX_DIM = 64
TOPK_MAX = 256
N_MEM_HEADS = 4
MEM_W = N_MEM_HEADS * HEAD_DIM
CONV_W = 256
CONV_K = 3
N_BRANCH = 3
N_GROUPS = 4
EXPERTS_PER_GROUP = 8
N_EXPERTS = N_GROUPS * EXPERTS_PER_GROUP
PAGE_SIZE = 128
ROPE_THETA = 10000.0
LN_EPS = 1e-5
NEG = -1e30
LANES = 128
SUBLANES = 8

C_U, C_B, C_C = 0, CONV_W, 2 * CONV_W
C_K = 3 * CONV_W
C_KI = C_K + ATTN_W
C_QM = C_KI + LANES
C_END = C_QM + MEM_W
R_Q = 0
R_QI = R_Q + ATTN_W
R_K = R_QI + N_IDX_HEADS * IDX_DIM
R_V = R_K + ATTN_W
R_KI = R_V + ATTN_W
R_WI = R_KI + IDX_DIM
R_END = R_WI + 16

INT_MIN = -(2 ** 31)
KEY_OF_NEG_INF = -2139095041
KEY_OF_POS_INF = 2139095040
BIG_IDX = 2 ** 30
FIELD = 10
Q_SCALE_LOG2 = HEAD_DIM ** -0.5 * 1.4426950408889634
VMEM_LIMIT = 56 * 1024 * 1024

_NT = (((1,), (1,)), ((), ()))


def _dot(a, b):
    return jnp.dot(a, b, preferred_element_type=F32)


def _dot_nt(a, b):
    return lax.dot_general(a, b, _NT, preferred_element_type=F32)


def _params(*sem):
    return pltpu.CompilerParams(dimension_semantics=sem, vmem_limit_bytes=VMEM_LIMIT)


def _rope(x, cos, sin):
    w = x.shape[1]
    lane = lax.broadcasted_iota(I32, x.shape, 1)
    swapped = jnp.where((lane & 63) < 32, pltpu.roll(x, w - 32, 1), pltpu.roll(x, 32, 1))
    reps = w // LANES
    if reps > 1:
        cos = jnp.concatenate([cos] * reps, axis=1)
        sin = jnp.concatenate([sin] * reps, axis=1)
    return x * cos + swapped * sin


def _rope_t(xt, cos_t, sin_t):
    x1, x2 = xt[:HALF, :], xt[HALF:, :]
    return x1 * cos_t - x2 * sin_t, x1 * sin_t + x2 * cos_t


def _order_key(x):
    b = lax.bitcast_convert_type(x, I32)
    return b ^ ((b >> 31) & 0x7FFFFFFF)


def _key_to_float(key):
    key = jnp.clip(key, KEY_OF_NEG_INF, KEY_OF_POS_INF)
    return lax.bitcast_convert_type(key ^ ((key >> 31) & 0x7FFFFFFF), F32)


def _layernorm(x, g, b):
    mu = jnp.mean(x, axis=-1, keepdims=True)
    xc = x - mu
    var = jnp.mean(xc * xc, axis=-1, keepdims=True)
    return xc * lax.rsqrt(var + LN_EPS) * g + b


def _softmax_rows(s):
    m = jnp.max(s, axis=-1, keepdims=True)
    e = jnp.exp(s - m)
    return e / jnp.sum(e, axis=-1, keepdims=True)


def _memkv_kernel(mem_ref, wk_ref, wv_ref, mk_ref, mv_ref):
    m = mem_ref[0].astype(BF16)
    mk_ref[0] = _dot(m, wk_ref[...])
    mv_ref[0] = _dot(m, wv_ref[...])


def _memkv(mem, wk, wv):
    b, n_mem, d = mem.shape
    return pl.pallas_call(
        _memkv_kernel,
        grid=(b,),
        in_specs=[pl.BlockSpec((1, n_mem, d), lambda i: (i, 0, 0)),
                  pl.BlockSpec((d, MEM_W), lambda i: (0, 0)),
                  pl.BlockSpec((d, MEM_W), lambda i: (0, 0))],
        out_specs=[pl.BlockSpec((1, n_mem, MEM_W), lambda i: (i, 0, 0))] * 2,
        out_shape=[jax.ShapeDtypeStruct((b, n_mem, MEM_W), F32)] * 2,
        compiler_params=_params("parallel"),
        name="memkv",
    )(mem, wk, wv)


def _inproj_kernel(x_ref, w_ref, wt_ref, cos_ref, sin_ref, cost_ref, sint_ref, cw_ref, mk_ref, mv_ref,
                   conv_ref, mem_ref, qt_ref, qit_ref, wit_ref, kt_ref, vt_ref, kb_ref, vtb_ref, kit_ref, kib_ref,
                   cnew_ref, zprev_ref):
    j = pl.program_id(1)
    tm = x_ref.shape[1]
    kc = vtb_ref.shape[4]
    xb = x_ref[0].astype(BF16)
    cos = cos_ref[...]
    sin = sin_ref[...]
    cos_t = cost_ref[...]
    sin_t = sint_ref[...]

    def proj(a, b):
        return _dot(xb, w_ref[:, a:b])

    def proj_t(a, b):
        return _dot_nt(wt_ref[a:b, :], xb)

    @pl.when(j == 0)
    def _():
        zprev_ref[...] = jnp.zeros_like(zprev_ref)

    ucc = proj(C_U, C_K)
    z = ucc[:, C_C:C_K] * ucc[:, C_U:C_B]
    prev = zprev_ref[...]
    row = lax.broadcasted_iota(I32, z.shape, 0)
    z1 = jnp.where(row == 0, prev[7:8, :], pltpu.roll(z, 1, 0))
    z2 = jnp.where(row == 0, prev[6:7, :], jnp.where(row == 1, prev[7:8, :], pltpu.roll(z, 2, 0)))
    cw = cw_ref[...]
    y = z2 * cw[0:1, :] + z1 * cw[1:2, :] + z * cw[2:3, :]
    conv_ref[0] = (ucc[:, C_B:C_C] * y).astype(BF16)
    zprev_ref[...] = z[tm - 8:tm, :]
    cnew_ref[0] = z[tm - 2:tm, :]

    k = _rope(proj(C_K, C_KI), cos, sin)
    for h in range(N_HEADS):
        kb_ref[0, h] = k[:, h * HEAD_DIM:(h + 1) * HEAD_DIM].astype(BF16)
    kib_ref[0] = _rope(proj(C_KI, C_QM), cos, sin)[:, :IDX_DIM].astype(BF16)

    qt = proj_t(R_Q, R_QI)
    qit = proj_t(R_QI, R_K)
    kt = proj_t(R_K, R_V)
    vt = proj_t(R_V, R_KI)
    for h in range(N_HEADS):
        hs = slice(h * HEAD_DIM, (h + 1) * HEAD_DIM)
        o1, o2 = _rope_t(qt[hs, :], cos_t, sin_t)
        qt_ref[0, h, :HALF, :] = (o1 * Q_SCALE_LOG2).astype(BF16)
        qt_ref[0, h, HALF:, :] = (o2 * Q_SCALE_LOG2).astype(BF16)
        o1, o2 = _rope_t(qit[hs, :], cos_t, sin_t)
        qit_ref[0, h, :HALF, :] = (o1 * (IDX_DIM ** -0.5)).astype(BF16)
        qit_ref[0, h, HALF:, :] = (o2 * (IDX_DIM ** -0.5)).astype(BF16)
        o1, o2 = _rope_t(kt[hs, :], cos_t, sin_t)
        kt_ref[0, h, :HALF, :] = o1
        kt_ref[0, h, HALF:, :] = o2
        vt_ref[0, h] = vt[hs, :]
        for cc in range(tm // kc):
            vtb_ref[0, h, cc] = vt[hs, cc * kc:(cc + 1) * kc].astype(BF16)
    o1, o2 = _rope_t(proj_t(R_KI, R_WI), cos_t, sin_t)
    kit_ref[0, :HALF, :] = o1
    kit_ref[0, HALF:, :] = o2
    wit_ref[0] = proj_t(R_WI, R_END)[:N_IDX_HEADS, :] * (N_IDX_HEADS ** -0.5)

    qm = (proj(C_QM, C_END) * (HEAD_DIM ** -0.5)).astype(BF16)
    mk = mk_ref[0].astype(BF16)
    mv = mv_ref[0].astype(BF16)
    outs = []
    for h in range(N_MEM_HEADS):
        sl = slice(h * HEAD_DIM, (h + 1) * HEAD_DIM)
        p = _softmax_rows(_dot_nt(qm[:, sl], mk[:, sl]))
        outs.append(_dot(p.astype(BF16), mv[:, sl]))
    mem_ref[0] = jnp.concatenate(outs, axis=1).astype(BF16)


def _inproj(x, wcat, wt, cos, sin, cos_t, sin_t, conv_w, mk, mv, tm, kc):
    b, t, d = x.shape
    n_mem = mk.shape[1]
    grid = (b, t // tm)
    tok = lambda w: pl.BlockSpec((1, tm, w), lambda i, j: (i, j, 0))
    hm_t = pl.BlockSpec((1, N_HEADS, HEAD_DIM, tm), lambda i, j: (i, 0, 0, j))
    const2 = lambda r, c: pl.BlockSpec((r, c), lambda i, j: (0, 0))
    hm_t_shape = lambda dt: jax.ShapeDtypeStruct((b, N_HEADS, HEAD_DIM, t), dt)
    out_shape = [
        jax.ShapeDtypeStruct((b, t, CONV_W), BF16),
        jax.ShapeDtypeStruct((b, t, MEM_W), BF16),
        hm_t_shape(BF16), hm_t_shape(BF16),
        jax.ShapeDtypeStruct((b, N_IDX_HEADS, t), F32),
        hm_t_shape(F32), hm_t_shape(F32),
        jax.ShapeDtypeStruct((b, N_HEADS, t, HEAD_DIM), BF16),
        jax.ShapeDtypeStruct((b, N_HEADS, t // kc, HEAD_DIM, kc), BF16),
        jax.ShapeDtypeStruct((b, IDX_DIM, t), F32),
        jax.ShapeDtypeStruct((b, t, IDX_DIM), BF16),
        jax.ShapeDtypeStruct((b, CONV_K - 1, CONV_W), F32),
    ]
    out_specs = [tok(CONV_W), tok(MEM_W), hm_t, hm_t,
                 pl.BlockSpec((1, N_IDX_HEADS, tm), lambda i, j: (i, 0, j)),
                 hm_t, hm_t,
                 pl.BlockSpec((1, N_HEADS, tm, HEAD_DIM), lambda i, j: (i, 0, j, 0)),
                 pl.BlockSpec((1, N_HEADS, tm // kc, HEAD_DIM, kc), lambda i, j: (i, 0, j, 0, 0)),
                 pl.BlockSpec((1, IDX_DIM, tm), lambda i, j: (i, 0, j)),
                 tok(IDX_DIM),
                 pl.BlockSpec((1, CONV_K - 1, CONV_W), lambda i, j: (i, 0, 0))]
    return pl.pallas_call(
        _inproj_kernel,
        grid=grid,
        in_specs=[tok(d), const2(d, C_END), const2(R_END, d),
                  pl.BlockSpec((tm, LANES), lambda i, j: (j, 0)), pl.BlockSpec((tm, LANES), lambda i, j: (j, 0)),
                  pl.BlockSpec((HALF, tm), lambda i, j: (0, j)), pl.BlockSpec((HALF, tm), lambda i, j: (0, j)),
                  const2(CONV_K, CONV_W),
                  pl.BlockSpec((1, n_mem, MEM_W), lambda i, j: (i, 0, 0)),
                  pl.BlockSpec((1, n_mem, MEM_W), lambda i, j: (i, 0, 0))],
        out_specs=out_specs,
        out_shape=out_shape,
        scratch_shapes=[pltpu.VMEM((8, CONV_W), F32)],
        compiler_params=_params("parallel", "arbitrary"),
        name="inproj",
    )(x, wcat, wt, cos, sin, cos_t, sin_t, conv_w, mk, mv)


def _dsa_kernel(qit_ref, wit_ref, ki_ref, qt_ref, k_ref, vt_ref, o_ref,
                sc_ref, j_ref, m_ref, l_ref, acc_ref, *, topk):
    tq = qt_ref.shape[3]
    kc = sc_ref.shape[1]
    i = pl.program_id(1)
    nk = i + 1
    t0 = i * tq
    wt = wit_ref[0]
    key_pos = lax.broadcasted_iota(I32, (kc, tq), 0)
    q_pos = lax.broadcasted_iota(I32, (kc, tq), 1) + t0

    def keys_of(c):
        return pl.ds(pl.multiple_of(c * kc, kc), kc)

    def score_body(c, carry):
        kic = ki_ref[0, keys_of(c), :]
        acc = jnp.zeros((kc, tq), F32)
        for h in range(N_IDX_HEADS):
            acc = acc + wt[h:h + 1, :] * jnp.maximum(_dot(kic, qit_ref[0, h]), 0.0)
        sc_ref[c] = jnp.where(key_pos + c * kc <= q_pos, acc, -jnp.inf)
        return carry

    lax.fori_loop(0, nk, score_body, 0)

    def fold(v):
        return jnp.sum(v.reshape(kc // SUBLANES, SUBLANES, tq), axis=0)

    def count(pred):
        def body(c, cnt):
            return cnt + fold(jnp.where(pred(sc_ref[c], key_pos + c * kc), 1.0, 0.0))
        cnt = lax.fori_loop(0, nk, body, jnp.zeros((SUBLANES, tq), F32))
        return jnp.sum(cnt, axis=0, keepdims=True)

    def count3(c1, c2, c3):
        def body(c, acc):
            sc = sc_ref[c]
            v = jnp.where(sc >= c3, 1 + (1 << FIELD) + (1 << 2 * FIELD),
                          jnp.where(sc >= c2, 1 + (1 << FIELD), jnp.where(sc >= c1, 1, 0)))
            return acc + fold(v)
        acc = lax.fori_loop(0, nk, body, jnp.zeros((SUBLANES, tq), I32))
        mask = (1 << FIELD) - 1
        return [jnp.sum(f.astype(F32), axis=0, keepdims=True)
                for f in (acc & mask, (acc >> FIELD) & mask, acc >> 2 * FIELD)]

    def bit_body(it, carry):
        thr, cnt_thr = carry
        hi = lax.shift_left(jnp.int32(1), 31 - 2 * it)
        lo = lax.shift_left(jnp.int32(1), 30 - 2 * it)
        cands = (thr ^ lo, thr ^ hi, thr ^ (hi | lo))
        cnts = count3(*[_key_to_float(cj) for cj in cands])
        for cj, nj in zip(cands, cnts):
            ok = nj >= topk
            thr = jnp.where(ok, cj, thr)
            cnt_thr = jnp.where(ok, nj, cnt_thr)
        return thr, cnt_thr

    thr_key, cnt_thr = lax.fori_loop(
        0, 16, bit_body, (jnp.full((1, tq), INT_MIN, I32), jnp.full((1, tq), float(topk), F32)))
    thr = _key_to_float(thr_key)

    j_ref[...] = jnp.full_like(j_ref, BIG_IDX)
    tie_q = jnp.where((thr_key > KEY_OF_NEG_INF) & (cnt_thr > topk), 1.0, 0.0)

    @pl.when(jnp.max(tie_q) > 0.0)
    def _():
        need = topk - count(lambda sc, pos: sc > thr)

        def idx_body(it, jj):
            cand = jj | lax.shift_left(jnp.int32(1), 29 - it)
            below = count(lambda sc, pos: (sc == thr) & (pos < cand))
            return jnp.where(below < need, cand, jj)

        j_ref[...] = lax.fori_loop(0, 30, idx_body, jnp.zeros((1, tq), I32))

    jlast = j_ref[...]

    m_ref[...] = jnp.full_like(m_ref, NEG)
    l_ref[...] = jnp.zeros_like(l_ref)
    acc_ref[...] = jnp.zeros_like(acc_ref)

    def att_body(c, carry):
        sc = sc_ref[c]
        pos = key_pos + c * kc
        sel = ((sc > thr) | ((sc == thr) & (pos <= jlast))) & (pos <= q_pos)
        bias = jnp.where(sel, 0.0, NEG)
        s = jnp.einsum('hkd,hdt->hkt', k_ref[0, :, keys_of(c), :], qt_ref[0],
                       preferred_element_type=F32) + bias[None]
        m_old = m_ref[...]
        m_new = jnp.maximum(m_old, jnp.max(s, axis=1, keepdims=True))
        alpha = jnp.exp2(m_old - m_new)
        p = jnp.exp2(s - m_new)
        l_ref[...] = alpha * l_ref[...] + jnp.sum(p, axis=1, keepdims=True)
        pv = jnp.einsum('hdk,hkt->hdt', vt_ref[0, :, c], p.astype(BF16), preferred_element_type=F32)
        acc_ref[...] = alpha * acc_ref[...] + pv
        m_ref[...] = m_new
        return carry

    lax.fori_loop(0, nk, att_body, 0)
    for h in range(N_HEADS):
        o_ref[0, :, h * HEAD_DIM:(h + 1) * HEAD_DIM] = (acc_ref[h] / l_ref[h]).T.astype(BF16)


def _dsa(qit, wit, kib, qt, kb, vtb, tq, topk):
    b, _, _, t = qt.shape
    nq = t // tq
    assert tq % LANES == 0 and (tq // SUBLANES) * nq < (1 << FIELD), "per-slot key counts must fit a packed field"
    hm_t = pl.BlockSpec((1, N_HEADS, HEAD_DIM, tq), lambda i, j: (i, 0, 0, j))
    return pl.pallas_call(
        functools.partial(_dsa_kernel, topk=topk),
        grid=(b, nq),
        in_specs=[hm_t,
                  pl.BlockSpec((1, N_IDX_HEADS, tq), lambda i, j: (i, 0, j)),
                  pl.BlockSpec((1, t, IDX_DIM), lambda i, j: (i, 0, 0)),
                  hm_t,
                  pl.BlockSpec((1, N_HEADS, t, HEAD_DIM), lambda i, j: (i, 0, 0, 0)),
                  pl.BlockSpec((1, N_HEADS, nq, HEAD_DIM, tq), lambda i, j: (i, 0, 0, 0, 0))],
        out_specs=pl.BlockSpec((1, tq, ATTN_W), lambda i, j: (i, j, 0)),
        out_shape=jax.ShapeDtypeStruct((b, t, ATTN_W), BF16),
        scratch_shapes=[pltpu.VMEM((nq, tq, tq), F32), pltpu.VMEM((1, tq), I32),
                        pltpu.VMEM((N_HEADS, 1, tq), F32), pltpu.VMEM((N_HEADS, 1, tq), F32),
                        pltpu.VMEM((N_HEADS, HEAD_DIM, tq), F32)],
        compiler_params=_params("parallel", "arbitrary"),
        name="dsa_prompt",
    )(qit, wit, kib, qt, kb, vtb)


def _merge_kernel(x_ref, conv_ref, attn_ref, mem_ref, wg_ref, bg_ref, wc_ref, wa_ref, wm_ref, wo_ref,
                  g1_ref, b1_ref, wrh_ref, wrl_ref, br_ref, hg_ref, *, alpha):
    x = x_ref[...]
    xb = x.astype(BF16)
    d = x.shape[1]
    cw = 2 * LANES
    blocks = []
    for c0 in range(0, d, cw):
        m = None
        for br, (src, wref) in enumerate(((conv_ref, wc_ref), (attn_ref, wa_ref), (mem_ref, wm_ref))):
            gc = slice(br * d + c0, br * d + c0 + cw)
            g = jax.nn.sigmoid(_dot(xb, wg_ref[:, gc]) + bg_ref[:, gc])
            term = g * _dot(src[...], wref[:, c0:c0 + cw])
            m = term if m is None else m + term
        blocks.append(m.astype(BF16))
    mb = jnp.concatenate(blocks, axis=1)
    h = _layernorm(alpha * x + _dot(mb, wo_ref[...]), g1_ref[...], b1_ref[...])
    hg_ref[:, :d] = h

    hh = h.astype(BF16)
    hl = (h - hh.astype(F32)).astype(BF16)
    lo = _dot(hh, wrh_ref[...]) + _dot(hl, wrh_ref[...]) + _dot(hh, wrl_ref[...]) + br_ref[...]
    lane = lax.broadcasted_iota(I32, lo.shape, 1)
    lanef = lane.astype(F32)
    is_g = (lane >= N_EXPERTS) & (lane < N_EXPERTS + N_GROUPS)
    mg = jnp.max(jnp.where(is_g, lo, -jnp.inf), axis=1, keepdims=True)
    sg = jnp.sum(jnp.where(is_g, jnp.exp(lo - mg), 0.0), axis=1, keepdims=True)
    gw = 1.0 / sg
    gsel = jnp.min(jnp.where(is_g & (lo == mg), lanef, 1e9), axis=1, keepdims=True) - N_EXPERTS
    in_g = (lane < N_EXPERTS) & ((lane >> 3).astype(F32) == gsel)
    me = jnp.max(jnp.where(in_g, lo, -jnp.inf), axis=1, keepdims=True)
    ee = jnp.where(in_g, jnp.exp(lo - me), 0.0)
    pe = jnp.where(in_g, ee / jnp.sum(ee, axis=1, keepdims=True), -1.0)
    p1 = jnp.max(pe, axis=1, keepdims=True)
    i1 = jnp.min(jnp.where(pe == p1, lanef, 1e9), axis=1, keepdims=True)
    pe2 = jnp.where(lanef == i1, -1.0, pe)
    p2 = jnp.max(pe2, axis=1, keepdims=True)
    i2 = jnp.min(jnp.where(pe2 == p2, lanef, 1e9), axis=1, keepdims=True)
    nrm = p1 + p2
    gate = gw * jnp.where(lanef == i1, p1 / nrm, jnp.where(lanef == i2, p2 / nrm, 0.0))
    hg_ref[:, d:] = jnp.where(lane == N_EXPERTS, gsel, gate)


def _merge(x, conv, attn, mem, wts, tm, alpha):
    n, d = x.shape
    tok = lambda w: pl.BlockSpec((tm, w), lambda i: (i, 0))
    const = lambda a: pl.BlockSpec(a.shape, lambda i: (0, 0))
    return pl.pallas_call(
        functools.partial(_merge_kernel, alpha=alpha),
        grid=(n // tm,),
        in_specs=[tok(d), tok(CONV_W), tok(ATTN_W), tok(MEM_W)] + [const(a) for a in wts],
        out_specs=tok(d + LANES),
        out_shape=jax.ShapeDtypeStruct((n, d + LANES), F32),
        compiler_params=_params("parallel"),
        name="merge_ln_router",
    )(x, conv, attn, mem, *wts)


def _route_kernel(rec_ref, pos_ref, tg_ref, cnt_ref, start_ref, carry_ref, *, tile_rows):
    phase = pl.program_id(0)
    i = pl.program_id(1)
    tm = rec_ref.shape[0]
    rec = rec_ref[...]
    lane = lax.broadcasted_iota(I32, rec.shape, 1)
    lanef = lane.astype(F32)
    gsel = jnp.sum(jnp.where(lane == N_EXPERTS, rec, 0.0), axis=1, keepdims=True)
    onehot = jnp.where(lanef == gsel, 1.0, 0.0)

    @pl.when((phase == 0) & (i == 0))
    def _():
        cnt_ref[...] = jnp.zeros_like(cnt_ref)

    @pl.when(phase == 0)
    def _():
        cnt_ref[...] += jnp.sum(onehot, axis=0, keepdims=True)

    @pl.when((phase == 1) & (i == 0))
    def _():
        n_tiles = jnp.ceil(cnt_ref[...] * (1.0 / tile_rows))
        n_tiles = jnp.where(lane[:1] < N_GROUPS, n_tiles, 0.0)
        r = lax.broadcasted_iota(I32, (LANES, LANES), 0)
        c = lax.broadcasted_iota(I32, (LANES, LANES), 1)
        before = jnp.where(r < c, 1.0, 0.0).astype(BF16)
        first = _dot(jnp.broadcast_to(n_tiles, (SUBLANES, LANES)).astype(BF16), before)[:1]
        start_ref[...] = first * tile_rows
        carry_ref[...] = jnp.zeros_like(carry_ref)
        tile = lane[:1].astype(F32)
        total = jnp.sum(n_tiles, axis=1, keepdims=True)
        grp = jnp.zeros((1, LANES), F32) - 1.0
        for g in range(N_GROUPS):
            first_g = jnp.sum(jnp.where(lane[:1] == g, first, 0.0), axis=1, keepdims=True)
            n_g = jnp.sum(jnp.where(lane[:1] == g, n_tiles, 0.0), axis=1, keepdims=True)
            grp = jnp.where((tile >= first_g) & (tile < first_g + n_g), float(g), grp)
        tg_ref[...] = jnp.where(tile < total, grp, -1.0).astype(I32)

    @pl.when(phase == 1)
    def _():
        r = lax.broadcasted_iota(I32, (tm, tm), 0)
        c = lax.broadcasted_iota(I32, (tm, tm), 1)
        earlier = jnp.where(c < r, 1.0, 0.0).astype(BF16)
        rank = _dot(earlier, onehot.astype(BF16))
        slot = jnp.sum(jnp.where(lanef == gsel, rank + carry_ref[...] + start_ref[...], 0.0), axis=1, keepdims=True)
        carry_ref[...] += jnp.sum(onehot, axis=0, keepdims=True)
        hi = jnp.floor(slot * (1.0 / 256.0))
        lo = slot - hi * 256.0
        ones = jnp.ones((SUBLANES, LANES), BF16)
        spread = lambda v: _dot_nt(ones, jnp.where(lane == 0, v, 0.0).astype(BF16))[:1]
        pos_ref[0] = (spread(hi) * 256.0 + spread(lo)).astype(I32)


def _route(hg, d, tm, tile_rows):
    n = hg.shape[0]
    nt = n // tm
    pos, tile_group = pl.pallas_call(
        functools.partial(_route_kernel, tile_rows=tile_rows),
        grid=(2, nt),
        in_specs=[pl.BlockSpec((tm, LANES), lambda p, i: (i, d // LANES))],
        out_specs=[pl.BlockSpec((1, 1, tm), lambda p, i: (i * p, 0, 0)),
                   pl.BlockSpec((1, LANES), lambda p, i: (0, 0))],
        out_shape=[jax.ShapeDtypeStruct((nt, 1, tm), I32), jax.ShapeDtypeStruct((1, LANES), I32)],
        scratch_shapes=[pltpu.VMEM((1, LANES), F32), pltpu.VMEM((1, LANES), F32), pltpu.VMEM((1, LANES), F32)],
        compiler_params=_params("arbitrary", "arbitrary"),
        name="moe_route",
    )(hg)
    return pos.reshape(n), tile_group[0]


def _row_copy(src_ref, dst_ref, sem, src_row, dst_row):
    return pltpu.make_async_copy(src_ref.at[pl.ds(src_row, 1)], dst_ref.at[pl.ds(dst_row, 1)], sem)


def _permute_kernel(idx_ref, src_ref, *rest, scatter):
    dst_ref, sem = rest[-2:]
    ch = idx_ref.shape[2]

    def rows(t):
        j = idx_ref[0, 0, t]
        return (t, j) if scatter else (j, t)

    def start(t, carry):
        _row_copy(src_ref, dst_ref, sem, *rows(t)).start()
        return carry

    def wait(t, carry):
        _row_copy(src_ref, dst_ref, sem, *rows(t)).wait()
        return carry

    lax.fori_loop(0, ch, start, 0, unroll=8)
    lax.fori_loop(0, ch, wait, 0, unroll=8)


def _permute_rows(src, idx, n_out, scatter, chunk):
    n_idx = idx.shape[0]
    width = src.shape[1]
    any_space = pl.BlockSpec(memory_space=pl.ANY)
    tile = pl.BlockSpec((chunk, width), lambda i: (i, 0))
    extra = (jnp.zeros((n_out, width), src.dtype),) if scatter else ()
    return pl.pallas_call(
        functools.partial(_permute_kernel, scatter=scatter),
        grid=(n_idx // chunk,),
        in_specs=[pl.BlockSpec((1, 1, chunk), lambda i: (i, 0, 0), memory_space=pltpu.SMEM),
                  tile if scatter else any_space] + [any_space] * len(extra),
        out_specs=any_space if scatter else tile,
        out_shape=jax.ShapeDtypeStruct((n_out, width), src.dtype),
        scratch_shapes=[pltpu.SemaphoreType.DMA],
        input_output_aliases={2: 0} if scatter else {},
        compiler_params=_params("arbitrary"),
        name="permute_rows_scatter" if scatter else "permute_rows_gather",
    )(idx.reshape(n_idx // chunk, 1, chunk), src, *extra)


def _moe_group_kernel(tg_ref, x_ref, wu_ref, wd_ref, g2_ref, b2_ref, o_ref, hb_ref, y_ref, *, alpha):
    i = pl.program_id(0)
    e = pl.program_id(1)
    d = o_ref.shape[1]
    de = wd_ref.shape[1]
    grp = tg_ref[i]

    @pl.when(grp < 0)
    def _():
        o_ref[...] = jnp.zeros_like(o_ref)

    @pl.when(grp >= 0)
    def _():
        @pl.when(e == 0)
        def _():
            hb_ref[...] = x_ref[:, :d].astype(BF16)
            y_ref[...] = jnp.zeros_like(y_ref)

        hu = _dot(hb_ref[...], wu_ref[0])
        a = hu[:, :de]
        act = (a * (1.0 / (1.0 + jnp.exp(-a))) * hu[:, de:]).astype(BF16)
        rec = x_ref[:, d:]
        lane = lax.broadcasted_iota(I32, rec.shape, 1)
        ge = jnp.sum(jnp.where(lane == grp * EXPERTS_PER_GROUP + e, rec, 0.0), axis=1, keepdims=True)
        y_ref[...] += ge * _dot(act, wd_ref[0])

        @pl.when(e == pl.num_programs(1) - 1)
        def _():
            o_ref[...] = _layernorm(alpha * x_ref[:, :d] + y_ref[...], g2_ref[...], b2_ref[...])


def _moe_grouped(xs, tile_group, wu, wd, g2, b2, tm, alpha):
    ns, width = xs.shape
    d = width - LANES
    _, _, du = wu.shape
    de = wd.shape[1]
    expert = lambda i, e, tg: (jnp.maximum(tg[i], 0) * EXPERTS_PER_GROUP + e, 0, 0)
    grid_spec = pltpu.PrefetchScalarGridSpec(
        num_scalar_prefetch=1,
        grid=(ns // tm, EXPERTS_PER_GROUP),
        in_specs=[pl.BlockSpec((tm, width), lambda i, e, tg: (i, 0)),
                  pl.BlockSpec((1, d, du), expert),
                  pl.BlockSpec((1, de, d), expert),
                  pl.BlockSpec((1, d), lambda i, e, tg: (0, 0)),
                  pl.BlockSpec((1, d), lambda i, e, tg: (0, 0))],
        out_specs=pl.BlockSpec((tm, d), lambda i, e, tg: (i, 0)),
        scratch_shapes=[pltpu.VMEM((tm, d), BF16), pltpu.VMEM((tm, d), F32)],
    )
    return pl.pallas_call(
        functools.partial(_moe_group_kernel, alpha=alpha),
        grid_spec=grid_spec,
        out_shape=jax.ShapeDtypeStruct((ns, d), F32),
        compiler_params=_params("arbitrary", "arbitrary"),
        name="moe_grouped_ln",
    )(tile_group, xs, wu, wd, g2, b2)


def _moe_kernel(h_ref, gate_ref, wu_ref, wd_ref, g2_ref, b2_ref, o_ref, hb_ref, y_ref, *, alpha):
    e = pl.program_id(1)
    de = wd_ref.shape[1]

    @pl.when(e == 0)
    def _():
        hb_ref[...] = h_ref[...].astype(BF16)
        y_ref[...] = jnp.zeros_like(y_ref)

    hu = _dot(hb_ref[...], wu_ref[0])
    a = hu[:, :de]
    act = (a * (1.0 / (1.0 + jnp.exp(-a))) * hu[:, de:]).astype(BF16)
    gate = gate_ref[...]
    lane = lax.broadcasted_iota(I32, gate.shape, 1)
    ge = jnp.sum(jnp.where(lane == e, gate, 0.0), axis=1, keepdims=True)
    y_ref[...] += ge * _dot(act, wd_ref[0])

    @pl.when(e == pl.num_programs(1) - 1)
    def _():
        o_ref[...] = _layernorm(alpha * h_ref[...] + y_ref[...], g2_ref[...], b2_ref[...])


def _moe(h, gate, wu, wd, g2, b2, tm, alpha):
    n, d = h.shape
    ne, _, du = wu.shape
    de = wd.shape[1]
    return pl.pallas_call(
        functools.partial(_moe_kernel, alpha=alpha),
        grid=(n // tm, ne),
        in_specs=[pl.BlockSpec((tm, d), lambda i, e: (i, 0)),
                  pl.BlockSpec((tm, ne), lambda i, e: (i, 0)),
                  pl.BlockSpec((1, d, du), lambda i, e: (e, 0, 0)),
                  pl.BlockSpec((1, de, d), lambda i, e: (e, 0, 0)),
                  pl.BlockSpec((1, d), lambda i, e: (0, 0)),
                  pl.BlockSpec((1, d), lambda i, e: (0, 0))],
        out_specs=pl.BlockSpec((tm, d), lambda i, e: (i, 0)),
        out_shape=jax.ShapeDtypeStruct((n, d), F32),
        scratch_shapes=[pltpu.VMEM((tm, d), BF16), pltpu.VMEM((tm, d), F32)],
        compiler_params=_params("parallel", "arbitrary"),
        name="moe_ln",
    )(h, gate, wu, wd, g2, b2)


def _sample_inproj_kernel(x_ref, w_ref, wt_ref, cos_ref, sin_ref, cw_ref, p0_ref, p1_ref,
                          conv_ref, z_ref, q_ref, k_ref, v_ref, qi_ref, ki_ref, wi_ref, qm_ref):
    xb = x_ref[...].astype(BF16)
    cos = cos_ref[...]
    sin = sin_ref[...]

    def proj(a, b):
        return _dot(xb, w_ref[:, a:b])

    def proj_rows(a, b):
        return _dot_nt(xb, wt_ref[a:b, :])

    ucc = proj(C_U, C_K)
    z = ucc[:, C_C:C_K] * ucc[:, C_U:C_B]
    cw = cw_ref[...]
    y = p0_ref[...] * cw[0:1, :] + p1_ref[...] * cw[1:2, :] + z * cw[2:3, :]
    conv_ref[...] = (ucc[:, C_B:C_C] * y).astype(BF16)
    z_ref[...] = z
    q_ref[...] = _rope(proj_rows(R_Q, R_QI), cos, sin) * (HEAD_DIM ** -0.5)
    k_ref[...] = _rope(proj(C_K, C_KI), cos, sin)
    v_ref[...] = proj_rows(R_V, R_KI)
    qi_ref[...] = _rope(proj_rows(R_QI, R_K), cos, sin) * (IDX_DIM ** -0.5)
    ki_ref[...] = _rope(proj(C_KI, C_QM), cos, sin)[:, :IDX_DIM]
    wi_ref[...] = proj_rows(R_WI, R_END)[:, :N_IDX_HEADS] * (N_IDX_HEADS ** -0.5)
    qm_ref[...] = proj(C_QM, C_END) * (HEAD_DIM ** -0.5)


def _sample_inproj(x, wcat, wt, cos, sin, conv_w, p0, p1):
    n, d = x.shape
    args = (x, wcat, wt, cos, sin, conv_w, p0, p1)
    widths = (CONV_W, CONV_W, ATTN_W, ATTN_W, ATTN_W, N_IDX_HEADS * IDX_DIM, IDX_DIM, N_IDX_HEADS, MEM_W)
    dtypes = (BF16,) + (F32,) * 8
    full = lambda shape: pl.BlockSpec(shape, lambda i: (0,) * len(shape))
    return pl.pallas_call(
        _sample_inproj_kernel,
        grid=(1,),
        in_specs=[full(a.shape) for a in args],
        out_specs=[full((n, w)) for w in widths],
        out_shape=[jax.ShapeDtypeStruct((n, w), dt) for w, dt in zip(widths, dtypes)],
        compiler_params=_params("arbitrary"),
        name="sample_inproj",
    )(*args)


def _head_mask(rows, width):
    r = lax.broadcasted_iota(I32, (rows, width), 0)
    c = lax.broadcasted_iota(I32, (rows, width), 1)
    return (c >> 6) == r


def _sample_mem_kernel(qm_ref, mk_ref, mv_ref, o_ref):
    hmask = _head_mask(8, MEM_W)
    qbd = jnp.where(hmask, jnp.broadcast_to(qm_ref[0], (8, MEM_W)), 0.0).astype(BF16)
    p = _softmax_rows(_dot_nt(qbd, mk_ref[0].astype(BF16)))
    o = _dot(p.astype(BF16), mv_ref[0].astype(BF16))
    o_ref[0] = jnp.sum(jnp.where(hmask, o, 0.0), axis=0, keepdims=True).astype(BF16)


def _sample_mem(qm, mk, mv):
    n, n_mem, w = mk.shape
    return pl.pallas_call(
        _sample_mem_kernel,
        grid=(n,),
        in_specs=[pl.BlockSpec((1, 1, w), lambda i: (i, 0, 0)),
                  pl.BlockSpec((1, n_mem, w), lambda i: (i, 0, 0)),
                  pl.BlockSpec((1, n_mem, w), lambda i: (i, 0, 0))],
        out_specs=pl.BlockSpec((1, 1, w), lambda i: (i, 0, 0)),
        out_shape=jax.ShapeDtypeStruct((n, 1, w), BF16),
        compiler_params=_params("parallel"),
        name="sample_mem_attn",
    )(qm, mk, mv)


def _sample_scores_kernel(pt_ref, *refs, pages):
    page_refs = refs[:pages]
    qi_ref, wi_ref, o_ref = refs[pages:]
    keys_t = jnp.concatenate([r[0] for r in page_refs], axis=1).astype(BF16)
    s = _dot(qi_ref[0].astype(BF16), keys_t)
    o_ref[0, 0] = jnp.sum(wi_ref[0] * jnp.maximum(s, 0.0), axis=0, keepdims=True)


def _sample_scores(page_table, cki_t, qi, wi, pages):
    n, n_pages = page_table.shape
    ng = n_pages // pages
    width = pages * PAGE_SIZE

    def page_spec(j):
        return pl.BlockSpec((1, IDX_DIM, PAGE_SIZE), lambda s, pt: (pt[s // ng, (s % ng) * pages + j], 0, 0))

    grid_spec = pltpu.PrefetchScalarGridSpec(
        num_scalar_prefetch=1,
        grid=(n * ng,),
        in_specs=[page_spec(j) for j in range(pages)] + [
            pl.BlockSpec((1, N_IDX_HEADS, IDX_DIM), lambda s, pt: (s // ng, 0, 0)),
            pl.BlockSpec((1, N_IDX_HEADS, 1), lambda s, pt: (s // ng, 0, 0))],
        out_specs=pl.BlockSpec((1, 1, 1, width), lambda s, pt: (s // ng, s % ng, 0, 0)),
    )
    out = pl.pallas_call(
        functools.partial(_sample_scores_kernel, pages=pages),
        grid_spec=grid_spec,
        out_shape=jax.ShapeDtypeStruct((n, ng, 1, width), F32),
        compiler_params=_params("arbitrary"),
        name="sample_idx_scores",
    )(page_table, *([cki_t] * pages), qi, wi)
    return out.reshape(n, n_pages * PAGE_SIZE)


def _sample_select_kernel(sc_ref, qi_ref, ki_ref, wi_ref, bias_ref, bnew_ref, j_ref, *, topk):
    n, past = sc_ref.shape
    qi = qi_ref[...].astype(BF16).astype(F32)
    ki = ki_ref[...].astype(BF16).astype(F32)
    s_new = jnp.maximum(jnp.sum(qi * ki, axis=2), 0.0)
    sc_new = jnp.sum(wi_ref[...] * s_new, axis=1, keepdims=True)
    idx = lax.broadcasted_iota(I32, (n, past), 1)

    def count(pred):
        one = jnp.where(pred(sc_ref[...], idx), 1.0, 0.0)
        return jnp.sum(one, axis=1, keepdims=True) + jnp.where(pred(sc_new, past), 1.0, 0.0)

    def bit_body(it, carry):
        thr, cnt_thr = carry
        cand = thr ^ lax.shift_left(jnp.int32(1), 31 - it)
        cand_f = _key_to_float(cand)
        cnt = count(lambda sc, ii: sc >= cand_f)
        ok = cnt >= topk
        return jnp.where(ok, cand, thr), jnp.where(ok, cnt, cnt_thr)

    thr_key, cnt_thr = lax.fori_loop(
        0, 32, bit_body, (jnp.full((n, 1), INT_MIN, I32), jnp.full((n, 1), float(topk), F32)))
    thr = _key_to_float(thr_key)

    j_ref[...] = jnp.full((n, 1), BIG_IDX, I32)

    @pl.when(jnp.max(jnp.where(cnt_thr > topk, 1.0, 0.0)) > 0.0)
    def _():
        need = topk - count(lambda sc, ii: sc > thr)

        def idx_body(it, jj):
            cand = jj | lax.shift_left(jnp.int32(1), 29 - it)
            below = count(lambda sc, ii: (sc == thr) & (ii < cand))
            return jnp.where(below < need, cand, jj)

        j_ref[...] = lax.fori_loop(0, 30, idx_body, jnp.zeros((n, 1), I32))

    jlast = j_ref[...]
    sc = sc_ref[...]
    bias_ref[...] = jnp.where((sc > thr) | ((sc == thr) & (idx <= jlast)), 0.0, NEG)
    bnew_ref[...] = jnp.where((sc_new > thr) | ((sc_new == thr) & (past <= jlast)), 0.0, NEG)


def _sample_select(scores, qi, ki, wi, topk):
    n, past = scores.shape
    args = (scores, qi, ki, wi)
    full = lambda shape: pl.BlockSpec(shape, lambda i: (0,) * len(shape))
    return pl.pallas_call(
        functools.partial(_sample_select_kernel, topk=topk),
        grid=(1,),
        in_specs=[full(a.shape) for a in args],
        out_specs=[full((n, past)), full((n, 1))],
        out_shape=[jax.ShapeDtypeStruct((n, past), F32), jax.ShapeDtypeStruct((n, 1), F32)],
        scratch_shapes=[pltpu.VMEM((n, 1), I32)],
        compiler_params=_params("arbitrary"),
        name="sample_topk_select",
    )(*args)


def _sample_attn_kernel(pt_ref, *refs, pages, nch):
    k_refs = refs[:pages]
    v_refs = refs[pages:2 * pages]
    q_ref, bias_ref, bnew_ref, kn_ref, vn_ref, o_ref, m_ref, l_ref, acc_ref = refs[2 * pages:]
    c = pl.program_id(0) % nch

    @pl.when(c == 0)
    def _():
        m_ref[...] = jnp.full_like(m_ref, NEG)
        l_ref[...] = jnp.zeros_like(l_ref)
        acc_ref[...] = jnp.zeros_like(acc_ref)

    rows = []
    for h in range(N_HEADS):
        qh = q_ref[0, h]
        rows.append(jnp.concatenate([jnp.sum(r[0, h] * qh, axis=0, keepdims=True) for r in k_refs], axis=1))
    s = jnp.concatenate(rows, axis=0) + bias_ref[0, 0]
    m_old = m_ref[...]
    m_new = jnp.maximum(m_old, jnp.max(s, axis=1, keepdims=True))
    alpha = jnp.exp(m_old - m_new)
    p = jnp.exp(s - m_new)
    l_ref[...] = alpha * l_ref[...] + jnp.sum(p, axis=1, keepdims=True)
    m_ref[...] = m_new
    for h in range(N_HEADS):
        acc = alpha[h:h + 1, :] * acc_ref[h]
        for j, r in enumerate(v_refs):
            acc = acc + r[0, h] * p[h:h + 1, j * PAGE_SIZE:(j + 1) * PAGE_SIZE]
        acc_ref[h] = acc

    @pl.when(c == nch - 1)
    def _():
        bn = bnew_ref[0]
        for h in range(N_HEADS):
            s_n = jnp.sum(q_ref[0, h] * kn_ref[0, h], axis=0, keepdims=True) + bn
            m0 = m_ref[h:h + 1, :]
            m1 = jnp.maximum(m0, s_n)
            a1 = jnp.exp(m0 - m1)
            p_n = jnp.exp(s_n - m1)
            l1 = a1 * l_ref[h:h + 1, :] + p_n
            out = a1 * jnp.sum(acc_ref[h], axis=1, keepdims=True) + p_n * vn_ref[0, h]
            o_ref[0, h] = out / l1


def _sample_attn(page_table, ck_t, cv_t, q, bias, bnew, kn, vn, pages):
    n, n_pages = page_table.shape
    nch = n_pages // pages
    width = pages * PAGE_SIZE

    def page_spec(j):
        return pl.BlockSpec((1, N_HEADS, HEAD_DIM, PAGE_SIZE),
                            lambda s, pt: (pt[s // nch, (s % nch) * pages + j], 0, 0, 0))

    per_seq = pl.BlockSpec((1, N_HEADS, HEAD_DIM, 1), lambda s, pt: (s // nch, 0, 0, 0))
    grid_spec = pltpu.PrefetchScalarGridSpec(
        num_scalar_prefetch=1,
        grid=(n * nch,),
        in_specs=[page_spec(j) for j in range(pages)] * 2 + [
            per_seq,
            pl.BlockSpec((1, 1, 1, width), lambda s, pt: (s // nch, s % nch, 0, 0)),
            pl.BlockSpec((1, 1, 1), lambda s, pt: (s // nch, 0, 0)),
            per_seq, per_seq],
        out_specs=per_seq,
        scratch_shapes=[pltpu.VMEM((N_HEADS, 1), F32), pltpu.VMEM((N_HEADS, 1), F32),
                        pltpu.VMEM((N_HEADS, HEAD_DIM, PAGE_SIZE), F32)],
    )
    return pl.pallas_call(
        functools.partial(_sample_attn_kernel, pages=pages, nch=nch),
        grid_spec=grid_spec,
        out_shape=jax.ShapeDtypeStruct((n, N_HEADS, HEAD_DIM, 1), F32),
        compiler_params=_params("arbitrary"),
        name="sample_attn",
    )(page_table, *([ck_t] * pages), *([cv_t] * pages), q, bias.reshape(n, nch, 1, width), bnew, kn, vn)


def _rope_tables(pos):
    inv = ROPE_THETA ** (-jnp.arange(HALF, dtype=F32) / HALF)
    ang = pos.astype(F32)[:, None] * inv[None, :]
    cos = jnp.cos(ang)
    sin = jnp.sin(ang)
    return jnp.concatenate([cos] * 4, axis=1), jnp.concatenate([-sin, sin] * 2, axis=1), cos.T, sin.T


def _largest_divisor(n, cap):
    for c in range(min(cap, n), 0, -1):
        if n % c == 0:
            return c
    return 1


def _pack_layer_weights(l, w_in, b_gate, w_br_conv, w_br_attn, w_br_mem, w_o, ln1_g, ln1_b, w_group, b_group,
                        w_expert, b_expert):
    d = w_in.shape[1]
    sizes = (CONV_W, CONV_W, CONV_W, ATTN_W, ATTN_W, ATTN_W, N_IDX_HEADS * IDX_DIM, IDX_DIM, N_IDX_HEADS, MEM_W,
             N_BRANCH * d)
    offs = [0]
    for s in sizes:
        offs.append(offs[-1] + s)
    col = lambda i: w_in[l][:, offs[i]:offs[i + 1]]
    u, c_b, c_c, q, k, v, qi, ki, wi, qm, gates = (col(i) for i in range(len(sizes)))
    zeros = lambda n: jnp.zeros((d, n), F32)
    wcat = jnp.concatenate([u, c_b, c_c, k, ki, zeros(LANES - IDX_DIM), qm], axis=1).astype(BF16)
    wt = jnp.concatenate([q, qi, k, v, ki, wi, zeros(16 - N_IDX_HEADS)], axis=1).T.astype(BF16)
    wr = jnp.concatenate([w_expert[l], w_group[l], zeros(LANES - N_EXPERTS - N_GROUPS)], axis=1)
    wrh = wr.astype(BF16)
    wrl = (wr - wrh.astype(F32)).astype(BF16)
    br = jnp.concatenate([b_expert[l], b_group[l], jnp.zeros((LANES - N_EXPERTS - N_GROUPS,), F32)])[None, :]
    merge_w = (gates.astype(BF16), b_gate[l][None, :], w_br_conv[l].astype(BF16), w_br_attn[l].astype(BF16),
               w_br_mem[l].astype(BF16), w_o[l].astype(BF16), ln1_g[l][None, :], ln1_b[l][None, :], wrh, wrl, br)
    return wcat, wt, merge_w


def kernel(x_prompt, x_sample, mem_prompt, cache_k, cache_v, cache_k_idx, state_conv, cache_mem_k, cache_mem_v,
           page_table, w_in, b_gate, conv_w, w_br_conv, w_br_attn, w_br_mem, w_o, w_mem_k, w_mem_v, ln1_g, ln1_b,
           w_group, b_group, w_expert, b_expert, w_up, w_down, ln2_g, ln2_b):
    depth = w_in.shape[0]
    b, s_len, d = x_prompt.shape
    bd, t_new, _ = x_sample.shape
    assert t_new == 1, "the sample group decodes one token per sequence"
    n_pages = page_table.shape[1]
    past = n_pages * PAGE_SIZE
    alpha = (2.0 * depth) ** 0.25

    tq = _largest_divisor(s_len, 256)
    tm_in = _largest_divisor(s_len, 512)
    assert tm_in % tq == 0
    n_tok = b * s_len
    tm_merge = _largest_divisor(n_tok, 512)
    tm_moe = _largest_divisor(n_tok, 1024)
    perm_chunk = _largest_divisor(n_tok, 512)
    assert n_tok // tm_moe + N_GROUPS <= LANES, "the tile -> group map is one lane vector"
    topk_p = min(TOPK_MAX, s_len // 4)
    topk_s = min(TOPK_MAX, (past + t_new) // 4)
    pages_idx = _largest_divisor(n_pages, 16)
    pages_att = _largest_divisor(n_pages, 16)

    cos_p, sin_p, cost_p, sint_p = _rope_tables(jnp.arange(s_len, dtype=jnp.int32))
    cos_s, sin_s, _, _ = _rope_tables(past + jnp.arange(t_new, dtype=jnp.int32))

    hp = x_prompt
    hs = x_sample.reshape(bd, d)
    outs = [[] for _ in range(10)]
    for l in range(depth):
        wcat, wt, merge_w = _pack_layer_weights(l, w_in, b_gate, w_br_conv, w_br_attn, w_br_mem, w_o, ln1_g, ln1_b,
                                                w_group, b_group, w_expert, b_expert)
        wu = w_up[l].astype(BF16)
        wd = w_down[l].astype(BF16)
        g2, b2 = ln2_g[l][None, :], ln2_b[l][None, :]

        mk, mv = _memkv(mem_prompt, w_mem_k[l].astype(BF16), w_mem_v[l].astype(BF16))
        (conv_o, mem_o, qt_b, qit_b, wit_p, kt_p, vt_p, k_b, vt_b, kit_p, ki_b, conv_new) = _inproj(
            hp, wcat, wt, cos_p, sin_p, cost_p, sint_p, conv_w[l], mk, mv, tm_in, tq)
        attn_o = _dsa(qit_b, wit_p, ki_b, qt_b, k_b, vt_b, tq, topk_p)
        hg = _merge(hp.reshape(n_tok, d), conv_o.reshape(n_tok, CONV_W), attn_o.reshape(n_tok, ATTN_W),
                    mem_o.reshape(n_tok, MEM_W), merge_w, tm_merge, alpha)
        slot, tile_group = _route(hg, d, tm_moe, tm_moe)
        n_sorted = n_tok + N_GROUPS * tm_moe
        xs = _permute_rows(hg, slot, n_sorted, True, perm_chunk)
        ys = _moe_grouped(xs, tile_group[:n_sorted // tm_moe], wu, wd, g2, b2, tm_moe, alpha)
        hp = _permute_rows(ys, slot, n_tok, False, perm_chunk).reshape(b, s_len, d)
        for lst, val in zip(outs[:6], (jnp.transpose(kt_p, (0, 3, 1, 2)), jnp.transpose(vt_p, (0, 3, 1, 2)),
                                       jnp.transpose(kit_p, (0, 2, 1)), conv_new,
                                       mk.reshape(b, -1, N_MEM_HEADS, HEAD_DIM),
                                       mv.reshape(b, -1, N_MEM_HEADS, HEAD_DIM))):
            lst.append(val)

        st = state_conv[l].astype(F32)
        (conv_s, z_s, q_s, k_s, v_s, qi_s, ki_s, wi_s, qm_s) = _sample_inproj(
            hs, wcat, wt, cos_s, sin_s, conv_w[l], st[:, 0, :], st[:, 1, :])
        mem_s = _sample_mem(qm_s.reshape(bd, 1, MEM_W), cache_mem_k[l].reshape(bd, -1, MEM_W),
                            cache_mem_v[l].reshape(bd, -1, MEM_W))
        qi3 = qi_s.reshape(bd, N_IDX_HEADS, IDX_DIM)
        wi3 = wi_s.reshape(bd, N_IDX_HEADS, 1)
        scores = _sample_scores(page_table, jnp.transpose(cache_k_idx[l], (0, 2, 1)), qi3, wi3, pages_idx)
        bias, bnew = _sample_select(scores, qi3, ki_s.reshape(bd, 1, IDX_DIM), wi_s, topk_s)
        col4 = lambda a: a.reshape(bd, N_HEADS, HEAD_DIM, 1)
        attn_s = _sample_attn(page_table, jnp.transpose(cache_k[l], (0, 2, 3, 1)),
                              jnp.transpose(cache_v[l], (0, 2, 3, 1)), col4(q_s), bias, bnew.reshape(bd, 1, 1),
                              col4(k_s), col4(v_s), pages_att)
        hgs = _merge(hs, conv_s, attn_s.reshape(bd, ATTN_W).astype(BF16), mem_s.reshape(bd, MEM_W), merge_w,
                     bd, alpha)
        hs = _moe(hgs[:, :d], hgs[:, d:d + N_EXPERTS], wu, wd, g2, b2, bd, alpha)
        for lst, val in zip(outs[6:], (k_s.reshape(bd, t_new, N_HEADS, HEAD_DIM),
                                       v_s.reshape(bd, t_new, N_HEADS, HEAD_DIM), ki_s.reshape(bd, t_new, IDX_DIM),
                                       jnp.stack([st[:, 1, :], z_s], axis=1))):
            lst.append(val)

    return (hp, hs.reshape(bd, t_new, d)) + tuple(jnp.stack(o) for o in outs)
```

```python
import functools

import jax
import jax.numpy as jnp
from jax import lax
from jax.experimental import pallas as pl
from jax.experimental.pallas import tpu as pltpu

F32 = jnp.float32
BF16 = jnp.bfloat16
I32 = jnp.int32

HEAD_DIM = 64
HALF = HEAD_DIM // 2
N_HEADS = 8
ATTN_W = N_HEADS * HEAD_DIM
N_IDX_HEADS = 8
IDX_DIM = 64
TOPK_MAX = 256
N_MEM_HEADS = 4
MEM_W = N_MEM_HEADS * HEAD_DIM
CONV_W = 256
CONV_K = 3
N_BRANCH = 3
N_GROUPS = 4
EXPERTS_PER_GROUP = 8
N_EXPERTS = N_GROUPS * EXPERTS_PER_GROUP
PAGE_SIZE = 128
ROPE_THETA = 10000.0
LN_EPS = 1e-5
NEG = -1e30
LANES = 128
SUBLANES = 8

C_U, C_B, C_C = 0, CONV_W, 2 * CONV_W
C_K = 3 * CONV_W
C_KI = C_K + ATTN_W
C_QM = C_KI + LANES
C_END = C_QM + MEM_W
R_Q = 0
R_QI = R_Q + ATTN_W
R_K = R_QI + N_IDX_HEADS * IDX_DIM
R_V = R_K + ATTN_W
R_KI = R_V + ATTN_W
R_WI = R_KI + IDX_DIM
R_END = R_WI + 16

INT_MIN = -(2 ** 31)
KEY_OF_NEG_INF = -2139095041
KEY_OF_POS_INF = 2139095040
BIG_IDX = 2 ** 30
FIELD = 10
Q_SCALE_LOG2 = HEAD_DIM ** -0.5 * 1.4426950408889634
HEADS_PER_BATCH = 8
VMEM_LIMIT = 56 * 1024 * 1024

_NT = (((1,), (1,)), ((), ()))


def _dot(a, b):
    return jnp.dot(a, b, preferred_element_type=F32)


def _dot_nt(a, b):
    return lax.dot_general(a, b, _NT, preferred_element_type=F32)


def _params(*sem):
    return pltpu.CompilerParams(dimension_semantics=sem, vmem_limit_bytes=VMEM_LIMIT)


def _rope(x, cos, sin):
    w = x.shape[1]
    lane = lax.broadcasted_iota(I32, x.shape, 1)
    swapped = jnp.where((lane & 63) < 32, pltpu.roll(x, w - 32, 1), pltpu.roll(x, 32, 1))
    reps = w // LANES
    if reps > 1:
        cos = jnp.concatenate([cos] * reps, axis=1)
        sin = jnp.concatenate([sin] * reps, axis=1)
    return x * cos + swapped * sin


def _rope_t(xt, cos_t, sin_t):
    x1, x2 = xt[:HALF, :], xt[HALF:, :]
    return x1 * cos_t - x2 * sin_t, x1 * sin_t + x2 * cos_t


def _key_to_float(key):
    key = jnp.clip(key, KEY_OF_NEG_INF, KEY_OF_POS_INF)
    return lax.bitcast_convert_type(key ^ ((key >> 31) & 0x7FFFFFFF), F32)


def _layernorm(x, g, b):
    mu = jnp.mean(x, axis=-1, keepdims=True)
    xc = x - mu
    var = jnp.mean(xc * xc, axis=-1, keepdims=True)
    return xc * lax.rsqrt(var + LN_EPS) * g + b


def _softmax_rows(s):
    m = jnp.max(s, axis=-1, keepdims=True)
    e = jnp.exp(s - m)
    return e / jnp.sum(e, axis=-1, keepdims=True)


def _memkv_kernel(mem_ref, wk_ref, wv_ref, mk_ref, mv_ref):
    m = mem_ref[0].astype(BF16)
    mk_ref[0] = _dot(m, wk_ref[...])
    mv_ref[0] = _dot(m, wv_ref[...])


def _memkv(mem, wk, wv):
    b, n_mem, d = mem.shape
    return pl.pallas_call(
        _memkv_kernel,
        grid=(b,),
        in_specs=[pl.BlockSpec((1, n_mem, d), lambda i: (i, 0, 0)),
                  pl.BlockSpec((d, MEM_W), lambda i: (0, 0)),
                  pl.BlockSpec((d, MEM_W), lambda i: (0, 0))],
        out_specs=[pl.BlockSpec((1, n_mem, MEM_W), lambda i: (i, 0, 0))] * 2,
        out_shape=[jax.ShapeDtypeStruct((b, n_mem, MEM_W), F32)] * 2,
        compiler_params=_params("parallel"),
        name="memkv",
    )(mem, wk, wv)


def _inproj_kernel(x_ref, w_ref, wt_ref, cos_ref, sin_ref, cost_ref, sint_ref, cw_ref, mk_ref, mv_ref,
                   conv_ref, mem_ref, qt_ref, qit_ref, wit_ref, kt_ref, vt_ref, kb_ref, vtb_ref, kit_ref, kib_ref,
                   cnew_ref, zprev_ref):
    j = pl.program_id(1)
    tm = x_ref.shape[1]
    kc = vtb_ref.shape[4]
    xb = x_ref[0].astype(BF16)
    cos = cos_ref[...]
    sin = sin_ref[...]
    cos_t = cost_ref[...]
    sin_t = sint_ref[...]

    def proj(a, b):
        return _dot(xb, w_ref[:, a:b])

    def proj_t(a, b):
        return _dot_nt(wt_ref[a:b, :], xb)

    @pl.when(j == 0)
    def _():
        zprev_ref[...] = jnp.zeros_like(zprev_ref)

    ucc = proj(C_U, C_K)
    z = ucc[:, C_C:C_K] * ucc[:, C_U:C_B]
    prev = zprev_ref[...]
    row = lax.broadcasted_iota(I32, z.shape, 0)
    z1 = jnp.where(row == 0, prev[7:8, :], pltpu.roll(z, 1, 0))
    z2 = jnp.where(row == 0, prev[6:7, :], jnp.where(row == 1, prev[7:8, :], pltpu.roll(z, 2, 0)))
    cw = cw_ref[...]
    y = z2 * cw[0:1, :] + z1 * cw[1:2, :] + z * cw[2:3, :]
    conv_ref[0] = (ucc[:, C_B:C_C] * y).astype(BF16)
    zprev_ref[...] = z[tm - 8:tm, :]
    cnew_ref[0] = z[tm - 2:tm, :]

    k = _rope(proj(C_K, C_KI), cos, sin)
    for h in range(N_HEADS):
        kb_ref[0, h] = k[:, h * HEAD_DIM:(h + 1) * HEAD_DIM].astype(BF16)
    kib_ref[0] = _rope(proj(C_KI, C_QM), cos, sin)[:, :IDX_DIM].astype(BF16)

    qt = proj_t(R_Q, R_QI)
    qit = proj_t(R_QI, R_K)
    kt = proj_t(R_K, R_V)
    vt = proj_t(R_V, R_KI)
    for h in range(N_HEADS):
        hs = slice(h * HEAD_DIM, (h + 1) * HEAD_DIM)
        o1, o2 = _rope_t(qt[hs, :], cos_t, sin_t)
        qt_ref[0, h, :HALF, :] = (o1 * Q_SCALE_LOG2).astype(BF16)
        qt_ref[0, h, HALF:, :] = (o2 * Q_SCALE_LOG2).astype(BF16)
        o1, o2 = _rope_t(qit[hs, :], cos_t, sin_t)
        qit_ref[0, h, :HALF, :] = (o1 * (IDX_DIM ** -0.5)).astype(BF16)
        qit_ref[0, h, HALF:, :] = (o2 * (IDX_DIM ** -0.5)).astype(BF16)
        o1, o2 = _rope_t(kt[hs, :], cos_t, sin_t)
        kt_ref[0, h, :HALF, :] = o1
        kt_ref[0, h, HALF:, :] = o2
        vt_ref[0, h] = vt[hs, :]
        for cc in range(tm // kc):
            vtb_ref[0, h, cc] = vt[hs, cc * kc:(cc + 1) * kc].astype(BF16)
    o1, o2 = _rope_t(proj_t(R_KI, R_WI), cos_t, sin_t)
    kit_ref[0, :HALF, :] = o1
    kit_ref[0, HALF:, :] = o2
    wit_ref[0] = proj_t(R_WI, R_END)[:N_IDX_HEADS, :] * (N_IDX_HEADS ** -0.5)

    qm = (proj(C_QM, C_END) * (HEAD_DIM ** -0.5)).astype(BF16)
    mk = mk_ref[0].astype(BF16)
    mv = mv_ref[0].astype(BF16)
    outs = []
    for h in range(N_MEM_HEADS):
        sl = slice(h * HEAD_DIM, (h + 1) * HEAD_DIM)
        p = _softmax_rows(_dot_nt(qm[:, sl], mk[:, sl]))
        outs.append(_dot(p.astype(BF16), mv[:, sl]))
    mem_ref[0] = jnp.concatenate(outs, axis=1).astype(BF16)


def _inproj(x, wcat, wt, cos, sin, cos_t, sin_t, conv_w, mk, mv, tm, kc):
    b, t, d = x.shape
    n_mem = mk.shape[1]
    grid = (b, t // tm)
    tok = lambda w: pl.BlockSpec((1, tm, w), lambda i, j: (i, j, 0))
    hm_t = pl.BlockSpec((1, N_HEADS, HEAD_DIM, tm), lambda i, j: (i, 0, 0, j))
    const2 = lambda r, c: pl.BlockSpec((r, c), lambda i, j: (0, 0))
    hm_t_shape = lambda dt: jax.ShapeDtypeStruct((b, N_HEADS, HEAD_DIM, t), dt)
    out_shape = [
        jax.ShapeDtypeStruct((b, t, CONV_W), BF16),
        jax.ShapeDtypeStruct((b, t, MEM_W), BF16),
        hm_t_shape(BF16), hm_t_shape(BF16),
        jax.ShapeDtypeStruct((b, N_IDX_HEADS, t), F32),
        hm_t_shape(F32), hm_t_shape(F32),
        jax.ShapeDtypeStruct((b, N_HEADS, t, HEAD_DIM), BF16),
        jax.ShapeDtypeStruct((b, N_HEADS, t // kc, HEAD_DIM, kc), BF16),
        jax.ShapeDtypeStruct((b, IDX_DIM, t), F32),
        jax.ShapeDtypeStruct((b, t, IDX_DIM), BF16),
        jax.ShapeDtypeStruct((b, CONV_K - 1, CONV_W), F32),
    ]
    out_specs = [tok(CONV_W), tok(MEM_W), hm_t, hm_t,
                 pl.BlockSpec((1, N_IDX_HEADS, tm), lambda i, j: (i, 0, j)),
                 hm_t, hm_t,
                 pl.BlockSpec((1, N_HEADS, tm, HEAD_DIM), lambda i, j: (i, 0, j, 0)),
                 pl.BlockSpec((1, N_HEADS, tm // kc, HEAD_DIM, kc), lambda i, j: (i, 0, j, 0, 0)),
                 pl.BlockSpec((1, IDX_DIM, tm), lambda i, j: (i, 0, j)),
                 tok(IDX_DIM),
                 pl.BlockSpec((1, CONV_K - 1, CONV_W), lambda i, j: (i, 0, 0))]
    return pl.pallas_call(
        _inproj_kernel,
        grid=grid,
        in_specs=[tok(d), const2(d, C_END), const2(R_END, d),
                  pl.BlockSpec((tm, LANES), lambda i, j: (j, 0)), pl.BlockSpec((tm, LANES), lambda i, j: (j, 0)),
                  pl.BlockSpec((HALF, tm), lambda i, j: (0, j)), pl.BlockSpec((HALF, tm), lambda i, j: (0, j)),
                  const2(CONV_K, CONV_W),
                  pl.BlockSpec((1, n_mem, MEM_W), lambda i, j: (i, 0, 0)),
                  pl.BlockSpec((1, n_mem, MEM_W), lambda i, j: (i, 0, 0))],
        out_specs=out_specs,
        out_shape=out_shape,
        scratch_shapes=[pltpu.VMEM((8, CONV_W), F32)],
        compiler_params=_params("parallel", "arbitrary"),
        name="inproj",
    )(x, wcat, wt, cos, sin, cos_t, sin_t, conv_w, mk, mv)


def _dsa_kernel(qit_ref, wit_ref, ki_ref, qt_ref, k_ref, vt_ref, o_ref,
                sc_ref, j_ref, m_ref, l_ref, acc_ref, *, topk):
    tq = qt_ref.shape[3]
    kc = sc_ref.shape[1]
    i = pl.program_id(1)
    nk = i + 1
    t0 = i * tq
    wt = wit_ref[0]
    key_pos = lax.broadcasted_iota(I32, (kc, tq), 0)
    q_pos = lax.broadcasted_iota(I32, (kc, tq), 1) + t0

    def keys_of(c):
        return pl.ds(pl.multiple_of(c * kc, kc), kc)

    def score_body(c, carry):
        kic = ki_ref[0, keys_of(c), :]
        acc = jnp.zeros((kc, tq), F32)
        for h in range(N_IDX_HEADS):
            acc = acc + wt[h:h + 1, :] * jnp.maximum(_dot(kic, qit_ref[0, h]), 0.0)
        sc_ref[c] = jnp.where(key_pos + c * kc <= q_pos, acc, -jnp.inf)
        return carry

    lax.fori_loop(0, nk, score_body, 0)

    def fold(v):
        return jnp.sum(v.reshape(kc // SUBLANES, SUBLANES, tq), axis=0)

    def count(pred):
        def body(c, cnt):
            return cnt + fold(jnp.where(pred(sc_ref[c], key_pos + c * kc), 1.0, 0.0))
        cnt = lax.fori_loop(0, nk, body, jnp.zeros((SUBLANES, tq), F32))
        return jnp.sum(cnt, axis=0, keepdims=True)

    def count3(c1, c2, c3):
        def body(c, acc):
            sc = sc_ref[c]
            v = jnp.where(sc >= c3, 1 + (1 << FIELD) + (1 << 2 * FIELD),
                          jnp.where(sc >= c2, 1 + (1 << FIELD), jnp.where(sc >= c1, 1, 0)))
            return acc + fold(v)
        acc = lax.fori_loop(0, nk, body, jnp.zeros((SUBLANES, tq), I32))
        mask = (1 << FIELD) - 1
        return [jnp.sum(f.astype(F32), axis=0, keepdims=True)
                for f in (acc & mask, (acc >> FIELD) & mask, acc >> 2 * FIELD)]

    def bit_body(it, carry):
        thr, cnt_thr = carry
        hi = lax.shift_left(jnp.int32(1), 31 - 2 * it)
        lo = lax.shift_left(jnp.int32(1), 30 - 2 * it)
        cands = (thr ^ lo, thr ^ hi, thr ^ (hi | lo))
        cnts = count3(*[_key_to_float(cj) for cj in cands])
        for cj, nj in zip(cands, cnts):
            ok = nj >= topk
            thr = jnp.where(ok, cj, thr)
            cnt_thr = jnp.where(ok, nj, cnt_thr)
        return thr, cnt_thr

    thr_key, cnt_thr = lax.fori_loop(
        0, 16, bit_body, (jnp.full((1, tq), INT_MIN, I32), jnp.full((1, tq), float(topk), F32)))
    thr = _key_to_float(thr_key)

    j_ref[...] = jnp.full_like(j_ref, BIG_IDX)
    tie_q = jnp.where((thr_key > KEY_OF_NEG_INF) & (cnt_thr > topk), 1.0, 0.0)

    @pl.when(jnp.max(tie_q) > 0.0)
    def _():
        need = topk - count(lambda sc, pos: sc > thr)

        def idx_body(it, jj):
            cand = jj | lax.shift_left(jnp.int32(1), 29 - it)
            below = count(lambda sc, pos: (sc == thr) & (pos < cand))
            return jnp.where(below < need, cand, jj)

        j_ref[...] = lax.fori_loop(0, 30, idx_body, jnp.zeros((1, tq), I32))

    jlast = j_ref[...]

    m_ref[...] = jnp.full_like(m_ref, NEG)
    l_ref[...] = jnp.zeros_like(l_ref)
    acc_ref[...] = jnp.zeros_like(acc_ref)

    def att_body(c, carry):
        sc = sc_ref[c]
        pos = key_pos + c * kc
        sel = ((sc > thr) | ((sc == thr) & (pos <= jlast))) & (pos <= q_pos)
        bias = jnp.where(sel, 0.0, NEG)
        for h0 in range(0, N_HEADS, HEADS_PER_BATCH):
            hs = slice(h0, h0 + HEADS_PER_BATCH)
            s = jnp.einsum('hkd,hdt->hkt', k_ref[0, hs, keys_of(c), :], qt_ref[0, hs],
                           preferred_element_type=F32) + bias[None]
            m_old = m_ref[hs]
            m_new = jnp.maximum(m_old, jnp.max(s, axis=1, keepdims=True))
            alpha = jnp.exp2(m_old - m_new)
            p = jnp.exp2(s - m_new)
            l_ref[hs] = alpha * l_ref[hs] + jnp.sum(p, axis=1, keepdims=True)
            pv = jnp.einsum('hdk,hkt->hdt', vt_ref[0, hs, c], p.astype(BF16), preferred_element_type=F32)
            acc_ref[hs] = alpha * acc_ref[hs] + pv
            m_ref[hs] = m_new
        return carry

    lax.fori_loop(0, nk, att_body, 0)
    for h in range(N_HEADS):
        o_ref[0, :, h * HEAD_DIM:(h + 1) * HEAD_DIM] = (acc_ref[h] / l_ref[h]).T.astype(BF16)


def _dsa(qit, wit, kib, qt, kb, vtb, tq, topk):
    b, _, _, t = qt.shape
    nq = t // tq
    assert tq % LANES == 0 and (tq // SUBLANES) * nq < (1 << FIELD), "per-slot key counts must fit a packed field"
    hm_t = pl.BlockSpec((1, N_HEADS, HEAD_DIM, tq), lambda i, j: (i, 0, 0, j))
    return pl.pallas_call(
        functools.partial(_dsa_kernel, topk=topk),
        grid=(b, nq),
        in_specs=[hm_t,
                  pl.BlockSpec((1, N_IDX_HEADS, tq), lambda i, j: (i, 0, j)),
                  pl.BlockSpec((1, t, IDX_DIM), lambda i, j: (i, 0, 0)),
                  hm_t,
                  pl.BlockSpec((1, N_HEADS, t, HEAD_DIM), lambda i, j: (i, 0, 0, 0)),
                  pl.BlockSpec((1, N_HEADS, nq, HEAD_DIM, tq), lambda i, j: (i, 0, 0, 0, 0))],
        out_specs=pl.BlockSpec((1, tq, ATTN_W), lambda i, j: (i, j, 0)),
        out_shape=jax.ShapeDtypeStruct((b, t, ATTN_W), BF16),
        scratch_shapes=[pltpu.VMEM((nq, tq, tq), F32), pltpu.VMEM((1, tq), I32),
                        pltpu.VMEM((N_HEADS, 1, tq), F32), pltpu.VMEM((N_HEADS, 1, tq), F32),
                        pltpu.VMEM((N_HEADS, HEAD_DIM, tq), F32)],
        compiler_params=_params("parallel", "arbitrary"),
        name="dsa_prompt",
    )(qit, wit, kib, qt, kb, vtb)


def _merge_kernel(x_ref, conv_ref, attn_ref, mem_ref, wg_ref, bg_ref, wc_ref, wa_ref, wm_ref, wo_ref,
                  g1_ref, b1_ref, wrh_ref, wrl_ref, br_ref, hg_ref, *, alpha):
    x = x_ref[...]
    xb = x.astype(BF16)
    d = x.shape[1]
    cw = 2 * LANES
    blocks = []
    for c0 in range(0, d, cw):
        m = None
        for br, (src, wref) in enumerate(((conv_ref, wc_ref), (attn_ref, wa_ref), (mem_ref, wm_ref))):
            gc = slice(br * d + c0, br * d + c0 + cw)
            g = jax.nn.sigmoid(_dot(xb, wg_ref[:, gc]) + bg_ref[:, gc])
            term = g * _dot(src[...], wref[:, c0:c0 + cw])
            m = term if m is None else m + term
        blocks.append(m.astype(BF16))
    mb = jnp.concatenate(blocks, axis=1)
    h = _layernorm(alpha * x + _dot(mb, wo_ref[...]), g1_ref[...], b1_ref[...])
    hg_ref[:, :d] = h

    hh = h.astype(BF16)
    hl = (h - hh.astype(F32)).astype(BF16)
    lo = _dot(hh, wrh_ref[...]) + _dot(hl, wrh_ref[...]) + _dot(hh, wrl_ref[...]) + br_ref[...]
    lane = lax.broadcasted_iota(I32, lo.shape, 1)
    lanef = lane.astype(F32)
    is_g = (lane >= N_EXPERTS) & (lane < N_EXPERTS + N_GROUPS)
    mg = jnp.max(jnp.where(is_g, lo, -jnp.inf), axis=1, keepdims=True)
    sg = jnp.sum(jnp.where(is_g, jnp.exp(lo - mg), 0.0), axis=1, keepdims=True)
    gw = 1.0 / sg
    gsel = jnp.min(jnp.where(is_g & (lo == mg), lanef, 1e9), axis=1, keepdims=True) - N_EXPERTS
    in_g = (lane < N_EXPERTS) & ((lane >> 3).astype(F32) == gsel)
    me = jnp.max(jnp.where(in_g, lo, -jnp.inf), axis=1, keepdims=True)
    ee = jnp.where(in_g, jnp.exp(lo - me), 0.0)
    pe = jnp.where(in_g, ee / jnp.sum(ee, axis=1, keepdims=True), -1.0)
    p1 = jnp.max(pe, axis=1, keepdims=True)
    i1 = jnp.min(jnp.where(pe == p1, lanef, 1e9), axis=1, keepdims=True)
    pe2 = jnp.where(lanef == i1, -1.0, pe)
    p2 = jnp.max(pe2, axis=1, keepdims=True)
    i2 = jnp.min(jnp.where(pe2 == p2, lanef, 1e9), axis=1, keepdims=True)
    nrm = p1 + p2
    gate = gw * jnp.where(lanef == i1, p1 / nrm, jnp.where(lanef == i2, p2 / nrm, 0.0))
    hg_ref[:, d:] = jnp.where(lane == N_EXPERTS, gsel, gate)


def _merge(x, conv, attn, mem, wts, tm, alpha):
    n, d = x.shape
    tok = lambda w: pl.BlockSpec((tm, w), lambda i: (i, 0))
    const = lambda a: pl.BlockSpec(a.shape, lambda i: (0, 0))
    return pl.pallas_call(
        functools.partial(_merge_kernel, alpha=alpha),
        grid=(n // tm,),
        in_specs=[tok(d), tok(CONV_W), tok(ATTN_W), tok(MEM_W)] + [const(a) for a in wts],
        out_specs=tok(d + LANES),
        out_shape=jax.ShapeDtypeStruct((n, d + LANES), F32),
        compiler_params=_params("parallel"),
        name="merge_ln_router",
    )(x, conv, attn, mem, *wts)


def _route_kernel(rec_ref, pos_ref, tg_ref, cnt_ref, start_ref, carry_ref, *, tile_rows):
    phase = pl.program_id(0)
    i = pl.program_id(1)
    tm = rec_ref.shape[0]
    rec = rec_ref[...]
    lane = lax.broadcasted_iota(I32, rec.shape, 1)
    lanef = lane.astype(F32)
    gsel = jnp.sum(jnp.where(lane == N_EXPERTS, rec, 0.0), axis=1, keepdims=True)
    onehot = jnp.where(lanef == gsel, 1.0, 0.0)

    @pl.when((phase == 0) & (i == 0))
    def _():
        cnt_ref[...] = jnp.zeros_like(cnt_ref)

    @pl.when(phase == 0)
    def _():
        cnt_ref[...] += jnp.sum(onehot, axis=0, keepdims=True)

    @pl.when((phase == 1) & (i == 0))
    def _():
        n_tiles = jnp.ceil(cnt_ref[...] * (1.0 / tile_rows))
        n_tiles = jnp.where(lane[:1] < N_GROUPS, n_tiles, 0.0)
        r = lax.broadcasted_iota(I32, (LANES, LANES), 0)
        c = lax.broadcasted_iota(I32, (LANES, LANES), 1)
        before = jnp.where(r < c, 1.0, 0.0).astype(BF16)
        first = _dot(jnp.broadcast_to(n_tiles, (SUBLANES, LANES)).astype(BF16), before)[:1]
        start_ref[...] = first * tile_rows
        carry_ref[...] = jnp.zeros_like(carry_ref)
        tile = lane[:1].astype(F32)
        total = jnp.sum(n_tiles, axis=1, keepdims=True)
        grp = jnp.zeros((1, LANES), F32) - 1.0
        for g in range(N_GROUPS):
            first_g = jnp.sum(jnp.where(lane[:1] == g, first, 0.0), axis=1, keepdims=True)
            n_g = jnp.sum(jnp.where(lane[:1] == g, n_tiles, 0.0), axis=1, keepdims=True)
            grp = jnp.where((tile >= first_g) & (tile < first_g + n_g), float(g), grp)
        tg_ref[...] = jnp.where(tile < total, grp, -1.0).astype(I32)

    @pl.when(phase == 1)
    def _():
        r = lax.broadcasted_iota(I32, (tm, tm), 0)
        c = lax.broadcasted_iota(I32, (tm, tm), 1)
        earlier = jnp.where(c < r, 1.0, 0.0).astype(BF16)
        rank = _dot(earlier, onehot.astype(BF16))
        slot = jnp.sum(jnp.where(lanef == gsel, rank + carry_ref[...] + start_ref[...], 0.0), axis=1, keepdims=True)
        carry_ref[...] += jnp.sum(onehot, axis=0, keepdims=True)
        hi = jnp.floor(slot * (1.0 / 256.0))
        lo = slot - hi * 256.0
        ones = jnp.ones((SUBLANES, LANES), BF16)
        spread = lambda v: _dot_nt(ones, jnp.where(lane == 0, v, 0.0).astype(BF16))[:1]
        pos_ref[0] = (spread(hi) * 256.0 + spread(lo)).astype(I32)


def _route(hg, d, tm, tile_rows):
    n = hg.shape[0]
    nt = n // tm
    pos, tile_group = pl.pallas_call(
        functools.partial(_route_kernel, tile_rows=tile_rows),
        grid=(2, nt),
        in_specs=[pl.BlockSpec((tm, LANES), lambda p, i: (i, d // LANES))],
        out_specs=[pl.BlockSpec((1, 1, tm), lambda p, i: (i * p, 0, 0)),
                   pl.BlockSpec((1, LANES), lambda p, i: (0, 0))],
        out_shape=[jax.ShapeDtypeStruct((nt, 1, tm), I32), jax.ShapeDtypeStruct((1, LANES), I32)],
        scratch_shapes=[pltpu.VMEM((1, LANES), F32), pltpu.VMEM((1, LANES), F32), pltpu.VMEM((1, LANES), F32)],
        compiler_params=_params("arbitrary", "arbitrary"),
        name="moe_route",
    )(hg)
    return pos.reshape(n), tile_group[0]


def _row_copy(src_ref, dst_ref, sem, src_row, dst_row):
    return pltpu.make_async_copy(src_ref.at[pl.ds(src_row, 1)], dst_ref.at[pl.ds(dst_row, 1)], sem)


def _permute_kernel(idx_ref, src_ref, *rest, scatter):
    dst_ref, sem = rest[-2:]
    ch = idx_ref.shape[2]

    def rows(t):
        j = idx_ref[0, 0, t]
        return (t, j) if scatter else (j, t)

    def start(t, carry):
        _row_copy(src_ref, dst_ref, sem, *rows(t)).start()
        return carry

    def wait(t, carry):
        _row_copy(src_ref, dst_ref, sem, *rows(t)).wait()
        return carry

    lax.fori_loop(0, ch, start, 0, unroll=8)
    lax.fori_loop(0, ch, wait, 0, unroll=8)


def _permute_rows(src, idx, n_out, scatter, chunk):
    n_idx = idx.shape[0]
    width = src.shape[1]
    any_space = pl.BlockSpec(memory_space=pl.ANY)
    tile = pl.BlockSpec((chunk, width), lambda i: (i, 0))
    extra = (jnp.zeros((n_out, width), src.dtype),) if scatter else ()
    return pl.pallas_call(
        functools.partial(_permute_kernel, scatter=scatter),
        grid=(n_idx // chunk,),
        in_specs=[pl.BlockSpec((1, 1, chunk), lambda i: (i, 0, 0), memory_space=pltpu.SMEM),
                  tile if scatter else any_space] + [any_space] * len(extra),
        out_specs=any_space if scatter else tile,
        out_shape=jax.ShapeDtypeStruct((n_out, width), src.dtype),
        scratch_shapes=[pltpu.SemaphoreType.DMA],
        input_output_aliases={2: 0} if scatter else {},
        compiler_params=_params("arbitrary"),
        name="permute_rows_scatter" if scatter else "permute_rows_gather",
    )(idx.reshape(n_idx // chunk, 1, chunk), src, *extra)


def _moe_group_kernel(tg_ref, x_ref, wu_ref, wd_ref, g2_ref, b2_ref, o_ref, hb_ref, y_ref, *, alpha):
    i = pl.program_id(0)
    e = pl.program_id(1)
    d = o_ref.shape[1]
    de = wd_ref.shape[1]
    grp = tg_ref[i]

    @pl.when(grp < 0)
    def _():
        o_ref[...] = jnp.zeros_like(o_ref)

    @pl.when(grp >= 0)
    def _():
        @pl.when(e == 0)
        def _():
            hb_ref[...] = x_ref[:, :d].astype(BF16)
            y_ref[...] = jnp.zeros_like(y_ref)

        hu = _dot(hb_ref[...], wu_ref[0])
        a = hu[:, :de]
        act = (a * (1.0 / (1.0 + jnp.exp(-a))) * hu[:, de:]).astype(BF16)
        rec = x_ref[:, d:]
        lane = lax.broadcasted_iota(I32, rec.shape, 1)
        ge = jnp.sum(jnp.where(lane == grp * EXPERTS_PER_GROUP + e, rec, 0.0), axis=1, keepdims=True)
        y_ref[...] += ge * _dot(act, wd_ref[0])

        @pl.when(e == pl.num_programs(1) - 1)
        def _():
            o_ref[...] = _layernorm(alpha * x_ref[:, :d] + y_ref[...], g2_ref[...], b2_ref[...])


def _moe_grouped(xs, tile_group, wu, wd, g2, b2, tm, alpha):
    ns, width = xs.shape
    d = width - LANES
    _, _, du = wu.shape
    de = wd.shape[1]
    expert = lambda i, e, tg: (jnp.maximum(tg[i], 0) * EXPERTS_PER_GROUP + e, 0, 0)
    grid_spec = pltpu.PrefetchScalarGridSpec(
        num_scalar_prefetch=1,
        grid=(ns // tm, EXPERTS_PER_GROUP),
        in_specs=[pl.BlockSpec((tm, width), lambda i, e, tg: (i, 0)),
                  pl.BlockSpec((1, d, du), expert),
                  pl.BlockSpec((1, de, d), expert),
                  pl.BlockSpec((1, d), lambda i, e, tg: (0, 0)),
                  pl.BlockSpec((1, d), lambda i, e, tg: (0, 0))],
        out_specs=pl.BlockSpec((tm, d), lambda i, e, tg: (i, 0)),
        scratch_shapes=[pltpu.VMEM((tm, d), BF16), pltpu.VMEM((tm, d), F32)],
    )
    return pl.pallas_call(
        functools.partial(_moe_group_kernel, alpha=alpha),
        grid_spec=grid_spec,
        out_shape=jax.ShapeDtypeStruct((ns, d), F32),
        compiler_params=_params("arbitrary", "arbitrary"),
        name="moe_grouped_ln",
    )(tile_group, xs, wu, wd, g2, b2)


def _moe_kernel(h_ref, gate_ref, wu_ref, wd_ref, g2_ref, b2_ref, o_ref, hb_ref, y_ref, *, alpha):
    e = pl.program_id(1)
    de = wd_ref.shape[1]

    @pl.when(e == 0)
    def _():
        hb_ref[...] = h_ref[...].astype(BF16)
        y_ref[...] = jnp.zeros_like(y_ref)

    hu = _dot(hb_ref[...], wu_ref[0])
    a = hu[:, :de]
    act = (a * (1.0 / (1.0 + jnp.exp(-a))) * hu[:, de:]).astype(BF16)
    gate = gate_ref[...]
    lane = lax.broadcasted_iota(I32, gate.shape, 1)
    ge = jnp.sum(jnp.where(lane == e, gate, 0.0), axis=1, keepdims=True)
    y_ref[...] += ge * _dot(act, wd_ref[0])

    @pl.when(e == pl.num_programs(1) - 1)
    def _():
        o_ref[...] = _layernorm(alpha * h_ref[...] + y_ref[...], g2_ref[...], b2_ref[...])


def _moe(h, gate, wu, wd, g2, b2, tm, alpha):
    n, d = h.shape
    ne, _, du = wu.shape
    de = wd.shape[1]
    return pl.pallas_call(
        functools.partial(_moe_kernel, alpha=alpha),
        grid=(n // tm, ne),
        in_specs=[pl.BlockSpec((tm, d), lambda i, e: (i, 0)),
                  pl.BlockSpec((tm, ne), lambda i, e: (i, 0)),
                  pl.BlockSpec((1, d, du), lambda i, e: (e, 0, 0)),
                  pl.BlockSpec((1, de, d), lambda i, e: (e, 0, 0)),
                  pl.BlockSpec((1, d), lambda i, e: (0, 0)),
                  pl.BlockSpec((1, d), lambda i, e: (0, 0))],
        out_specs=pl.BlockSpec((tm, d), lambda i, e: (i, 0)),
        out_shape=jax.ShapeDtypeStruct((n, d), F32),
        scratch_shapes=[pltpu.VMEM((tm, d), BF16), pltpu.VMEM((tm, d), F32)],
        compiler_params=_params("parallel", "arbitrary"),
        name="moe_ln",
    )(h, gate, wu, wd, g2, b2)


def _sample_inproj_kernel(x_ref, w_ref, wt_ref, cos_ref, sin_ref, cw_ref, p0_ref, p1_ref,
                          conv_ref, z_ref, q_ref, k_ref, v_ref, qi_ref, ki_ref, wi_ref, qm_ref):
    xb = x_ref[...].astype(BF16)
    cos = cos_ref[...]
    sin = sin_ref[...]

    def proj(a, b):
        return _dot(xb, w_ref[:, a:b])

    def proj_rows(a, b):
        return _dot_nt(xb, wt_ref[a:b, :])

    ucc = proj(C_U, C_K)
    z = ucc[:, C_C:C_K] * ucc[:, C_U:C_B]
    cw = cw_ref[...]
    y = p0_ref[...] * cw[0:1, :] + p1_ref[...] * cw[1:2, :] + z * cw[2:3, :]
    conv_ref[...] = (ucc[:, C_B:C_C] * y).astype(BF16)
    z_ref[...] = z
    q_ref[...] = _rope(proj_rows(R_Q, R_QI), cos, sin) * (HEAD_DIM ** -0.5)
    k_ref[...] = _rope(proj(C_K, C_KI), cos, sin)
    v_ref[...] = proj_rows(R_V, R_KI)
    qi_ref[...] = _rope(proj_rows(R_QI, R_K), cos, sin) * (IDX_DIM ** -0.5)
    ki_ref[...] = _rope(proj(C_KI, C_QM), cos, sin)[:, :IDX_DIM]
    wi_ref[...] = proj_rows(R_WI, R_END)[:, :N_IDX_HEADS] * (N_IDX_HEADS ** -0.5)
    qm_ref[...] = proj(C_QM, C_END) * (HEAD_DIM ** -0.5)


def _sample_inproj(x, wcat, wt, cos, sin, conv_w, p0, p1):
    n, d = x.shape
    args = (x, wcat, wt, cos, sin, conv_w, p0, p1)
    widths = (CONV_W, CONV_W, ATTN_W, ATTN_W, ATTN_W, N_IDX_HEADS * IDX_DIM, IDX_DIM, N_IDX_HEADS, MEM_W)
    dtypes = (BF16,) + (F32,) * 8
    full = lambda shape: pl.BlockSpec(shape, lambda i: (0,) * len(shape))
    return pl.pallas_call(
        _sample_inproj_kernel,
        grid=(1,),
        in_specs=[full(a.shape) for a in args],
        out_specs=[full((n, w)) for w in widths],
        out_shape=[jax.ShapeDtypeStruct((n, w), dt) for w, dt in zip(widths, dtypes)],
        compiler_params=_params("arbitrary"),
        name="sample_inproj",
    )(*args)


def _head_mask(rows, width):
    r = lax.broadcasted_iota(I32, (rows, width), 0)
    c = lax.broadcasted_iota(I32, (rows, width), 1)
    return (c >> 6) == r


def _sample_mem_kernel(qm_ref, mk_ref, mv_ref, o_ref):
    hmask = _head_mask(8, MEM_W)
    qbd = jnp.where(hmask, jnp.broadcast_to(qm_ref[0], (8, MEM_W)), 0.0).astype(BF16)
    p = _softmax_rows(_dot_nt(qbd, mk_ref[0].astype(BF16)))
    o = _dot(p.astype(BF16), mv_ref[0].astype(BF16))
    o_ref[0] = jnp.sum(jnp.where(hmask, o, 0.0), axis=0, keepdims=True).astype(BF16)


def _sample_mem(qm, mk, mv):
    n, n_mem, w = mk.shape
    return pl.pallas_call(
        _sample_mem_kernel,
        grid=(n,),
        in_specs=[pl.BlockSpec((1, 1, w), lambda i: (i, 0, 0)),
                  pl.BlockSpec((1, n_mem, w), lambda i: (i, 0, 0)),
                  pl.BlockSpec((1, n_mem, w), lambda i: (i, 0, 0))],
        out_specs=pl.BlockSpec((1, 1, w), lambda i: (i, 0, 0)),
        out_shape=jax.ShapeDtypeStruct((n, 1, w), BF16),
        compiler_params=_params("parallel"),
        name="sample_mem_attn",
    )(qm, mk, mv)


def _sample_scores_kernel(pt_ref, *refs, pages):
    page_refs = refs[:pages]
    qi_ref, wi_ref, o_ref = refs[pages:]
    keys_t = jnp.concatenate([r[0] for r in page_refs], axis=1).astype(BF16)
    s = _dot(qi_ref[0].astype(BF16), keys_t)
    o_ref[0, 0] = jnp.sum(wi_ref[0] * jnp.maximum(s, 0.0), axis=0, keepdims=True)


def _sample_scores(page_table, cki_t, qi, wi, pages):
    n, n_pages = page_table.shape
    ng = n_pages // pages
    width = pages * PAGE_SIZE

    def page_spec(j):
        return pl.BlockSpec((1, IDX_DIM, PAGE_SIZE), lambda s, pt: (pt[s // ng, (s % ng) * pages + j], 0, 0))

    grid_spec = pltpu.PrefetchScalarGridSpec(
        num_scalar_prefetch=1,
        grid=(n * ng,),
        in_specs=[page_spec(j) for j in range(pages)] + [
            pl.BlockSpec((1, N_IDX_HEADS, IDX_DIM), lambda s, pt: (s // ng, 0, 0)),
            pl.BlockSpec((1, N_IDX_HEADS, 1), lambda s, pt: (s // ng, 0, 0))],
        out_specs=pl.BlockSpec((1, 1, 1, width), lambda s, pt: (s // ng, s % ng, 0, 0)),
    )
    out = pl.pallas_call(
        functools.partial(_sample_scores_kernel, pages=pages),
        grid_spec=grid_spec,
        out_shape=jax.ShapeDtypeStruct((n, ng, 1, width), F32),
        compiler_params=_params("arbitrary"),
        name="sample_idx_scores",
    )(page_table, *([cki_t] * pages), qi, wi)
    return out.reshape(n, n_pages * PAGE_SIZE)


def _sample_select_kernel(sc_ref, qi_ref, ki_ref, wi_ref, bias_ref, bnew_ref, j_ref, *, topk):
    n, past = sc_ref.shape
    qi = qi_ref[...].astype(BF16).astype(F32)
    ki = ki_ref[...].astype(BF16).astype(F32)
    s_new = jnp.maximum(jnp.sum(qi * ki, axis=2), 0.0)
    sc_new = jnp.sum(wi_ref[...] * s_new, axis=1, keepdims=True)
    idx = lax.broadcasted_iota(I32, (n, past), 1)

    def count(pred):
        one = jnp.where(pred(sc_ref[...], idx), 1.0, 0.0)
        return jnp.sum(one, axis=1, keepdims=True) + jnp.where(pred(sc_new, past), 1.0, 0.0)

    def bit_body(it, carry):
        thr, cnt_thr = carry
        cand = thr ^ lax.shift_left(jnp.int32(1), 31 - it)
        cand_f = _key_to_float(cand)
        cnt = count(lambda sc, ii: sc >= cand_f)
        ok = cnt >= topk
        return jnp.where(ok, cand, thr), jnp.where(ok, cnt, cnt_thr)

    thr_key, cnt_thr = lax.fori_loop(
        0, 32, bit_body, (jnp.full((n, 1), INT_MIN, I32), jnp.full((n, 1), float(topk), F32)))
    thr = _key_to_float(thr_key)

    j_ref[...] = jnp.full((n, 1), BIG_IDX, I32)

    @pl.when(jnp.max(jnp.where(cnt_thr > topk, 1.0, 0.0)) > 0.0)
    def _():
        need = topk - count(lambda sc, ii: sc > thr)

        def idx_body(it, jj):
            cand = jj | lax.shift_left(jnp.int32(1), 29 - it)
            below = count(lambda sc, ii: (sc == thr) & (ii < cand))
            return jnp.where(below < need, cand, jj)

        j_ref[...] = lax.fori_loop(0, 30, idx_body, jnp.zeros((n, 1), I32))

    jlast = j_ref[...]
    sc = sc_ref[...]
    bias_ref[...] = jnp.where((sc > thr) | ((sc == thr) & (idx <= jlast)), 0.0, NEG)
    bnew_ref[...] = jnp.where((sc_new > thr) | ((sc_new == thr) & (past <= jlast)), 0.0, NEG)


def _sample_select(scores, qi, ki, wi, topk):
    n, past = scores.shape
    args = (scores, qi, ki, wi)
    full = lambda shape: pl.BlockSpec(shape, lambda i: (0,) * len(shape))
    return pl.pallas_call(
        functools.partial(_sample_select_kernel, topk=topk),
        grid=(1,),
        in_specs=[full(a.shape) for a in args],
        out_specs=[full((n, past)), full((n, 1))],
        out_shape=[jax.ShapeDtypeStruct((n, past), F32), jax.ShapeDtypeStruct((n, 1), F32)],
        scratch_shapes=[pltpu.VMEM((n, 1), I32)],
        compiler_params=_params("arbitrary"),
        name="sample_topk_select",
    )(*args)


def _sample_attn_kernel(pt_ref, *refs, pages, nch):
    k_refs = refs[:pages]
    v_refs = refs[pages:2 * pages]
    q_ref, bias_ref, bnew_ref, kn_ref, vn_ref, o_ref, m_ref, l_ref, acc_ref = refs[2 * pages:]
    c = pl.program_id(0) % nch

    @pl.when(c == 0)
    def _():
        m_ref[...] = jnp.full_like(m_ref, NEG)
        l_ref[...] = jnp.zeros_like(l_ref)
        acc_ref[...] = jnp.zeros_like(acc_ref)

    rows = []
    for h in range(N_HEADS):
        qh = q_ref[0, h]
        rows.append(jnp.concatenate([jnp.sum(r[0, h] * qh, axis=0, keepdims=True) for r in k_refs], axis=1))
    s = jnp.concatenate(rows, axis=0) + bias_ref[0, 0]
    m_old = m_ref[...]
    m_new = jnp.maximum(m_old, jnp.max(s, axis=1, keepdims=True))
    alpha = jnp.exp(m_old - m_new)
    p = jnp.exp(s - m_new)
    l_ref[...] = alpha * l_ref[...] + jnp.sum(p, axis=1, keepdims=True)
    m_ref[...] = m_new
    for h in range(N_HEADS):
        acc = alpha[h:h + 1, :] * acc_ref[h]
        for j, r in enumerate(v_refs):
            acc = acc + r[0, h] * p[h:h + 1, j * PAGE_SIZE:(j + 1) * PAGE_SIZE]
        acc_ref[h] = acc

    @pl.when(c == nch - 1)
    def _():
        bn = bnew_ref[0]
        for h in range(N_HEADS):
            s_n = jnp.sum(q_ref[0, h] * kn_ref[0, h], axis=0, keepdims=True) + bn
            m0 = m_ref[h:h + 1, :]
            m1 = jnp.maximum(m0, s_n)
            a1 = jnp.exp(m0 - m1)
            p_n = jnp.exp(s_n - m1)
            l1 = a1 * l_ref[h:h + 1, :] + p_n
            out = a1 * jnp.sum(acc_ref[h], axis=1, keepdims=True) + p_n * vn_ref[0, h]
            o_ref[0, h] = out / l1


def _sample_attn(page_table, ck_t, cv_t, q, bias, bnew, kn, vn, pages):
    n, n_pages = page_table.shape
    nch = n_pages // pages
    width = pages * PAGE_SIZE

    def page_spec(j):
        return pl.BlockSpec((1, N_HEADS, HEAD_DIM, PAGE_SIZE),
                            lambda s, pt: (pt[s // nch, (s % nch) * pages + j], 0, 0, 0))

    per_seq = pl.BlockSpec((1, N_HEADS, HEAD_DIM, 1), lambda s, pt: (s // nch, 0, 0, 0))
    grid_spec = pltpu.PrefetchScalarGridSpec(
        num_scalar_prefetch=1,
        grid=(n * nch,),
        in_specs=[page_spec(j) for j in range(pages)] * 2 + [
            per_seq,
            pl.BlockSpec((1, 1, 1, width), lambda s, pt: (s // nch, s % nch, 0, 0)),
            pl.BlockSpec((1, 1, 1), lambda s, pt: (s // nch, 0, 0)),
            per_seq, per_seq],
        out_specs=per_seq,
        scratch_shapes=[pltpu.VMEM((N_HEADS, 1), F32), pltpu.VMEM((N_HEADS, 1), F32),
                        pltpu.VMEM((N_HEADS, HEAD_DIM, PAGE_SIZE), F32)],
    )
    return pl.pallas_call(
        functools.partial(_sample_attn_kernel, pages=pages, nch=nch),
        grid_spec=grid_spec,
        out_shape=jax.ShapeDtypeStruct((n, N_HEADS, HEAD_DIM, 1), F32),
        compiler_params=_params("arbitrary"),
        name="sample_attn",
    )(page_table, *([ck_t] * pages), *([cv_t] * pages), q, bias.reshape(n, nch, 1, width), bnew, kn, vn)


def _rope_tables(pos):
    inv = ROPE_THETA ** (-jnp.arange(HALF, dtype=F32) / HALF)
    ang = pos.astype(F32)[:, None] * inv[None, :]
    cos = jnp.cos(ang)
    sin = jnp.sin(ang)
    return jnp.concatenate([cos] * 4, axis=1), jnp.concatenate([-sin, sin] * 2, axis=1), cos.T, sin.T


def _largest_divisor(n, cap):
    for c in range(min(cap, n), 0, -1):
        if n % c == 0:
            return c
    return 1


def _pack_layer_weights(l, w_in, b_gate, w_br_conv, w_br_attn, w_br_mem, w_o, ln1_g, ln1_b, w_group, b_group,
                        w_expert, b_expert):
    d = w_in.shape[1]
    sizes = (CONV_W, CONV_W, CONV_W, ATTN_W, ATTN_W, ATTN_W, N_IDX_HEADS * IDX_DIM, IDX_DIM, N_IDX_HEADS, MEM_W,
             N_BRANCH * d)
    offs = [0]
    for s in sizes:
        offs.append(offs[-1] + s)
    col = lambda i: w_in[l][:, offs[i]:offs[i + 1]]
    u, c_b, c_c, q, k, v, qi, ki, wi, qm, gates = (col(i) for i in range(len(sizes)))
    zeros = lambda n: jnp.zeros((d, n), F32)
    wcat = jnp.concatenate([u, c_b, c_c, k, ki, zeros(LANES - IDX_DIM), qm], axis=1).astype(BF16)
    wt = jnp.concatenate([q, qi, k, v, ki, wi, zeros(16 - N_IDX_HEADS)], axis=1).T.astype(BF16)
    wr = jnp.concatenate([w_expert[l], w_group[l], zeros(LANES - N_EXPERTS - N_GROUPS)], axis=1)
    wrh = wr.astype(BF16)
    wrl = (wr - wrh.astype(F32)).astype(BF16)
    br = jnp.concatenate([b_expert[l], b_group[l], jnp.zeros((LANES - N_EXPERTS - N_GROUPS,), F32)])[None, :]
    merge_w = (gates.astype(BF16), b_gate[l][None, :], w_br_conv[l].astype(BF16), w_br_attn[l].astype(BF16),
               w_br_mem[l].astype(BF16), w_o[l].astype(BF16), ln1_g[l][None, :], ln1_b[l][None, :], wrh, wrl, br)
    return wcat, wt, merge_w


def kernel(x_prompt, x_sample, mem_prompt, cache_k, cache_v, cache_k_idx, state_conv, cache_mem_k, cache_mem_v,
           page_table, w_in, b_gate, conv_w, w_br_conv, w_br_attn, w_br_mem, w_o, w_mem_k, w_mem_v, ln1_g, ln1_b,
           w_group, b_group, w_expert, b_expert, w_up, w_down, ln2_g, ln2_b):
    depth = w_in.shape[0]
    b, s_len, d = x_prompt.shape
    bd, t_new, _ = x_sample.shape
    assert t_new == 1, "the sample group decodes one token per sequence"
    n_pages = page_table.shape[1]
    past = n_pages * PAGE_SIZE
    alpha = (2.0 * depth) ** 0.25

    tq = _largest_divisor(s_len, 256)
    tm_in = _largest_divisor(s_len, 512)
    assert tm_in % tq == 0
    n_tok = b * s_len
    tm_merge = _largest_divisor(n_tok, 512)
    tm_moe = _largest_divisor(n_tok, 1024)
    perm_chunk = _largest_divisor(n_tok, 1024)
    assert n_tok // tm_moe + N_GROUPS <= LANES, "the tile -> group map is one lane vector"
    topk_p = min(TOPK_MAX, s_len // 4)
    topk_s = min(TOPK_MAX, (past + t_new) // 4)
    pages_idx = _largest_divisor(n_pages, 32)
    pages_att = _largest_divisor(n_pages, 16)

    cos_p, sin_p, cost_p, sint_p = _rope_tables(jnp.arange(s_len, dtype=jnp.int32))
    cos_s, sin_s, _, _ = _rope_tables(past + jnp.arange(t_new, dtype=jnp.int32))

    hp = x_prompt
    hs = x_sample.reshape(bd, d)
    outs = [[] for _ in range(10)]
    for l in range(depth):
        wcat, wt, merge_w = _pack_layer_weights(l, w_in, b_gate, w_br_conv, w_br_attn, w_br_mem, w_o, ln1_g, ln1_b,
                                                w_group, b_group, w_expert, b_expert)
        wu = w_up[l].astype(BF16)
        wd = w_down[l].astype(BF16)
        g2, b2 = ln2_g[l][None, :], ln2_b[l][None, :]

        mk, mv = _memkv(mem_prompt, w_mem_k[l].astype(BF16), w_mem_v[l].astype(BF16))
        (conv_o, mem_o, qt_b, qit_b, wit_p, kt_p, vt_p, k_b, vt_b, kit_p, ki_b, conv_new) = _inproj(
            hp, wcat, wt, cos_p, sin_p, cost_p, sint_p, conv_w[l], mk, mv, tm_in, tq)
        attn_o = _dsa(qit_b, wit_p, ki_b, qt_b, k_b, vt_b, tq, topk_p)
        hg = _merge(hp.reshape(n_tok, d), conv_o.reshape(n_tok, CONV_W), attn_o.reshape(n_tok, ATTN_W),
                    mem_o.reshape(n_tok, MEM_W), merge_w, tm_merge, alpha)
        slot, tile_group = _route(hg, d, tm_moe, tm_moe)
        n_sorted = n_tok + N_GROUPS * tm_moe
        xs = _permute_rows(hg, slot, n_sorted, True, perm_chunk)
        ys = _moe_grouped(xs, tile_group[:n_sorted // tm_moe], wu, wd, g2, b2, tm_moe, alpha)
        hp = _permute_rows(ys, slot, n_tok, False, perm_chunk).reshape(b, s_len, d)
        for lst, val in zip(outs[:6], (jnp.transpose(kt_p, (0, 3, 1, 2)), jnp.transpose(vt_p, (0, 3, 1, 2)),
                                       jnp.transpose(kit_p, (0, 2, 1)), conv_new,
                                       mk.reshape(b, -1, N_MEM_HEADS, HEAD_DIM),
                                       mv.reshape(b, -1, N_MEM_HEADS, HEAD_DIM))):
            lst.append(val)

        st = state_conv[l].astype(F32)
        (conv_s, z_s, q_s, k_s, v_s, qi_s, ki_s, wi_s, qm_s) = _sample_inproj(
            hs, wcat, wt, cos_s, sin_s, conv_w[l], st[:, 0, :], st[:, 1, :])
        mem_s = _sample_mem(qm_s.reshape(bd, 1, MEM_W), cache_mem_k[l].reshape(bd, -1, MEM_W),
                            cache_mem_v[l].reshape(bd, -1, MEM_W))
        qi3 = qi_s.reshape(bd, N_IDX_HEADS, IDX_DIM)
        wi3 = wi_s.reshape(bd, N_IDX_HEADS, 1)
        scores = _sample_scores(page_table, jnp.transpose(cache_k_idx[l], (0, 2, 1)), qi3, wi3, pages_idx)
        bias, bnew = _sample_select(scores, qi3, ki_s.reshape(bd, 1, IDX_DIM), wi_s, topk_s)
        col4 = lambda a: a.reshape(bd, N_HEADS, HEAD_DIM, 1)
        attn_s = _sample_attn(page_table, jnp.transpose(cache_k[l], (0, 2, 3, 1)),
                              jnp.transpose(cache_v[l], (0, 2, 3, 1)), col4(q_s), bias, bnew.reshape(bd, 1, 1),
                              col4(k_s), col4(v_s), pages_att)
        hgs = _merge(hs, conv_s, attn_s.reshape(bd, ATTN_W).astype(BF16), mem_s.reshape(bd, MEM_W), merge_w,
                     bd, alpha)
        hs = _moe(hgs[:, :d], hgs[:, d:d + N_EXPERTS], wu, wd, g2, b2, bd, alpha)
        for lst, val in zip(outs[6:], (k_s.reshape(bd, t_new, N_HEADS, HEAD_DIM),
                                       v_s.reshape(bd, t_new, N_HEADS, HEAD_DIM), ki_s.reshape(bd, t_new, IDX_DIM),
                                       jnp.stack([st[:, 1, :], z_s], axis=1))):
            lst.append(val)

    return (hp, hs.reshape(bd, t_new, d)) + tuple(jnp.stack(o) for o in outs)
```

```python
import functools

import jax
import jax.numpy as jnp
from jax import lax
from jax.experimental import pallas as pl
from jax.experimental.pallas import tpu as pltpu

F32 = jnp.float32
BF16 = jnp.bfloat16
I32 = jnp.int32

HEAD_DIM = 64
HALF = HEAD_DIM // 2
N_HEADS = 8
ATTN_W = N_HEADS * HEAD_DIM
N_IDX_HEADS = 8
IDX_DIM = 64
TOPK_MAX = 256
N_MEM_HEADS = 4
MEM_W = N_MEM_HEADS * HEAD_DIM
CONV_W = 256
CONV_K = 3
N_BRANCH = 3
N_GROUPS = 4
EXPERTS_PER_GROUP = 8
N_EXPERTS = N_GROUPS * EXPERTS_PER_GROUP
PAGE_SIZE = 128
ROPE_THETA = 10000.0
LN_EPS = 1e-5
NEG = -1e30
LANES = 128
SUBLANES = 8

C_U, C_B, C_C = 0, CONV_W, 2 * CONV_W
C_K = 3 * CONV_W
C_KI = C_K + ATTN_W
C_QM = C_KI + LANES
C_END = C_QM + MEM_W
R_Q = 0
R_QI = R_Q + ATTN_W
R_K = R_QI + N_IDX_HEADS * IDX_DIM
R_V = R_K + ATTN_W
R_KI = R_V + ATTN_W
R_WI = R_KI + IDX_DIM
R_END = R_WI + 16

INT_MIN = -(2 ** 31)
KEY_OF_NEG_INF = -2139095041
KEY_OF_POS_INF = 2139095040
BIG_IDX = 2 ** 30
FIELD = 10
Q_SCALE_LOG2 = HEAD_DIM ** -0.5 * 1.4426950408889634
HEADS_PER_BATCH = 8
VMEM_LIMIT = 56 * 1024 * 1024

_NT = (((1,), (1,)), ((), ()))


def _dot(a, b):
    return jnp.dot(a, b, preferred_element_type=F32)


def _dot_nt(a, b):
    return lax.dot_general(a, b, _NT, preferred_element_type=F32)


def _params(*sem):
    return pltpu.CompilerParams(dimension_semantics=sem, vmem_limit_bytes=VMEM_LIMIT)


def _rope(x, cos, sin):
    w = x.shape[1]
    lane = lax.broadcasted_iota(I32, x.shape, 1)
    swapped = jnp.where((lane & 63) < 32, pltpu.roll(x, w - 32, 1), pltpu.roll(x, 32, 1))
    reps = w // LANES
    if reps > 1:
        cos = jnp.concatenate([cos] * reps, axis=1)
        sin = jnp.concatenate([sin] * reps, axis=1)
    return x * cos + swapped * sin


def _rope_t(xt, cos_t, sin_t):
    x1, x2 = xt[:HALF, :], xt[HALF:, :]
    return x1 * cos_t - x2 * sin_t, x1 * sin_t + x2 * cos_t


def _key_to_float(key):
    key = jnp.clip(key, KEY_OF_NEG_INF, KEY_OF_POS_INF)
    return lax.bitcast_convert_type(key ^ ((key >> 31) & 0x7FFFFFFF), F32)


def _layernorm(x, g, b):
    mu = jnp.mean(x, axis=-1, keepdims=True)
    xc = x - mu
    var = jnp.mean(xc * xc, axis=-1, keepdims=True)
    return xc * lax.rsqrt(var + LN_EPS) * g + b


def _softmax_rows(s):
    m = jnp.max(s, axis=-1, keepdims=True)
    e = jnp.exp(s - m)
    return e / jnp.sum(e, axis=-1, keepdims=True)


def _memkv_kernel(mem_ref, wk_ref, wv_ref, mk_ref, mv_ref):
    m = mem_ref[0].astype(BF16)
    mk_ref[0] = _dot(m, wk_ref[...])
    mv_ref[0] = _dot(m, wv_ref[...])


def _memkv(mem, wk, wv):
    b, n_mem, d = mem.shape
    return pl.pallas_call(
        _memkv_kernel,
        grid=(b,),
        in_specs=[pl.BlockSpec((1, n_mem, d), lambda i: (i, 0, 0)),
                  pl.BlockSpec((d, MEM_W), lambda i: (0, 0)),
                  pl.BlockSpec((d, MEM_W), lambda i: (0, 0))],
        out_specs=[pl.BlockSpec((1, n_mem, MEM_W), lambda i: (i, 0, 0))] * 2,
        out_shape=[jax.ShapeDtypeStruct((b, n_mem, MEM_W), F32)] * 2,
        compiler_params=_params("parallel"),
        name="memkv",
    )(mem, wk, wv)


def _inproj_kernel(x_ref, w_ref, wt_ref, cos_ref, sin_ref, cost_ref, sint_ref, cw_ref, mk_ref, mv_ref,
                   conv_ref, mem_ref, qt_ref, qit_ref, wit_ref, kt_ref, vt_ref, kb_ref, vtb_ref, kit_ref, kib_ref,
                   cnew_ref, zprev_ref):
    j = pl.program_id(1)
    tm = x_ref.shape[1]
    kc = vtb_ref.shape[4]
    xb = x_ref[0].astype(BF16)
    cos = cos_ref[...]
    sin = sin_ref[...]
    cos_t = cost_ref[...]
    sin_t = sint_ref[...]

    def proj(a, b):
        return _dot(xb, w_ref[:, a:b])

    def proj_t(a, b):
        return _dot_nt(wt_ref[a:b, :], xb)

    @pl.when(j == 0)
    def _():
        zprev_ref[...] = jnp.zeros_like(zprev_ref)

    ucc = proj(C_U, C_K)
    z = ucc[:, C_C:C_K] * ucc[:, C_U:C_B]
    prev = zprev_ref[...]
    row = lax.broadcasted_iota(I32, z.shape, 0)
    z1 = jnp.where(row == 0, prev[7:8, :], pltpu.roll(z, 1, 0))
    z2 = jnp.where(row == 0, prev[6:7, :], jnp.where(row == 1, prev[7:8, :], pltpu.roll(z, 2, 0)))
    cw = cw_ref[...]
    y = z2 * cw[0:1, :] + z1 * cw[1:2, :] + z * cw[2:3, :]
    conv_ref[0] = (ucc[:, C_B:C_C] * y).astype(BF16)
    zprev_ref[...] = z[tm - 8:tm, :]
    cnew_ref[0] = z[tm - 2:tm, :]

    k = _rope(proj(C_K, C_KI), cos, sin)
    for h in range(N_HEADS):
        kb_ref[0, h] = k[:, h * HEAD_DIM:(h + 1) * HEAD_DIM].astype(BF16)
    kib_ref[0] = _rope(proj(C_KI, C_QM), cos, sin)[:, :IDX_DIM].astype(BF16)

    qt = proj_t(R_Q, R_QI)
    qit = proj_t(R_QI, R_K)
    kt = proj_t(R_K, R_V)
    vt = proj_t(R_V, R_KI)
    for h in range(N_HEADS):
        hs = slice(h * HEAD_DIM, (h + 1) * HEAD_DIM)
        o1, o2 = _rope_t(qt[hs, :], cos_t, sin_t)
        qt_ref[0, h, :HALF, :] = (o1 * Q_SCALE_LOG2).astype(BF16)
        qt_ref[0, h, HALF:, :] = (o2 * Q_SCALE_LOG2).astype(BF16)
        o1, o2 = _rope_t(qit[hs, :], cos_t, sin_t)
        qit_ref[0, h, :HALF, :] = (o1 * (IDX_DIM ** -0.5)).astype(BF16)
        qit_ref[0, h, HALF:, :] = (o2 * (IDX_DIM ** -0.5)).astype(BF16)
        o1, o2 = _rope_t(kt[hs, :], cos_t, sin_t)
        kt_ref[0, h, :HALF, :] = o1
        kt_ref[0, h, HALF:, :] = o2
        vt_ref[0, h] = vt[hs, :]
        for cc in range(tm // kc):
            vtb_ref[0, h, cc] = vt[hs, cc * kc:(cc + 1) * kc].astype(BF16)
    o1, o2 = _rope_t(proj_t(R_KI, R_WI), cos_t, sin_t)
    kit_ref[0, :HALF, :] = o1
    kit_ref[0, HALF:, :] = o2
    wit_ref[0] = proj_t(R_WI, R_END)[:N_IDX_HEADS, :] * (N_IDX_HEADS ** -0.5)

    qm = (proj(C_QM, C_END) * (HEAD_DIM ** -0.5)).astype(BF16)
    mk = mk_ref[0].astype(BF16)
    mv = mv_ref[0].astype(BF16)
    outs = []
    for h in range(N_MEM_HEADS):
        sl = slice(h * HEAD_DIM, (h + 1) * HEAD_DIM)
        p = _softmax_rows(_dot_nt(qm[:, sl], mk[:, sl]))
        outs.append(_dot(p.astype(BF16), mv[:, sl]))
    mem_ref[0] = jnp.concatenate(outs, axis=1).astype(BF16)


def _inproj(x, wcat, wt, cos, sin, cos_t, sin_t, conv_w, mk, mv, tm, kc):
    b, t, d = x.shape
    n_mem = mk.shape[1]
    grid = (b, t // tm)
    tok = lambda w: pl.BlockSpec((1, tm, w), lambda i, j: (i, j, 0))
    hm_t = pl.BlockSpec((1, N_HEADS, HEAD_DIM, tm), lambda i, j: (i, 0, 0, j))
    const2 = lambda r, c: pl.BlockSpec((r, c), lambda i, j: (0, 0))
    hm_t_shape = lambda dt: jax.ShapeDtypeStruct((b, N_HEADS, HEAD_DIM, t), dt)
    out_shape = [
        jax.ShapeDtypeStruct((b, t, CONV_W), BF16),
        jax.ShapeDtypeStruct((b, t, MEM_W), BF16),
        hm_t_shape(BF16), hm_t_shape(BF16),
        jax.ShapeDtypeStruct((b, N_IDX_HEADS, t), F32),
        hm_t_shape(F32), hm_t_shape(F32),
        jax.ShapeDtypeStruct((b, N_HEADS, t, HEAD_DIM), BF16),
        jax.ShapeDtypeStruct((b, N_HEADS, t // kc, HEAD_DIM, kc), BF16),
        jax.ShapeDtypeStruct((b, IDX_DIM, t), F32),
        jax.ShapeDtypeStruct((b, t, IDX_DIM), BF16),
        jax.ShapeDtypeStruct((b, CONV_K - 1, CONV_W), F32),
    ]
    out_specs = [tok(CONV_W), tok(MEM_W), hm_t, hm_t,
                 pl.BlockSpec((1, N_IDX_HEADS, tm), lambda i, j: (i, 0, j)),
                 hm_t, hm_t,
                 pl.BlockSpec((1, N_HEADS, tm, HEAD_DIM), lambda i, j: (i, 0, j, 0)),
                 pl.BlockSpec((1, N_HEADS, tm // kc, HEAD_DIM, kc), lambda i, j: (i, 0, j, 0, 0)),
                 pl.BlockSpec((1, IDX_DIM, tm), lambda i, j: (i, 0, j)),
                 tok(IDX_DIM),
                 pl.BlockSpec((1, CONV_K - 1, CONV_W), lambda i, j: (i, 0, 0))]
    return pl.pallas_call(
        _inproj_kernel,
        grid=grid,
        in_specs=[tok(d), const2(d, C_END), const2(R_END, d),
                  pl.BlockSpec((tm, LANES), lambda i, j: (j, 0)), pl.BlockSpec((tm, LANES), lambda i, j: (j, 0)),
                  pl.BlockSpec((HALF, tm), lambda i, j: (0, j)), pl.BlockSpec((HALF, tm), lambda i, j: (0, j)),
                  const2(CONV_K, CONV_W),
                  pl.BlockSpec((1, n_mem, MEM_W), lambda i, j: (i, 0, 0)),
                  pl.BlockSpec((1, n_mem, MEM_W), lambda i, j: (i, 0, 0))],
        out_specs=out_specs,
        out_shape=out_shape,
        scratch_shapes=[pltpu.VMEM((8, CONV_W), F32)],
        compiler_params=_params("parallel", "arbitrary"),
        name="inproj",
    )(x, wcat, wt, cos, sin, cos_t, sin_t, conv_w, mk, mv)


def _dsa_kernel(qit_ref, wit_ref, ki_ref, qt_ref, k_ref, vt_ref, o_ref,
                sc_ref, j_ref, m_ref, l_ref, acc_ref, *, topk):
    tq = qt_ref.shape[3]
    kc = sc_ref.shape[1]
    i = pl.program_id(1)
    nk = i + 1
    t0 = i * tq
    wt = wit_ref[0]
    key_pos = lax.broadcasted_iota(I32, (kc, tq), 0)
    q_pos = lax.broadcasted_iota(I32, (kc, tq), 1) + t0

    def keys_of(c):
        return pl.ds(pl.multiple_of(c * kc, kc), kc)

    def score_body(c, carry):
        kic = ki_ref[0, keys_of(c), :]
        acc = jnp.zeros((kc, tq), F32)
        for h in range(N_IDX_HEADS):
            acc = acc + wt[h:h + 1, :] * jnp.maximum(_dot(kic, qit_ref[0, h]), 0.0)
        sc_ref[c] = jnp.where(key_pos + c * kc <= q_pos, acc, -jnp.inf)
        return carry

    lax.fori_loop(0, nk, score_body, 0)

    def fold(v):
        return jnp.sum(v.reshape(kc // SUBLANES, SUBLANES, tq), axis=0)

    def count(pred):
        def body(c, cnt):
            return cnt + fold(jnp.where(pred(sc_ref[c], key_pos + c * kc), 1.0, 0.0))
        cnt = lax.fori_loop(0, nk, body, jnp.zeros((SUBLANES, tq), F32))
        return jnp.sum(cnt, axis=0, keepdims=True)

    def count3(c1, c2, c3):
        def body(c, acc):
            sc = sc_ref[c]
            v = jnp.where(sc >= c3, 1 + (1 << FIELD) + (1 << 2 * FIELD),
                          jnp.where(sc >= c2, 1 + (1 << FIELD), jnp.where(sc >= c1, 1, 0)))
            return acc + fold(v)
        acc = lax.fori_loop(0, nk, body, jnp.zeros((SUBLANES, tq), I32))
        mask = (1 << FIELD) - 1
        return [jnp.sum(f.astype(F32), axis=0, keepdims=True)
                for f in (acc & mask, (acc >> FIELD) & mask, acc >> 2 * FIELD)]

    def bit_body(it, carry):
        thr, cnt_thr = carry
        hi = lax.shift_left(jnp.int32(1), 31 - 2 * it)
        lo = lax.shift_left(jnp.int32(1), 30 - 2 * it)
        cands = (thr ^ lo, thr ^ hi, thr ^ (hi | lo))
        cnts = count3(*[_key_to_float(cj) for cj in cands])
        for cj, nj in zip(cands, cnts):
            ok = nj >= topk
            thr = jnp.where(ok, cj, thr)
            cnt_thr = jnp.where(ok, nj, cnt_thr)
        return thr, cnt_thr

    thr_key, cnt_thr = lax.fori_loop(
        0, 16, bit_body, (jnp.full((1, tq), INT_MIN, I32), jnp.full((1, tq), float(topk), F32)))
    thr = _key_to_float(thr_key)

    j_ref[...] = jnp.full_like(j_ref, BIG_IDX)
    tie_q = jnp.where((thr_key > KEY_OF_NEG_INF) & (cnt_thr > topk), 1.0, 0.0)

    @pl.when(jnp.max(tie_q) > 0.0)
    def _():
        need = topk - count(lambda sc, pos: sc > thr)

        def idx_body(it, jj):
            cand = jj | lax.shift_left(jnp.int32(1), 29 - it)
            below = count(lambda sc, pos: (sc == thr) & (pos < cand))
            return jnp.where(below < need, cand, jj)

        j_ref[...] = lax.fori_loop(0, 30, idx_body, jnp.zeros((1, tq), I32))

    jlast = j_ref[...]

    m_ref[...] = jnp.full_like(m_ref, NEG)
    l_ref[...] = jnp.zeros_like(l_ref)
    acc_ref[...] = jnp.zeros_like(acc_ref)

    def att_body(c, carry):
        sc = sc_ref[c]
        pos = key_pos + c * kc
        sel = ((sc > thr) | ((sc == thr) & (pos <= jlast))) & (pos <= q_pos)
        bias = jnp.where(sel, 0.0, NEG)
        for h0 in range(0, N_HEADS, HEADS_PER_BATCH):
            hs = slice(h0, h0 + HEADS_PER_BATCH)
            s = jnp.einsum('hkd,hdt->hkt', k_ref[0, hs, keys_of(c), :], qt_ref[0, hs],
                           preferred_element_type=F32) + bias[None]
            m_old = m_ref[hs]
            m_new = jnp.maximum(m_old, jnp.max(s, axis=1, keepdims=True))
            alpha = jnp.exp2(m_old - m_new)
            p = jnp.exp2(s - m_new)
            l_ref[hs] = alpha * l_ref[hs] + jnp.sum(p, axis=1, keepdims=True)
            pv = jnp.einsum('hdk,hkt->hdt', vt_ref[0, hs, c], p.astype(BF16), preferred_element_type=F32)
            acc_ref[hs] = alpha * acc_ref[hs] + pv
            m_ref[hs] = m_new
        return carry

    lax.fori_loop(0, nk, att_body, 0)
    for h in range(N_HEADS):
        o_ref[0, :, h * HEAD_DIM:(h + 1) * HEAD_DIM] = (acc_ref[h] / l_ref[h]).T.astype(BF16)


def _dsa(qit, wit, kib, qt, kb, vtb, tq, topk):
    b, _, _, t = qt.shape
    nq = t // tq
    assert tq % LANES == 0 and (tq // SUBLANES) * nq < (1 << FIELD), "per-slot key counts must fit a packed field"
    hm_t = pl.BlockSpec((1, N_HEADS, HEAD_DIM, tq), lambda i, j: (i, 0, 0, j))
    return pl.pallas_call(
        functools.partial(_dsa_kernel, topk=topk),
        grid=(b, nq),
        in_specs=[hm_t,
                  pl.BlockSpec((1, N_IDX_HEADS, tq), lambda i, j: (i, 0, j)),
                  pl.BlockSpec((1, t, IDX_DIM), lambda i, j: (i, 0, 0)),
                  hm_t,
                  pl.BlockSpec((1, N_HEADS, t, HEAD_DIM), lambda i, j: (i, 0, 0, 0)),
                  pl.BlockSpec((1, N_HEADS, nq, HEAD_DIM, tq), lambda i, j: (i, 0, 0, 0, 0))],
        out_specs=pl.BlockSpec((1, tq, ATTN_W), lambda i, j: (i, j, 0)),
        out_shape=jax.ShapeDtypeStruct((b, t, ATTN_W), BF16),
        scratch_shapes=[pltpu.VMEM((nq, tq, tq), F32), pltpu.VMEM((1, tq), I32),
                        pltpu.VMEM((N_HEADS, 1, tq), F32), pltpu.VMEM((N_HEADS, 1, tq), F32),
                        pltpu.VMEM((N_HEADS, HEAD_DIM, tq), F32)],
        compiler_params=_params("parallel", "arbitrary"),
        name="dsa_prompt",
    )(qit, wit, kib, qt, kb, vtb)


def _merge_kernel(x_ref, conv_ref, attn_ref, mem_ref, wg_ref, bg_ref, wc_ref, wa_ref, wm_ref, wo_ref,
                  g1_ref, b1_ref, wrh_ref, wrl_ref, br_ref, hg_ref, *, alpha):
    x = x_ref[...]
    xb = x.astype(BF16)
    d = x.shape[1]
    cw = 2 * LANES
    blocks = []
    for c0 in range(0, d, cw):
        m = None
        for br, (src, wref) in enumerate(((conv_ref, wc_ref), (attn_ref, wa_ref), (mem_ref, wm_ref))):
            gc = slice(br * d + c0, br * d + c0 + cw)
            g = jax.nn.sigmoid(_dot(xb, wg_ref[:, gc]) + bg_ref[:, gc])
            term = g * _dot(src[...], wref[:, c0:c0 + cw])
            m = term if m is None else m + term
        blocks.append(m.astype(BF16))
    mb = jnp.concatenate(blocks, axis=1)
    h = _layernorm(alpha * x + _dot(mb, wo_ref[...]), g1_ref[...], b1_ref[...])
    hg_ref[:, :d] = h

    hh = h.astype(BF16)
    hl = (h - hh.astype(F32)).astype(BF16)
    lo = _dot(hh, wrh_ref[...]) + _dot(hl, wrh_ref[...]) + _dot(hh, wrl_ref[...]) + br_ref[...]
    lane = lax.broadcasted_iota(I32, lo.shape, 1)
    lanef = lane.astype(F32)
    is_g = (lane >= N_EXPERTS) & (lane < N_EXPERTS + N_GROUPS)
    mg = jnp.max(jnp.where(is_g, lo, -jnp.inf), axis=1, keepdims=True)
    sg = jnp.sum(jnp.where(is_g, jnp.exp(lo - mg), 0.0), axis=1, keepdims=True)
    gw = 1.0 / sg
    gsel = jnp.min(jnp.where(is_g & (lo == mg), lanef, 1e9), axis=1, keepdims=True) - N_EXPERTS
    in_g = (lane < N_EXPERTS) & ((lane >> 3).astype(F32) == gsel)
    me = jnp.max(jnp.where(in_g, lo, -jnp.inf), axis=1, keepdims=True)
    ee = jnp.where(in_g, jnp.exp(lo - me), 0.0)
    pe = jnp.where(in_g, ee / jnp.sum(ee, axis=1, keepdims=True), -1.0)
    p1 = jnp.max(pe, axis=1, keepdims=True)
    i1 = jnp.min(jnp.where(pe == p1, lanef, 1e9), axis=1, keepdims=True)
    pe2 = jnp.where(lanef == i1, -1.0, pe)
    p2 = jnp.max(pe2, axis=1, keepdims=True)
    i2 = jnp.min(jnp.where(pe2 == p2, lanef, 1e9), axis=1, keepdims=True)
    nrm = p1 + p2
    gate = gw * jnp.where(lanef == i1, p1 / nrm, jnp.where(lanef == i2, p2 / nrm, 0.0))
    hg_ref[:, d:] = jnp.where(lane == N_EXPERTS, gsel, gate)


def _merge(x, conv, attn, mem, wts, tm, alpha):
    n, d = x.shape
    tok = lambda w: pl.BlockSpec((tm, w), lambda i: (i, 0))
    const = lambda a: pl.BlockSpec(a.shape, lambda i: (0, 0))
    return pl.pallas_call(
        functools.partial(_merge_kernel, alpha=alpha),
        grid=(n // tm,),
        in_specs=[tok(d), tok(CONV_W), tok(ATTN_W), tok(MEM_W)] + [const(a) for a in wts],
        out_specs=tok(d + LANES),
        out_shape=jax.ShapeDtypeStruct((n, d + LANES), F32),
        compiler_params=_params("parallel"),
        name="merge_ln_router",
    )(x, conv, attn, mem, *wts)


def _route_kernel(rec_ref, pos_ref, tg_ref, cnt_ref, start_ref, carry_ref, earlier_ref, *, tile_rows):
    phase = pl.program_id(0)
    i = pl.program_id(1)
    tm = rec_ref.shape[0]
    rec = rec_ref[...]
    lane = lax.broadcasted_iota(I32, rec.shape, 1)
    lanef = lane.astype(F32)
    gsel = jnp.sum(jnp.where(lane == N_EXPERTS, rec, 0.0), axis=1, keepdims=True)
    onehot = jnp.where(lanef == gsel, 1.0, 0.0)

    @pl.when((phase == 0) & (i == 0))
    def _():
        cnt_ref[...] = jnp.zeros_like(cnt_ref)

    @pl.when(phase == 0)
    def _():
        cnt_ref[...] += jnp.sum(onehot, axis=0, keepdims=True)

    @pl.when((phase == 1) & (i == 0))
    def _():
        n_tiles = jnp.ceil(cnt_ref[...] * (1.0 / tile_rows))
        n_tiles = jnp.where(lane[:1] < N_GROUPS, n_tiles, 0.0)
        r = lax.broadcasted_iota(I32, (LANES, LANES), 0)
        c = lax.broadcasted_iota(I32, (LANES, LANES), 1)
        before = jnp.where(r < c, 1.0, 0.0).astype(BF16)
        first = _dot(jnp.broadcast_to(n_tiles, (SUBLANES, LANES)).astype(BF16), before)[:1]
        start_ref[...] = first * tile_rows
        carry_ref[...] = jnp.zeros_like(carry_ref)
        tile = lane[:1].astype(F32)
        total = jnp.sum(n_tiles, axis=1, keepdims=True)
        grp = jnp.zeros((1, LANES), F32) - 1.0
        for g in range(N_GROUPS):
            first_g = jnp.sum(jnp.where(lane[:1] == g, first, 0.0), axis=1, keepdims=True)
            n_g = jnp.sum(jnp.where(lane[:1] == g, n_tiles, 0.0), axis=1, keepdims=True)
            grp = jnp.where((tile >= first_g) & (tile < first_g + n_g), float(g), grp)
        tg_ref[...] = jnp.where(tile < total, grp, -1.0).astype(I32)

    @pl.when((phase == 1) & (i == 0))
    def _():
        r = lax.broadcasted_iota(I32, (tm, tm), 0)
        c = lax.broadcasted_iota(I32, (tm, tm), 1)
        earlier_ref[...] = jnp.where(c < r, 1.0, 0.0).astype(BF16)

    @pl.when(phase == 1)
    def _():
        rank = _dot(earlier_ref[...], onehot.astype(BF16))
        slot = jnp.sum(jnp.where(lanef == gsel, rank + carry_ref[...] + start_ref[...], 0.0), axis=1, keepdims=True)
        carry_ref[...] += jnp.sum(onehot, axis=0, keepdims=True)
        hi = jnp.floor(slot * (1.0 / 256.0))
        lo = slot - hi * 256.0
        ones = jnp.ones((SUBLANES, LANES), BF16)
        spread = lambda v: _dot_nt(ones, jnp.where(lane == 0, v, 0.0).astype(BF16))[:1]
        pos_ref[0] = (spread(hi) * 256.0 + spread(lo)).astype(I32)


def _route(hg, d, tm, tile_rows):
    n = hg.shape[0]
    nt = n // tm
    pos, tile_group = pl.pallas_call(
        functools.partial(_route_kernel, tile_rows=tile_rows),
        grid=(2, nt),
        in_specs=[pl.BlockSpec((tm, LANES), lambda p, i: (i, d // LANES))],
        out_specs=[pl.BlockSpec((1, 1, tm), lambda p, i: (i * p, 0, 0)),
                   pl.BlockSpec((1, LANES), lambda p, i: (0, 0))],
        out_shape=[jax.ShapeDtypeStruct((nt, 1, tm), I32), jax.ShapeDtypeStruct((1, LANES), I32)],
        scratch_shapes=[pltpu.VMEM((1, LANES), F32), pltpu.VMEM((1, LANES), F32), pltpu.VMEM((1, LANES), F32),
                        pltpu.VMEM((tm, tm), BF16)],
        compiler_params=_params("arbitrary", "arbitrary"),
        name="moe_route",
    )(hg)
    return pos.reshape(n), tile_group[0]


def _row_copy(src_ref, dst_ref, sem, src_row, dst_row):
    return pltpu.make_async_copy(src_ref.at[pl.ds(src_row, 1)], dst_ref.at[pl.ds(dst_row, 1)], sem)


def _permute_kernel(idx_ref, src_ref, *rest, scatter):
    dst_ref, sem = rest[-2:]
    ch = idx_ref.shape[2]

    def rows(t):
        j = idx_ref[0, 0, t]
        return (t, j) if scatter else (j, t)

    def start(t, carry):
        _row_copy(src_ref, dst_ref, sem, *rows(t)).start()
        return carry

    def wait(t, carry):
        _row_copy(src_ref, dst_ref, sem, *rows(t)).wait()
        return carry

    lax.fori_loop(0, ch, start, 0, unroll=8)
    lax.fori_loop(0, ch, wait, 0, unroll=8)


def _permute_rows(src, idx, n_out, scatter, chunk):
    n_idx = idx.shape[0]
    width = src.shape[1]
    any_space = pl.BlockSpec(memory_space=pl.ANY)
    tile = pl.BlockSpec((chunk, width), lambda i: (i, 0))
    extra = (jnp.zeros((n_out, width), src.dtype),) if scatter else ()
    return pl.pallas_call(
        functools.partial(_permute_kernel, scatter=scatter),
        grid=(n_idx // chunk,),
        in_specs=[pl.BlockSpec((1, 1, chunk), lambda i: (i, 0, 0), memory_space=pltpu.SMEM),
                  tile if scatter else any_space] + [any_space] * len(extra),
        out_specs=any_space if scatter else tile,
        out_shape=jax.ShapeDtypeStruct((n_out, width), src.dtype),
        scratch_shapes=[pltpu.SemaphoreType.DMA],
        input_output_aliases={2: 0} if scatter else {},
        compiler_params=_params("arbitrary"),
        name="permute_rows_scatter" if scatter else "permute_rows_gather",
    )(idx.reshape(n_idx // chunk, 1, chunk), src, *extra)


def _moe_group_kernel(tg_ref, x_ref, wu_ref, wd_ref, g2_ref, b2_ref, o_ref, hb_ref, y_ref, *, alpha):
    i = pl.program_id(0)
    e = pl.program_id(1)
    d = o_ref.shape[1]
    de = wd_ref.shape[1]
    grp = tg_ref[i]

    @pl.when(grp < 0)
    def _():
        o_ref[...] = jnp.zeros_like(o_ref)

    @pl.when(grp >= 0)
    def _():
        @pl.when(e == 0)
        def _():
            hb_ref[...] = x_ref[:, :d].astype(BF16)
            y_ref[...] = jnp.zeros_like(y_ref)

        hu = _dot(hb_ref[...], wu_ref[0])
        a = hu[:, :de]
        act = (a * (1.0 / (1.0 + jnp.exp(-a))) * hu[:, de:]).astype(BF16)
        rec = x_ref[:, d:]
        lane = lax.broadcasted_iota(I32, rec.shape, 1)
        ge = jnp.sum(jnp.where(lane == grp * EXPERTS_PER_GROUP + e, rec, 0.0), axis=1, keepdims=True)
        y_ref[...] += ge * _dot(act, wd_ref[0])

        @pl.when(e == pl.num_programs(1) - 1)
        def _():
            o_ref[...] = _layernorm(alpha * x_ref[:, :d] + y_ref[...], g2_ref[...], b2_ref[...])


def _moe_grouped(xs, tile_group, wu, wd, g2, b2, tm, alpha):
    ns, width = xs.shape
    d = width - LANES
    _, _, du = wu.shape
    de = wd.shape[1]
    expert = lambda i, e, tg: (jnp.maximum(tg[i], 0) * EXPERTS_PER_GROUP + e, 0, 0)
    grid_spec = pltpu.PrefetchScalarGridSpec(
        num_scalar_prefetch=1,
        grid=(ns // tm, EXPERTS_PER_GROUP),
        in_specs=[pl.BlockSpec((tm, width), lambda i, e, tg: (i, 0)),
                  pl.BlockSpec((1, d, du), expert),
                  pl.BlockSpec((1, de, d), expert),
                  pl.BlockSpec((1, d), lambda i, e, tg: (0, 0)),
                  pl.BlockSpec((1, d), lambda i, e, tg: (0, 0))],
        out_specs=pl.BlockSpec((tm, d), lambda i, e, tg: (i, 0)),
        scratch_shapes=[pltpu.VMEM((tm, d), BF16), pltpu.VMEM((tm, d), F32)],
    )
    return pl.pallas_call(
        functools.partial(_moe_group_kernel, alpha=alpha),
        grid_spec=grid_spec,
        out_shape=jax.ShapeDtypeStruct((ns, d), F32),
        compiler_params=_params("arbitrary", "arbitrary"),
        name="moe_grouped_ln",
    )(tile_group, xs, wu, wd, g2, b2)


def _moe_kernel(h_ref, gate_ref, wu_ref, wd_ref, g2_ref, b2_ref, o_ref, hb_ref, y_ref, *, alpha):
    e = pl.program_id(1)
    de = wd_ref.shape[1]

    @pl.when(e == 0)
    def _():
        hb_ref[...] = h_ref[...].astype(BF16)
        y_ref[...] = jnp.zeros_like(y_ref)

    hu = _dot(hb_ref[...], wu_ref[0])
    a = hu[:, :de]
    act = (a * (1.0 / (1.0 + jnp.exp(-a))) * hu[:, de:]).astype(BF16)
    gate = gate_ref[...]
    lane = lax.broadcasted_iota(I32, gate.shape, 1)
    ge = jnp.sum(jnp.where(lane == e, gate, 0.0), axis=1, keepdims=True)
    y_ref[...] += ge * _dot(act, wd_ref[0])

    @pl.when(e == pl.num_programs(1) - 1)
    def _():
        o_ref[...] = _layernorm(alpha * h_ref[...] + y_ref[...], g2_ref[...], b2_ref[...])


def _moe(h, gate, wu, wd, g2, b2, tm, alpha):
    n, d = h.shape
    ne, _, du = wu.shape
    de = wd.shape[1]
    return pl.pallas_call(
        functools.partial(_moe_kernel, alpha=alpha),
        grid=(n // tm, ne),
        in_specs=[pl.BlockSpec((tm, d), lambda i, e: (i, 0)),
                  pl.BlockSpec((tm, ne), lambda i, e: (i, 0)),
                  pl.BlockSpec((1, d, du), lambda i, e: (e, 0, 0)),
                  pl.BlockSpec((1, de, d), lambda i, e: (e, 0, 0)),
                  pl.BlockSpec((1, d), lambda i, e: (0, 0)),
                  pl.BlockSpec((1, d), lambda i, e: (0, 0))],
        out_specs=pl.BlockSpec((tm, d), lambda i, e: (i, 0)),
        out_shape=jax.ShapeDtypeStruct((n, d), F32),
        scratch_shapes=[pltpu.VMEM((tm, d), BF16), pltpu.VMEM((tm, d), F32)],
        compiler_params=_params("parallel", "arbitrary"),
        name="moe_ln",
    )(h, gate, wu, wd, g2, b2)


def _sample_inproj_kernel(x_ref, w_ref, wt_ref, cos_ref, sin_ref, cw_ref, p0_ref, p1_ref,
                          conv_ref, z_ref, q_ref, k_ref, v_ref, qi_ref, ki_ref, wi_ref, qm_ref):
    xb = x_ref[...].astype(BF16)
    cos = cos_ref[...]
    sin = sin_ref[...]

    def proj(a, b):
        return _dot(xb, w_ref[:, a:b])

    def proj_rows(a, b):
        return _dot_nt(xb, wt_ref[a:b, :])

    ucc = proj(C_U, C_K)
    z = ucc[:, C_C:C_K] * ucc[:, C_U:C_B]
    cw = cw_ref[...]
    y = p0_ref[...] * cw[0:1, :] + p1_ref[...] * cw[1:2, :] + z * cw[2:3, :]
    conv_ref[...] = (ucc[:, C_B:C_C] * y).astype(BF16)
    z_ref[...] = z
    q_ref[...] = _rope(proj_rows(R_Q, R_QI), cos, sin) * (HEAD_DIM ** -0.5)
    k_ref[...] = _rope(proj(C_K, C_KI), cos, sin)
    v_ref[...] = proj_rows(R_V, R_KI)
    qi_ref[...] = _rope(proj_rows(R_QI, R_K), cos, sin) * (IDX_DIM ** -0.5)
    ki_ref[...] = _rope(proj(C_KI, C_QM), cos, sin)[:, :IDX_DIM]
    wi_ref[...] = proj_rows(R_WI, R_END)[:, :N_IDX_HEADS] * (N_IDX_HEADS ** -0.5)
    qm_ref[...] = proj(C_QM, C_END) * (HEAD_DIM ** -0.5)


def _sample_inproj(x, wcat, wt, cos, sin, conv_w, p0, p1):
    n, d = x.shape
    args = (x, wcat, wt, cos, sin, conv_w, p0, p1)
    widths = (CONV_W, CONV_W, ATTN_W, ATTN_W, ATTN_W, N_IDX_HEADS * IDX_DIM, IDX_DIM, N_IDX_HEADS, MEM_W)
    dtypes = (BF16,) + (F32,) * 8
    full = lambda shape: pl.BlockSpec(shape, lambda i: (0,) * len(shape))
    return pl.pallas_call(
        _sample_inproj_kernel,
        grid=(1,),
        in_specs=[full(a.shape) for a in args],
        out_specs=[full((n, w)) for w in widths],
        out_shape=[jax.ShapeDtypeStruct((n, w), dt) for w, dt in zip(widths, dtypes)],
        compiler_params=_params("arbitrary"),
        name="sample_inproj",
    )(*args)


def _head_mask(rows, width):
    r = lax.broadcasted_iota(I32, (rows, width), 0)
    c = lax.broadcasted_iota(I32, (rows, width), 1)
    return (c >> 6) == r


def _sample_mem_kernel(qm_ref, mk_ref, mv_ref, o_ref):
    hmask = _head_mask(8, MEM_W)
    qbd = jnp.where(hmask, jnp.broadcast_to(qm_ref[0], (8, MEM_W)), 0.0).astype(BF16)
    p = _softmax_rows(_dot_nt(qbd, mk_ref[0].astype(BF16)))
    o = _dot(p.astype(BF16), mv_ref[0].astype(BF16))
    o_ref[0] = jnp.sum(jnp.where(hmask, o, 0.0), axis=0, keepdims=True).astype(BF16)


def _sample_mem(qm, mk, mv):
    n, n_mem, w = mk.shape
    return pl.pallas_call(
        _sample_mem_kernel,
        grid=(n,),
        in_specs=[pl.BlockSpec((1, 1, w), lambda i: (i, 0, 0)),
                  pl.BlockSpec((1, n_mem, w), lambda i: (i, 0, 0)),
                  pl.BlockSpec((1, n_mem, w), lambda i: (i, 0, 0))],
        out_specs=pl.BlockSpec((1, 1, w), lambda i: (i, 0, 0)),
        out_shape=jax.ShapeDtypeStruct((n, 1, w), BF16),
        compiler_params=_params("parallel"),
        name="sample_mem_attn",
    )(qm, mk, mv)


def _sample_scores_kernel(pt_ref, *refs, pages):
    page_refs = refs[:pages]
    qi_ref, wi_ref, o_ref = refs[pages:]
    keys_t = jnp.concatenate([r[0] for r in page_refs], axis=1).astype(BF16)
    s = _dot(qi_ref[0].astype(BF16), keys_t)
    o_ref[0, 0] = jnp.sum(wi_ref[0] * jnp.maximum(s, 0.0), axis=0, keepdims=True)


def _sample_scores(page_table, cki_t, qi, wi, pages):
    n, n_pages = page_table.shape
    ng = n_pages // pages
    width = pages * PAGE_SIZE

    def page_spec(j):
        return pl.BlockSpec((1, IDX_DIM, PAGE_SIZE), lambda s, pt: (pt[s // ng, (s % ng) * pages + j], 0, 0))

    grid_spec = pltpu.PrefetchScalarGridSpec(
        num_scalar_prefetch=1,
        grid=(n * ng,),
        in_specs=[page_spec(j) for j in range(pages)] + [
            pl.BlockSpec((1, N_IDX_HEADS, IDX_DIM), lambda s, pt: (s // ng, 0, 0)),
            pl.BlockSpec((1, N_IDX_HEADS, 1), lambda s, pt: (s // ng, 0, 0))],
        out_specs=pl.BlockSpec((1, 1, 1, width), lambda s, pt: (s // ng, s % ng, 0, 0)),
    )
    out = pl.pallas_call(
        functools.partial(_sample_scores_kernel, pages=pages),
        grid_spec=grid_spec,
        out_shape=jax.ShapeDtypeStruct((n, ng, 1, width), F32),
        compiler_params=_params("arbitrary"),
        name="sample_idx_scores",
    )(page_table, *([cki_t] * pages), qi, wi)
    return out.reshape(n, n_pages * PAGE_SIZE)


def _sample_select_kernel(sc_ref, qi_ref, ki_ref, wi_ref, bias_ref, bnew_ref, j_ref, *, topk):
    n, past = sc_ref.shape
    qi = qi_ref[...].astype(BF16).astype(F32)
    ki = ki_ref[...].astype(BF16).astype(F32)
    s_new = jnp.maximum(jnp.sum(qi * ki, axis=2), 0.0)
    sc_new = jnp.sum(wi_ref[...] * s_new, axis=1, keepdims=True)
    idx = lax.broadcasted_iota(I32, (n, past), 1)

    def count(pred):
        one = jnp.where(pred(sc_ref[...], idx), 1.0, 0.0)
        return jnp.sum(one, axis=1, keepdims=True) + jnp.where(pred(sc_new, past), 1.0, 0.0)

    def bit_body(it, carry):
        thr, cnt_thr = carry
        cand = thr ^ lax.shift_left(jnp.int32(1), 31 - it)
        cand_f = _key_to_float(cand)
        cnt = count(lambda sc, ii: sc >= cand_f)
        ok = cnt >= topk
        return jnp.where(ok, cand, thr), jnp.where(ok, cnt, cnt_thr)

    thr_key, cnt_thr = lax.fori_loop(
        0, 32, bit_body, (jnp.full((n, 1), INT_MIN, I32), jnp.full((n, 1), float(topk), F32)))
    thr = _key_to_float(thr_key)

    j_ref[...] = jnp.full((n, 1), BIG_IDX, I32)

    @pl.when(jnp.max(jnp.where(cnt_thr > topk, 1.0, 0.0)) > 0.0)
    def _():
        need = topk - count(lambda sc, ii: sc > thr)

        def idx_body(it, jj):
            cand = jj | lax.shift_left(jnp.int32(1), 29 - it)
            below = count(lambda sc, ii: (sc == thr) & (ii < cand))
            return jnp.where(below < need, cand, jj)

        j_ref[...] = lax.fori_loop(0, 30, idx_body, jnp.zeros((n, 1), I32))

    jlast = j_ref[...]
    sc = sc_ref[...]
    bias_ref[...] = jnp.where((sc > thr) | ((sc == thr) & (idx <= jlast)), 0.0, NEG)
    bnew_ref[...] = jnp.where((sc_new > thr) | ((sc_new == thr) & (past <= jlast)), 0.0, NEG)


def _sample_select(scores, qi, ki, wi, topk):
    n, past = scores.shape
    args = (scores, qi, ki, wi)
    full = lambda shape: pl.BlockSpec(shape, lambda i: (0,) * len(shape))
    return pl.pallas_call(
        functools.partial(_sample_select_kernel, topk=topk),
        grid=(1,),
        in_specs=[full(a.shape) for a in args],
        out_specs=[full((n, past)), full((n, 1))],
        out_shape=[jax.ShapeDtypeStruct((n, past), F32), jax.ShapeDtypeStruct((n, 1), F32)],
        scratch_shapes=[pltpu.VMEM((n, 1), I32)],
        compiler_params=_params("arbitrary"),
        name="sample_topk_select",
    )(*args)


def _sample_attn_kernel(pt_ref, *refs, pages, nch):
    k_refs = refs[:pages]
    v_refs = refs[pages:2 * pages]
    q_ref, bias_ref, bnew_ref, kn_ref, vn_ref, o_ref, m_ref, l_ref, acc_ref = refs[2 * pages:]
    c = pl.program_id(0) % nch

    @pl.when(c == 0)
    def _():
        m_ref[...] = jnp.full_like(m_ref, NEG)
        l_ref[...] = jnp.zeros_like(l_ref)
        acc_ref[...] = jnp.zeros_like(acc_ref)

    rows = []
    for h in range(N_HEADS):
        qh = q_ref[0, h]
        rows.append(jnp.concatenate([jnp.sum(r[0, h] * qh, axis=0, keepdims=True) for r in k_refs], axis=1))
    s = jnp.concatenate(rows, axis=0) + bias_ref[0, 0]
    m_old = m_ref[...]
    m_new = jnp.maximum(m_old, jnp.max(s, axis=1, keepdims=True))
    alpha = jnp.exp(m_old - m_new)
    p = jnp.exp(s - m_new)
    l_ref[...] = alpha * l_ref[...] + jnp.sum(p, axis=1, keepdims=True)
    m_ref[...] = m_new
    for h in range(N_HEADS):
        acc = alpha[h:h + 1, :] * acc_ref[h]
        for j, r in enumerate(v_refs):
            acc = acc + r[0, h] * p[h:h + 1, j * PAGE_SIZE:(j + 1) * PAGE_SIZE]
        acc_ref[h] = acc

    @pl.when(c == nch - 1)
    def _():
        bn = bnew_ref[0]
        for h in range(N_HEADS):
            s_n = jnp.sum(q_ref[0, h] * kn_ref[0, h], axis=0, keepdims=True) + bn
            m0 = m_ref[h:h + 1, :]
            m1 = jnp.maximum(m0, s_n)
            a1 = jnp.exp(m0 - m1)
            p_n = jnp.exp(s_n - m1)
            l1 = a1 * l_ref[h:h + 1, :] + p_n
            out = a1 * jnp.sum(acc_ref[h], axis=1, keepdims=True) + p_n * vn_ref[0, h]
            o_ref[0, h] = out / l1


def _sample_attn(page_table, ck_t, cv_t, q, bias, bnew, kn, vn, pages):
    n, n_pages = page_table.shape
    nch = n_pages // pages
    width = pages * PAGE_SIZE

    def page_spec(j):
        return pl.BlockSpec((1, N_HEADS, HEAD_DIM, PAGE_SIZE),
                            lambda s, pt: (pt[s // nch, (s % nch) * pages + j], 0, 0, 0))

    per_seq = pl.BlockSpec((1, N_HEADS, HEAD_DIM, 1), lambda s, pt: (s // nch, 0, 0, 0))
    grid_spec = pltpu.PrefetchScalarGridSpec(
        num_scalar_prefetch=1,
        grid=(n * nch,),
        in_specs=[page_spec(j) for j in range(pages)] * 2 + [
            per_seq,
            pl.BlockSpec((1, 1, 1, width), lambda s, pt: (s // nch, s % nch, 0, 0)),
            pl.BlockSpec((1, 1, 1), lambda s, pt: (s // nch, 0, 0)),
            per_seq, per_seq],
        out_specs=per_seq,
        scratch_shapes=[pltpu.VMEM((N_HEADS, 1), F32), pltpu.VMEM((N_HEADS, 1), F32),
                        pltpu.VMEM((N_HEADS, HEAD_DIM, PAGE_SIZE), F32)],
    )
    return pl.pallas_call(
        functools.partial(_sample_attn_kernel, pages=pages, nch=nch),
        grid_spec=grid_spec,
        out_shape=jax.ShapeDtypeStruct((n, N_HEADS, HEAD_DIM, 1), F32),
        compiler_params=_params("arbitrary"),
        name="sample_attn",
    )(page_table, *([ck_t] * pages), *([cv_t] * pages), q, bias.reshape(n, nch, 1, width), bnew, kn, vn)


def _rope_tables(pos):
    inv = ROPE_THETA ** (-jnp.arange(HALF, dtype=F32) / HALF)
    ang = pos.astype(F32)[:, None] * inv[None, :]
    cos = jnp.cos(ang)
    sin = jnp.sin(ang)
    return jnp.concatenate([cos] * 4, axis=1), jnp.concatenate([-sin, sin] * 2, axis=1), cos.T, sin.T


def _largest_divisor(n, cap):
    for c in range(min(cap, n), 0, -1):
        if n % c == 0:
            return c
    return 1


def _pack_layer_weights(l, w_in, b_gate, w_br_conv, w_br_attn, w_br_mem, w_o, ln1_g, ln1_b, w_group, b_group,
                        w_expert, b_expert):
    d = w_in.shape[1]
    sizes = (CONV_W, CONV_W, CONV_W, ATTN_W, ATTN_W, ATTN_W, N_IDX_HEADS * IDX_DIM, IDX_DIM, N_IDX_HEADS, MEM_W,
             N_BRANCH * d)
    offs = [0]
    for s in sizes:
        offs.append(offs[-1] + s)
    col = lambda i: w_in[l][:, offs[i]:offs[i + 1]]
    u, c_b, c_c, q, k, v, qi, ki, wi, qm, gates = (col(i) for i in range(len(sizes)))
    zeros = lambda n: jnp.zeros((d, n), F32)
    wcat = jnp.concatenate([u, c_b, c_c, k, ki, zeros(LANES - IDX_DIM), qm], axis=1).astype(BF16)
    wt = jnp.concatenate([q, qi, k, v, ki, wi, zeros(16 - N_IDX_HEADS)], axis=1).T.astype(BF16)
    wr = jnp.concatenate([w_expert[l], w_group[l], zeros(LANES - N_EXPERTS - N_GROUPS)], axis=1)
    wrh = wr.astype(BF16)
    wrl = (wr - wrh.astype(F32)).astype(BF16)
    br = jnp.concatenate([b_expert[l], b_group[l], jnp.zeros((LANES - N_EXPERTS - N_GROUPS,), F32)])[None, :]
    merge_w = (gates.astype(BF16), b_gate[l][None, :], w_br_conv[l].astype(BF16), w_br_attn[l].astype(BF16),
               w_br_mem[l].astype(BF16), w_o[l].astype(BF16), ln1_g[l][None, :], ln1_b[l][None, :], wrh, wrl, br)
    return wcat, wt, merge_w


def kernel(x_prompt, x_sample, mem_prompt, cache_k, cache_v, cache_k_idx, state_conv, cache_mem_k, cache_mem_v,
           page_table, w_in, b_gate, conv_w, w_br_conv, w_br_attn, w_br_mem, w_o, w_mem_k, w_mem_v, ln1_g, ln1_b,
           w_group, b_group, w_expert, b_expert, w_up, w_down, ln2_g, ln2_b):
    depth = w_in.shape[0]
    b, s_len, d = x_prompt.shape
    bd, t_new, _ = x_sample.shape
    assert t_new == 1, "the sample group decodes one token per sequence"
    n_pages = page_table.shape[1]
    past = n_pages * PAGE_SIZE
    alpha = (2.0 * depth) ** 0.25

    tq = _largest_divisor(s_len, 256)
    tm_in = _largest_divisor(s_len, 512)
    assert tm_in % tq == 0
    n_tok = b * s_len
    tm_merge = _largest_divisor(n_tok, 512)
    tm_moe = _largest_divisor(n_tok, 1024)
    perm_chunk = _largest_divisor(n_tok, 1024)
    assert n_tok // tm_moe + N_GROUPS <= LANES, "the tile -> group map is one lane vector"
    topk_p = min(TOPK_MAX, s_len // 4)
    topk_s = min(TOPK_MAX, (past + t_new) // 4)
    pages_idx = _largest_divisor(n_pages, 64)
    pages_att = _largest_divisor(n_pages, 16)

    cos_p, sin_p, cost_p, sint_p = _rope_tables(jnp.arange(s_len, dtype=jnp.int32))
    cos_s, sin_s, _, _ = _rope_tables(past + jnp.arange(t_new, dtype=jnp.int32))

    hp = x_prompt
    hs = x_sample.reshape(bd, d)
    outs = [[] for _ in range(10)]
    for l in range(depth):
        wcat, wt, merge_w = _pack_layer_weights(l, w_in, b_gate, w_br_conv, w_br_attn, w_br_mem, w_o, ln1_g, ln1_b,
                                                w_group, b_group, w_expert, b_expert)
        wu = w_up[l].astype(BF16)
        wd = w_down[l].astype(BF16)
        g2, b2 = ln2_g[l][None, :], ln2_b[l][None, :]

        mk, mv = _memkv(mem_prompt, w_mem_k[l].astype(BF16), w_mem_v[l].astype(BF16))
        (conv_o, mem_o, qt_b, qit_b, wit_p, kt_p, vt_p, k_b, vt_b, kit_p, ki_b, conv_new) = _inproj(
            hp, wcat, wt, cos_p, sin_p, cost_p, sint_p, conv_w[l], mk, mv, tm_in, tq)
        attn_o = _dsa(qit_b, wit_p, ki_b, qt_b, k_b, vt_b, tq, topk_p)
        hg = _merge(hp.reshape(n_tok, d), conv_o.reshape(n_tok, CONV_W), attn_o.reshape(n_tok, ATTN_W),
                    mem_o.reshape(n_tok, MEM_W), merge_w, tm_merge, alpha)
        slot, tile_group = _route(hg, d, tm_moe, tm_moe)
        n_sorted = n_tok + N_GROUPS * tm_moe
        xs = _permute_rows(hg, slot, n_sorted, True, perm_chunk)
        ys = _moe_grouped(xs, tile_group[:n_sorted // tm_moe], wu, wd, g2, b2, tm_moe, alpha)
        hp = _permute_rows(ys, slot, n_tok, False, perm_chunk).reshape(b, s_len, d)
        for lst, val in zip(outs[:6], (jnp.transpose(kt_p, (0, 3, 1, 2)), jnp.transpose(vt_p, (0, 3, 1, 2)),
                                       jnp.transpose(kit_p, (0, 2, 1)), conv_new,
                                       mk.reshape(b, -1, N_MEM_HEADS, HEAD_DIM),
                                       mv.reshape(b, -1, N_MEM_HEADS, HEAD_DIM))):
            lst.append(val)

        st = state_conv[l].astype(F32)
        (conv_s, z_s, q_s, k_s, v_s, qi_s, ki_s, wi_s, qm_s) = _sample_inproj(
            hs, wcat, wt, cos_s, sin_s, conv_w[l], st[:, 0, :], st[:, 1, :])
        mem_s = _sample_mem(qm_s.reshape(bd, 1, MEM_W), cache_mem_k[l].reshape(bd, -1, MEM_W),
                            cache_mem_v[l].reshape(bd, -1, MEM_W))
        qi3 = qi_s.reshape(bd, N_IDX_HEADS, IDX_DIM)
        wi3 = wi_s.reshape(bd, N_IDX_HEADS, 1)
        scores = _sample_scores(page_table, jnp.transpose(cache_k_idx[l], (0, 2, 1)), qi3, wi3, pages_idx)
        bias, bnew = _sample_select(scores, qi3, ki_s.reshape(bd, 1, IDX_DIM), wi_s, topk_s)
        col4 = lambda a: a.reshape(bd, N_HEADS, HEAD_DIM, 1)
        attn_s = _sample_attn(page_table, jnp.transpose(cache_k[l], (0, 2, 3, 1)),
                              jnp.transpose(cache_v[l], (0, 2, 3, 1)), col4(q_s), bias, bnew.reshape(bd, 1, 1),
                              col4(k_s), col4(v_s), pages_att)
        hgs = _merge(hs, conv_s, attn_s.reshape(bd, ATTN_W).astype(BF16), mem_s.reshape(bd, MEM_W), merge_w,
                     bd, alpha)
        hs = _moe(hgs[:, :d], hgs[:, d:d + N_EXPERTS], wu, wd, g2, b2, bd, alpha)
        for lst, val in zip(outs[6:], (k_s.reshape(bd, t_new, N_HEADS, HEAD_DIM),
                                       v_s.reshape(bd, t_new, N_HEADS, HEAD_DIM), ki_s.reshape(bd, t_new, IDX_DIM),
                                       jnp.stack([st[:, 1, :], z_s], axis=1))):
            lst.append(val)

    return (hp, hs.reshape(bd, t_new, d)) + tuple(jnp.stack(o) for o in outs)
```

```python
import functools

import jax
import jax.numpy as jnp
from jax import lax
from jax.experimental import pallas as pl
from jax.experimental.pallas import tpu as pltpu

F32 = jnp.float32
BF16 = jnp.bfloat16
I32 = jnp.int32

HEAD_DIM = 64
HALF = HEAD_DIM // 2
N_HEADS = 8
ATTN_W = N_HEADS * HEAD_DIM
N_IDX_HEADS = 8
IDX_DIM = 64
TOPK_MAX = 256
N_MEM_HEADS = 4
MEM_W = N_MEM_HEADS * HEAD_DIM
CONV_W = 256
CONV_K = 3
N_BRANCH = 3
N_GROUPS = 4
EXPERTS_PER_GROUP = 8
N_EXPERTS = N_GROUPS * EXPERTS_PER_GROUP
PAGE_SIZE = 128
ROPE_THETA = 10000.0
LN_EPS = 1e-5
NEG = -1e30
LANES = 128
SUBLANES = 8

C_U, C_B, C_C = 0, CONV_W, 2 * CONV_W
C_K = 3 * CONV_W
C_KI = C_K + ATTN_W
C_QM = C_KI + LANES
C_END = C_QM + MEM_W
R_Q = 0
R_QI = R_Q + ATTN_W
R_K = R_QI + N_IDX_HEADS * IDX_DIM
R_V = R_K + ATTN_W
R_KI = R_V + ATTN_W
R_WI = R_KI + IDX_DIM
R_END = R_WI + 16

INT_MIN = -(2 ** 31)
KEY_OF_NEG_INF = -2139095041
KEY_OF_POS_INF = 2139095040
BIG_IDX = 2 ** 30
FIELD = 10
Q_SCALE_LOG2 = HEAD_DIM ** -0.5 * 1.4426950408889634
HEADS_PER_BATCH = 8
RING_SLOTS = 3
VMEM_LIMIT = 56 * 1024 * 1024

_NT = (((1,), (1,)), ((), ()))


def _dot(a, b):
    return jnp.dot(a, b, preferred_element_type=F32)


def _dot_nt(a, b):
    return lax.dot_general(a, b, _NT, preferred_element_type=F32)


def _params(*sem):
    return pltpu.CompilerParams(dimension_semantics=sem, vmem_limit_bytes=VMEM_LIMIT)


def _rope(x, cos, sin):
    w = x.shape[1]
    lane = lax.broadcasted_iota(I32, x.shape, 1)
    swapped = jnp.where((lane & 63) < 32, pltpu.roll(x, w - 32, 1), pltpu.roll(x, 32, 1))
    reps = w // LANES
    if reps > 1:
        cos = jnp.concatenate([cos] * reps, axis=1)
        sin = jnp.concatenate([sin] * reps, axis=1)
    return x * cos + swapped * sin


def _rope_t(xt, cos_t, sin_t):
    x1, x2 = xt[:HALF, :], xt[HALF:, :]
    return x1 * cos_t - x2 * sin_t, x1 * sin_t + x2 * cos_t


def _key_to_float(key):
    key = jnp.clip(key, KEY_OF_NEG_INF, KEY_OF_POS_INF)
    return lax.bitcast_convert_type(key ^ ((key >> 31) & 0x7FFFFFFF), F32)


def _layernorm(x, g, b):
    mu = jnp.mean(x, axis=-1, keepdims=True)
    xc = x - mu
    var = jnp.mean(xc * xc, axis=-1, keepdims=True)
    return xc * lax.rsqrt(var + LN_EPS) * g + b


def _softmax_rows(s):
    m = jnp.max(s, axis=-1, keepdims=True)
    e = jnp.exp(s - m)
    return e / jnp.sum(e, axis=-1, keepdims=True)


def _memkv_kernel(mem_ref, wk_ref, wv_ref, mk_ref, mv_ref):
    m = mem_ref[0].astype(BF16)
    mk_ref[0] = _dot(m, wk_ref[...])
    mv_ref[0] = _dot(m, wv_ref[...])


def _memkv(mem, wk, wv):
    b, n_mem, d = mem.shape
    return pl.pallas_call(
        _memkv_kernel,
        grid=(b,),
        in_specs=[pl.BlockSpec((1, n_mem, d), lambda i: (i, 0, 0)),
                  pl.BlockSpec((d, MEM_W), lambda i: (0, 0)),
                  pl.BlockSpec((d, MEM_W), lambda i: (0, 0))],
        out_specs=[pl.BlockSpec((1, n_mem, MEM_W), lambda i: (i, 0, 0))] * 2,
        out_shape=[jax.ShapeDtypeStruct((b, n_mem, MEM_W), F32)] * 2,
        compiler_params=_params("parallel"),
        name="memkv",
    )(mem, wk, wv)


def _inproj_kernel(x_ref, w_ref, wt_ref, cos_ref, sin_ref, cost_ref, sint_ref, cw_ref, mk_ref, mv_ref,
                   conv_ref, mem_ref, qt_ref, qit_ref, wit_ref, kt_ref, vt_ref, kb_ref, vtb_ref, kit_ref, kib_ref,
                   cnew_ref, zprev_ref):
    j = pl.program_id(1)
    tm = x_ref.shape[1]
    kc = vtb_ref.shape[4]
    xb = x_ref[0].astype(BF16)
    cos = cos_ref[...]
    sin = sin_ref[...]
    cos_t = cost_ref[...]
    sin_t = sint_ref[...]

    def proj(a, b):
        return _dot(xb, w_ref[:, a:b])

    def proj_t(a, b):
        return _dot_nt(wt_ref[a:b, :], xb)

    @pl.when(j == 0)
    def _():
        zprev_ref[...] = jnp.zeros_like(zprev_ref)

    ucc = proj(C_U, C_K)
    z = ucc[:, C_C:C_K] * ucc[:, C_U:C_B]
    prev = zprev_ref[...]
    row = lax.broadcasted_iota(I32, z.shape, 0)
    z1 = jnp.where(row == 0, prev[7:8, :], pltpu.roll(z, 1, 0))
    z2 = jnp.where(row == 0, prev[6:7, :], jnp.where(row == 1, prev[7:8, :], pltpu.roll(z, 2, 0)))
    cw = cw_ref[...]
    y = z2 * cw[0:1, :] + z1 * cw[1:2, :] + z * cw[2:3, :]
    conv_ref[0] = (ucc[:, C_B:C_C] * y).astype(BF16)
    zprev_ref[...] = z[tm - 8:tm, :]
    cnew_ref[0] = z[tm - 2:tm, :]

    k = _rope(proj(C_K, C_KI), cos, sin)
    for h in range(N_HEADS):
        kb_ref[0, h] = k[:, h * HEAD_DIM:(h + 1) * HEAD_DIM].astype(BF16)
    kib_ref[0] = _rope(proj(C_KI, C_QM), cos, sin)[:, :IDX_DIM].astype(BF16)

    qt = proj_t(R_Q, R_QI)
    qit = proj_t(R_QI, R_K)
    kt = proj_t(R_K, R_V)
    vt = proj_t(R_V, R_KI)
    for h in range(N_HEADS):
        hs = slice(h * HEAD_DIM, (h + 1) * HEAD_DIM)
        o1, o2 = _rope_t(qt[hs, :], cos_t, sin_t)
        qt_ref[0, h, :HALF, :] = (o1 * Q_SCALE_LOG2).astype(BF16)
        qt_ref[0, h, HALF:, :] = (o2 * Q_SCALE_LOG2).astype(BF16)
        o1, o2 = _rope_t(qit[hs, :], cos_t, sin_t)
        qit_ref[0, h, :HALF, :] = (o1 * (IDX_DIM ** -0.5)).astype(BF16)
        qit_ref[0, h, HALF:, :] = (o2 * (IDX_DIM ** -0.5)).astype(BF16)
        o1, o2 = _rope_t(kt[hs, :], cos_t, sin_t)
        kt_ref[0, h, :HALF, :] = o1
        kt_ref[0, h, HALF:, :] = o2
        vt_ref[0, h] = vt[hs, :]
        for cc in range(tm // kc):
            vtb_ref[0, h, cc] = vt[hs, cc * kc:(cc + 1) * kc].astype(BF16)
    o1, o2 = _rope_t(proj_t(R_KI, R_WI), cos_t, sin_t)
    kit_ref[0, :HALF, :] = o1
    kit_ref[0, HALF:, :] = o2
    wit_ref[0] = proj_t(R_WI, R_END)[:N_IDX_HEADS, :] * (N_IDX_HEADS ** -0.5)

    qm = (proj(C_QM, C_END) * (HEAD_DIM ** -0.5)).astype(BF16)
    mk = mk_ref[0].astype(BF16)
    mv = mv_ref[0].astype(BF16)
    outs = []
    for h in range(N_MEM_HEADS):
        sl = slice(h * HEAD_DIM, (h + 1) * HEAD_DIM)
        p = _softmax_rows(_dot_nt(qm[:, sl], mk[:, sl]))
        outs.append(_dot(p.astype(BF16), mv[:, sl]))
    mem_ref[0] = jnp.concatenate(outs, axis=1).astype(BF16)


def _inproj(x, wcat, wt, cos, sin, cos_t, sin_t, conv_w, mk, mv, tm, kc):
    b, t, d = x.shape
    n_mem = mk.shape[1]
    grid = (b, t // tm)
    tok = lambda w: pl.BlockSpec((1, tm, w), lambda i, j: (i, j, 0))
    hm_t = pl.BlockSpec((1, N_HEADS, HEAD_DIM, tm), lambda i, j: (i, 0, 0, j))
    const2 = lambda r, c: pl.BlockSpec((r, c), lambda i, j: (0, 0))
    hm_t_shape = lambda dt: jax.ShapeDtypeStruct((b, N_HEADS, HEAD_DIM, t), dt)
    out_shape = [
        jax.ShapeDtypeStruct((b, t, CONV_W), BF16),
        jax.ShapeDtypeStruct((b, t, MEM_W), BF16),
        hm_t_shape(BF16), hm_t_shape(BF16),
        jax.ShapeDtypeStruct((b, N_IDX_HEADS, t), F32),
        hm_t_shape(F32), hm_t_shape(F32),
        jax.ShapeDtypeStruct((b, N_HEADS, t, HEAD_DIM), BF16),
        jax.ShapeDtypeStruct((b, N_HEADS, t // kc, HEAD_DIM, kc), BF16),
        jax.ShapeDtypeStruct((b, IDX_DIM, t), F32),
        jax.ShapeDtypeStruct((b, t, IDX_DIM), BF16),
        jax.ShapeDtypeStruct((b, CONV_K - 1, CONV_W), F32),
    ]
    out_specs = [tok(CONV_W), tok(MEM_W), hm_t, hm_t,
                 pl.BlockSpec((1, N_IDX_HEADS, tm), lambda i, j: (i, 0, j)),
                 hm_t, hm_t,
                 pl.BlockSpec((1, N_HEADS, tm, HEAD_DIM), lambda i, j: (i, 0, j, 0)),
                 pl.BlockSpec((1, N_HEADS, tm // kc, HEAD_DIM, kc), lambda i, j: (i, 0, j, 0, 0)),
                 pl.BlockSpec((1, IDX_DIM, tm), lambda i, j: (i, 0, j)),
                 tok(IDX_DIM),
                 pl.BlockSpec((1, CONV_K - 1, CONV_W), lambda i, j: (i, 0, 0))]
    return pl.pallas_call(
        _inproj_kernel,
        grid=grid,
        in_specs=[tok(d), const2(d, C_END), const2(R_END, d),
                  pl.BlockSpec((tm, LANES), lambda i, j: (j, 0)), pl.BlockSpec((tm, LANES), lambda i, j: (j, 0)),
                  pl.BlockSpec((HALF, tm), lambda i, j: (0, j)), pl.BlockSpec((HALF, tm), lambda i, j: (0, j)),
                  const2(CONV_K, CONV_W),
                  pl.BlockSpec((1, n_mem, MEM_W), lambda i, j: (i, 0, 0)),
                  pl.BlockSpec((1, n_mem, MEM_W), lambda i, j: (i, 0, 0))],
        out_specs=out_specs,
        out_shape=out_shape,
        scratch_shapes=[pltpu.VMEM((8, CONV_W), F32)],
        compiler_params=_params("parallel", "arbitrary"),
        name="inproj",
    )(x, wcat, wt, cos, sin, cos_t, sin_t, conv_w, mk, mv)


def _dsa_kernel(qit_ref, wit_ref, ki_ref, qt_ref, k_ref, vt_ref, o_ref,
                sc_ref, j_ref, m_ref, l_ref, acc_ref, *, topk):
    tq = qt_ref.shape[3]
    kc = sc_ref.shape[1]
    i = pl.program_id(1)
    nk = i + 1
    t0 = i * tq
    wt = wit_ref[0]
    key_pos = lax.broadcasted_iota(I32, (kc, tq), 0)
    q_pos = lax.broadcasted_iota(I32, (kc, tq), 1) + t0

    def keys_of(c):
        return pl.ds(pl.multiple_of(c * kc, kc), kc)

    def score_body(c, carry):
        kic = ki_ref[0, keys_of(c), :]
        acc = jnp.zeros((kc, tq), F32)
        for h in range(N_IDX_HEADS):
            acc = acc + wt[h:h + 1, :] * jnp.maximum(_dot(kic, qit_ref[0, h]), 0.0)
        sc_ref[c] = jnp.where(key_pos + c * kc <= q_pos, acc, -jnp.inf)
        return carry

    lax.fori_loop(0, nk, score_body, 0)

    def fold(v):
        return jnp.sum(v.reshape(kc // SUBLANES, SUBLANES, tq), axis=0)

    def count(pred):
        def body(c, cnt):
            return cnt + fold(jnp.where(pred(sc_ref[c], key_pos + c * kc), 1.0, 0.0))
        cnt = lax.fori_loop(0, nk, body, jnp.zeros((SUBLANES, tq), F32))
        return jnp.sum(cnt, axis=0, keepdims=True)

    def count3(c1, c2, c3):
        def body(c, acc):
            sc = sc_ref[c]
            v = jnp.where(sc >= c3, 1 + (1 << FIELD) + (1 << 2 * FIELD),
                          jnp.where(sc >= c2, 1 + (1 << FIELD), jnp.where(sc >= c1, 1, 0)))
            return acc + fold(v)
        acc = lax.fori_loop(0, nk, body, jnp.zeros((SUBLANES, tq), I32))
        mask = (1 << FIELD) - 1
        return [jnp.sum(f.astype(F32), axis=0, keepdims=True)
                for f in (acc & mask, (acc >> FIELD) & mask, acc >> 2 * FIELD)]

    def bit_body(it, carry):
        thr, cnt_thr = carry
        hi = lax.shift_left(jnp.int32(1), 31 - 2 * it)
        lo = lax.shift_left(jnp.int32(1), 30 - 2 * it)
        cands = (thr ^ lo, thr ^ hi, thr ^ (hi | lo))
        cnts = count3(*[_key_to_float(cj) for cj in cands])
        for cj, nj in zip(cands, cnts):
            ok = nj >= topk
            thr = jnp.where(ok, cj, thr)
            cnt_thr = jnp.where(ok, nj, cnt_thr)
        return thr, cnt_thr

    thr_key, cnt_thr = lax.fori_loop(
        0, 16, bit_body, (jnp.full((1, tq), INT_MIN, I32), jnp.full((1, tq), float(topk), F32)))
    thr = _key_to_float(thr_key)

    j_ref[...] = jnp.full_like(j_ref, BIG_IDX)
    tie_q = jnp.where((thr_key > KEY_OF_NEG_INF) & (cnt_thr > topk), 1.0, 0.0)

    @pl.when(jnp.max(tie_q) > 0.0)
    def _():
        need = topk - count(lambda sc, pos: sc > thr)

        def idx_body(it, jj):
            cand = jj | lax.shift_left(jnp.int32(1), 29 - it)
            below = count(lambda sc, pos: (sc == thr) & (pos < cand))
            return jnp.where(below < need, cand, jj)

        j_ref[...] = lax.fori_loop(0, 30, idx_body, jnp.zeros((1, tq), I32))

    jlast = j_ref[...]

    m_ref[...] = jnp.full_like(m_ref, NEG)
    l_ref[...] = jnp.zeros_like(l_ref)
    acc_ref[...] = jnp.zeros_like(acc_ref)

    def att_body(c, carry):
        sc = sc_ref[c]
        pos = key_pos + c * kc
        sel = ((sc > thr) | ((sc == thr) & (pos <= jlast))) & (pos <= q_pos)
        bias = jnp.where(sel, 0.0, NEG)
        for h0 in range(0, N_HEADS, HEADS_PER_BATCH):
            hs = slice(h0, h0 + HEADS_PER_BATCH)
            s = jnp.einsum('hkd,hdt->hkt', k_ref[0, hs, keys_of(c), :], qt_ref[0, hs],
                           preferred_element_type=F32) + bias[None]
            m_old = m_ref[hs]
            m_new = jnp.maximum(m_old, jnp.max(s, axis=1, keepdims=True))
            alpha = jnp.exp2(m_old - m_new)
            p = jnp.exp2(s - m_new)
            l_ref[hs] = alpha * l_ref[hs] + jnp.sum(p, axis=1, keepdims=True)
            pv = jnp.einsum('hdk,hkt->hdt', vt_ref[0, hs, c], p.astype(BF16), preferred_element_type=F32)
            acc_ref[hs] = alpha * acc_ref[hs] + pv
            m_ref[hs] = m_new
        return carry

    lax.fori_loop(0, nk, att_body, 0)
    for h in range(N_HEADS):
        o_ref[0, :, h * HEAD_DIM:(h + 1) * HEAD_DIM] = (acc_ref[h] / l_ref[h]).T.astype(BF16)


def _dsa(qit, wit, kib, qt, kb, vtb, tq, topk):
    b, _, _, t = qt.shape
    nq = t // tq
    assert tq % LANES == 0 and (tq // SUBLANES) * nq < (1 << FIELD), "per-slot key counts must fit a packed field"
    hm_t = pl.BlockSpec((1, N_HEADS, HEAD_DIM, tq), lambda i, j: (i, 0, 0, j))
    return pl.pallas_call(
        functools.partial(_dsa_kernel, topk=topk),
        grid=(b, nq),
        in_specs=[hm_t,
                  pl.BlockSpec((1, N_IDX_HEADS, tq), lambda i, j: (i, 0, j)),
                  pl.BlockSpec((1, t, IDX_DIM), lambda i, j: (i, 0, 0)),
                  hm_t,
                  pl.BlockSpec((1, N_HEADS, t, HEAD_DIM), lambda i, j: (i, 0, 0, 0)),
                  pl.BlockSpec((1, N_HEADS, nq, HEAD_DIM, tq), lambda i, j: (i, 0, 0, 0, 0))],
        out_specs=pl.BlockSpec((1, tq, ATTN_W), lambda i, j: (i, j, 0)),
        out_shape=jax.ShapeDtypeStruct((b, t, ATTN_W), BF16),
        scratch_shapes=[pltpu.VMEM((nq, tq, tq), F32), pltpu.VMEM((1, tq), I32),
                        pltpu.VMEM((N_HEADS, 1, tq), F32), pltpu.VMEM((N_HEADS, 1, tq), F32),
                        pltpu.VMEM((N_HEADS, HEAD_DIM, tq), F32)],
        compiler_params=_params("parallel", "arbitrary"),
        name="dsa_prompt",
    )(qit, wit, kib, qt, kb, vtb)


def _merge_kernel(x_ref, conv_ref, attn_ref, mem_ref, wg_ref, bg_ref, wc_ref, wa_ref, wm_ref, wo_ref,
                  g1_ref, b1_ref, wrh_ref, wrl_ref, br_ref, hg_ref, *, alpha):
    x = x_ref[...]
    xb = x.astype(BF16)
    d = x.shape[1]
    cw = 2 * LANES
    blocks = []
    for c0 in range(0, d, cw):
        m = None
        for br, (src, wref) in enumerate(((conv_ref, wc_ref), (attn_ref, wa_ref), (mem_ref, wm_ref))):
            gc = slice(br * d + c0, br * d + c0 + cw)
            g = jax.nn.sigmoid(_dot(xb, wg_ref[:, gc]) + bg_ref[:, gc])
            term = g * _dot(src[...], wref[:, c0:c0 + cw])
            m = term if m is None else m + term
        blocks.append(m.astype(BF16))
    mb = jnp.concatenate(blocks, axis=1)
    h = _layernorm(alpha * x + _dot(mb, wo_ref[...]), g1_ref[...], b1_ref[...])
    hg_ref[:, :d] = h

    hh = h.astype(BF16)
    hl = (h - hh.astype(F32)).astype(BF16)
    lo = _dot(hh, wrh_ref[...]) + _dot(hl, wrh_ref[...]) + _dot(hh, wrl_ref[...]) + br_ref[...]
    lane = lax.broadcasted_iota(I32, lo.shape, 1)
    lanef = lane.astype(F32)
    is_g = (lane >= N_EXPERTS) & (lane < N_EXPERTS + N_GROUPS)
    mg = jnp.max(jnp.where(is_g, lo, -jnp.inf), axis=1, keepdims=True)
    sg = jnp.sum(jnp.where(is_g, jnp.exp(lo - mg), 0.0), axis=1, keepdims=True)
    gw = 1.0 / sg
    gsel = jnp.min(jnp.where(is_g & (lo == mg), lanef, 1e9), axis=1, keepdims=True) - N_EXPERTS
    in_g = (lane < N_EXPERTS) & ((lane >> 3).astype(F32) == gsel)
    me = jnp.max(jnp.where(in_g, lo, -jnp.inf), axis=1, keepdims=True)
    ee = jnp.where(in_g, jnp.exp(lo - me), 0.0)
    pe = jnp.where(in_g, ee / jnp.sum(ee, axis=1, keepdims=True), -1.0)
    p1 = jnp.max(pe, axis=1, keepdims=True)
    i1 = jnp.min(jnp.where(pe == p1, lanef, 1e9), axis=1, keepdims=True)
    pe2 = jnp.where(lanef == i1, -1.0, pe)
    p2 = jnp.max(pe2, axis=1, keepdims=True)
    i2 = jnp.min(jnp.where(pe2 == p2, lanef, 1e9), axis=1, keepdims=True)
    nrm = p1 + p2
    gate = gw * jnp.where(lanef == i1, p1 / nrm, jnp.where(lanef == i2, p2 / nrm, 0.0))
    hg_ref[:, d:] = jnp.where(lane == N_EXPERTS, gsel, gate)


def _merge(x, conv, attn, mem, wts, tm, alpha):
    n, d = x.shape
    tok = lambda w: pl.BlockSpec((tm, w), lambda i: (i, 0))
    const = lambda a: pl.BlockSpec(a.shape, lambda i: (0, 0))
    return pl.pallas_call(
        functools.partial(_merge_kernel, alpha=alpha),
        grid=(n // tm,),
        in_specs=[tok(d), tok(CONV_W), tok(ATTN_W), tok(MEM_W)] + [const(a) for a in wts],
        out_specs=tok(d + LANES),
        out_shape=jax.ShapeDtypeStruct((n, d + LANES), F32),
        compiler_params=_params("parallel"),
        name="merge_ln_router",
    )(x, conv, attn, mem, *wts)


def _route_kernel(rec_ref, pos_ref, tg_ref, cnt_ref, start_ref, carry_ref, *, tile_rows):
    phase = pl.program_id(0)
    i = pl.program_id(1)
    tm = rec_ref.shape[0]
    rec = rec_ref[...]
    lane = lax.broadcasted_iota(I32, rec.shape, 1)
    lanef = lane.astype(F32)
    gsel = jnp.sum(jnp.where(lane == N_EXPERTS, rec, 0.0), axis=1, keepdims=True)
    onehot = jnp.where(lanef == gsel, 1.0, 0.0)

    @pl.when((phase == 0) & (i == 0))
    def _():
        cnt_ref[...] = jnp.zeros_like(cnt_ref)

    @pl.when(phase == 0)
    def _():
        cnt_ref[...] += jnp.sum(onehot, axis=0, keepdims=True)

    @pl.when((phase == 1) & (i == 0))
    def _():
        n_tiles = jnp.ceil(cnt_ref[...] * (1.0 / tile_rows))
        n_tiles = jnp.where(lane[:1] < N_GROUPS, n_tiles, 0.0)
        r = lax.broadcasted_iota(I32, (LANES, LANES), 0)
        c = lax.broadcasted_iota(I32, (LANES, LANES), 1)
        before = jnp.where(r < c, 1.0, 0.0).astype(BF16)
        first = _dot(jnp.broadcast_to(n_tiles, (SUBLANES, LANES)).astype(BF16), before)[:1]
        start_ref[...] = first * tile_rows
        carry_ref[...] = jnp.zeros_like(carry_ref)
        tile = lane[:1].astype(F32)
        total = jnp.sum(n_tiles, axis=1, keepdims=True)
        grp = jnp.zeros((1, LANES), F32) - 1.0
        for g in range(N_GROUPS):
            first_g = jnp.sum(jnp.where(lane[:1] == g, first, 0.0), axis=1, keepdims=True)
            n_g = jnp.sum(jnp.where(lane[:1] == g, n_tiles, 0.0), axis=1, keepdims=True)
            grp = jnp.where((tile >= first_g) & (tile < first_g + n_g), float(g), grp)
        tg_ref[...] = jnp.where(tile < total, grp, -1.0).astype(I32)

    @pl.when(phase == 1)
    def _():
        r = lax.broadcasted_iota(I32, (tm, tm), 0)
        c = lax.broadcasted_iota(I32, (tm, tm), 1)
        earlier = jnp.where(c < r, 1.0, 0.0).astype(BF16)
        rank = _dot(earlier, onehot.astype(BF16))
        slot = jnp.sum(jnp.where(lanef == gsel, rank + carry_ref[...] + start_ref[...], 0.0), axis=1, keepdims=True)
        carry_ref[...] += jnp.sum(onehot, axis=0, keepdims=True)
        hi = jnp.floor(slot * (1.0 / 256.0))
        lo = slot - hi * 256.0
        ones = jnp.ones((SUBLANES, LANES), BF16)
        spread = lambda v: _dot_nt(ones, jnp.where(lane == 0, v, 0.0).astype(BF16))[:1]
        pos_ref[0] = (spread(hi) * 256.0 + spread(lo)).astype(I32)


def _route(hg, d, tm, tile_rows):
    n = hg.shape[0]
    nt = n // tm
    pos, tile_group = pl.pallas_call(
        functools.partial(_route_kernel, tile_rows=tile_rows),
        grid=(2, nt),
        in_specs=[pl.BlockSpec((tm, LANES), lambda p, i: (i, d // LANES))],
        out_specs=[pl.BlockSpec((1, 1, tm), lambda p, i: (i * p, 0, 0)),
                   pl.BlockSpec((1, LANES), lambda p, i: (0, 0))],
        out_shape=[jax.ShapeDtypeStruct((nt, 1, tm), I32), jax.ShapeDtypeStruct((1, LANES), I32)],
        scratch_shapes=[pltpu.VMEM((1, LANES), F32), pltpu.VMEM((1, LANES), F32), pltpu.VMEM((1, LANES), F32)],
        compiler_params=_params("arbitrary", "arbitrary"),
        name="moe_route",
    )(hg)
    return pos.reshape(n), tile_group[0]


def _row_copy(src_ref, dst_ref, sem, src_row, dst_row):
    return pltpu.make_async_copy(src_ref.at[pl.ds(src_row, 1)], dst_ref.at[pl.ds(dst_row, 1)], sem)


def _permute_kernel(idx_ref, src_ref, *rest, scatter):
    dst_ref, sem = rest[-2:]
    ch = idx_ref.shape[2]

    def rows(t):
        j = idx_ref[0, 0, t]
        return (t, j) if scatter else (j, t)

    def start(t, carry):
        _row_copy(src_ref, dst_ref, sem, *rows(t)).start()
        return carry

    def wait(t, carry):
        _row_copy(src_ref, dst_ref, sem, *rows(t)).wait()
        return carry

    lax.fori_loop(0, ch, start, 0, unroll=8)
    lax.fori_loop(0, ch, wait, 0, unroll=8)


def _permute_rows(src, idx, n_out, scatter, chunk):
    n_idx = idx.shape[0]
    width = src.shape[1]
    any_space = pl.BlockSpec(memory_space=pl.ANY)
    tile = pl.BlockSpec((chunk, width), lambda i: (i, 0))
    extra = (jnp.zeros((n_out, width), src.dtype),) if scatter else ()
    return pl.pallas_call(
        functools.partial(_permute_kernel, scatter=scatter),
        grid=(n_idx // chunk,),
        in_specs=[pl.BlockSpec((1, 1, chunk), lambda i: (i, 0, 0), memory_space=pltpu.SMEM),
                  tile if scatter else any_space] + [any_space] * len(extra),
        out_specs=any_space if scatter else tile,
        out_shape=jax.ShapeDtypeStruct((n_out, width), src.dtype),
        scratch_shapes=[pltpu.SemaphoreType.DMA],
        input_output_aliases={2: 0} if scatter else {},
        compiler_params=_params("arbitrary"),
        name="permute_rows_scatter" if scatter else "permute_rows_gather",
    )(idx.reshape(n_idx // chunk, 1, chunk), src, *extra)


def _moe_group_kernel(tg_ref, x_ref, wu_ref, wd_ref, g2_ref, b2_ref, o_ref, hb_ref, y_ref, *, alpha):
    i = pl.program_id(0)
    e = pl.program_id(1)
    d = o_ref.shape[1]
    de = wd_ref.shape[1]
    grp = tg_ref[i]

    @pl.when(grp < 0)
    def _():
        o_ref[...] = jnp.zeros_like(o_ref)

    @pl.when(grp >= 0)
    def _():
        @pl.when(e == 0)
        def _():
            hb_ref[...] = x_ref[:, :d].astype(BF16)
            y_ref[...] = jnp.zeros_like(y_ref)

        hu = _dot(hb_ref[...], wu_ref[0])
        a = hu[:, :de]
        act = (a * (1.0 / (1.0 + jnp.exp(-a))) * hu[:, de:]).astype(BF16)
        rec = x_ref[:, d:]
        lane = lax.broadcasted_iota(I32, rec.shape, 1)
        ge = jnp.sum(jnp.where(lane == grp * EXPERTS_PER_GROUP + e, rec, 0.0), axis=1, keepdims=True)
        y_ref[...] += ge * _dot(act, wd_ref[0])

        @pl.when(e == pl.num_programs(1) - 1)
        def _():
            o_ref[...] = _layernorm(alpha * x_ref[:, :d] + y_ref[...], g2_ref[...], b2_ref[...])


def _moe_grouped(xs, tile_group, wu, wd, g2, b2, tm, alpha):
    ns, width = xs.shape
    d = width - LANES
    _, _, du = wu.shape
    de = wd.shape[1]
    expert = lambda i, e, tg: (jnp.maximum(tg[i], 0) * EXPERTS_PER_GROUP + e, 0, 0)
    grid_spec = pltpu.PrefetchScalarGridSpec(
        num_scalar_prefetch=1,
        grid=(ns // tm, EXPERTS_PER_GROUP),
        in_specs=[pl.BlockSpec((tm, width), lambda i, e, tg: (i, 0)),
                  pl.BlockSpec((1, d, du), expert),
                  pl.BlockSpec((1, de, d), expert),
                  pl.BlockSpec((1, d), lambda i, e, tg: (0, 0)),
                  pl.BlockSpec((1, d), lambda i, e, tg: (0, 0))],
        out_specs=pl.BlockSpec((tm, d), lambda i, e, tg: (i, 0)),
        scratch_shapes=[pltpu.VMEM((tm, d), BF16), pltpu.VMEM((tm, d), F32)],
    )
    return pl.pallas_call(
        functools.partial(_moe_group_kernel, alpha=alpha),
        grid_spec=grid_spec,
        out_shape=jax.ShapeDtypeStruct((ns, d), F32),
        compiler_params=_params("arbitrary", "arbitrary"),
        name="moe_grouped_ln",
    )(tile_group, xs, wu, wd, g2, b2)


def _moe_kernel(h_ref, gate_ref, wu_ref, wd_ref, g2_ref, b2_ref, o_ref, hb_ref, y_ref, *, alpha):
    e = pl.program_id(1)
    de = wd_ref.shape[1]

    @pl.when(e == 0)
    def _():
        hb_ref[...] = h_ref[...].astype(BF16)
        y_ref[...] = jnp.zeros_like(y_ref)

    hu = _dot(hb_ref[...], wu_ref[0])
    a = hu[:, :de]
    act = (a * (1.0 / (1.0 + jnp.exp(-a))) * hu[:, de:]).astype(BF16)
    gate = gate_ref[...]
    lane = lax.broadcasted_iota(I32, gate.shape, 1)
    ge = jnp.sum(jnp.where(lane == e, gate, 0.0), axis=1, keepdims=True)
    y_ref[...] += ge * _dot(act, wd_ref[0])

    @pl.when(e == pl.num_programs(1) - 1)
    def _():
        o_ref[...] = _layernorm(alpha * h_ref[...] + y_ref[...], g2_ref[...], b2_ref[...])


def _moe(h, gate, wu, wd, g2, b2, tm, alpha):
    n, d = h.shape
    ne, _, du = wu.shape
    de = wd.shape[1]
    return pl.pallas_call(
        functools.partial(_moe_kernel, alpha=alpha),
        grid=(n // tm, ne),
        in_specs=[pl.BlockSpec((tm, d), lambda i, e: (i, 0)),
                  pl.BlockSpec((tm, ne), lambda i, e: (i, 0)),
                  pl.BlockSpec((1, d, du), lambda i, e: (e, 0, 0)),
                  pl.BlockSpec((1, de, d), lambda i, e: (e, 0, 0)),
                  pl.BlockSpec((1, d), lambda i, e: (0, 0)),
                  pl.BlockSpec((1, d), lambda i, e: (0, 0))],
        out_specs=pl.BlockSpec((tm, d), lambda i, e: (i, 0)),
        out_shape=jax.ShapeDtypeStruct((n, d), F32),
        scratch_shapes=[pltpu.VMEM((tm, d), BF16), pltpu.VMEM((tm, d), F32)],
        compiler_params=_params("parallel", "arbitrary"),
        name="moe_ln",
    )(h, gate, wu, wd, g2, b2)


def _sample_inproj_kernel(x_ref, w_ref, wt_ref, cos_ref, sin_ref, cw_ref, p0_ref, p1_ref,
                          conv_ref, z_ref, q_ref, k_ref, v_ref, qi_ref, ki_ref, wi_ref, qm_ref):
    xb = x_ref[...].astype(BF16)
    cos = cos_ref[...]
    sin = sin_ref[...]

    def proj(a, b):
        return _dot(xb, w_ref[:, a:b])

    def proj_rows(a, b):
        return _dot_nt(xb, wt_ref[a:b, :])

    ucc = proj(C_U, C_K)
    z = ucc[:, C_C:C_K] * ucc[:, C_U:C_B]
    cw = cw_ref[...]
    y = p0_ref[...] * cw[0:1, :] + p1_ref[...] * cw[1:2, :] + z * cw[2:3, :]
    conv_ref[...] = (ucc[:, C_B:C_C] * y).astype(BF16)
    z_ref[...] = z
    q_ref[...] = _rope(proj_rows(R_Q, R_QI), cos, sin) * (HEAD_DIM ** -0.5)
    k_ref[...] = _rope(proj(C_K, C_KI), cos, sin)
    v_ref[...] = proj_rows(R_V, R_KI)
    qi_ref[...] = _rope(proj_rows(R_QI, R_K), cos, sin) * (IDX_DIM ** -0.5)
    ki_ref[...] = _rope(proj(C_KI, C_QM), cos, sin)[:, :IDX_DIM]
    wi_ref[...] = proj_rows(R_WI, R_END)[:, :N_IDX_HEADS] * (N_IDX_HEADS ** -0.5)
    qm_ref[...] = proj(C_QM, C_END) * (HEAD_DIM ** -0.5)


def _sample_inproj(x, wcat, wt, cos, sin, conv_w, p0, p1):
    n, d = x.shape
    args = (x, wcat, wt, cos, sin, conv_w, p0, p1)
    widths = (CONV_W, CONV_W, ATTN_W, ATTN_W, ATTN_W, N_IDX_HEADS * IDX_DIM, IDX_DIM, N_IDX_HEADS, MEM_W)
    dtypes = (BF16,) + (F32,) * 8
    full = lambda shape: pl.BlockSpec(shape, lambda i: (0,) * len(shape))
    return pl.pallas_call(
        _sample_inproj_kernel,
        grid=(1,),
        in_specs=[full(a.shape) for a in args],
        out_specs=[full((n, w)) for w in widths],
        out_shape=[jax.ShapeDtypeStruct((n, w), dt) for w, dt in zip(widths, dtypes)],
        compiler_params=_params("arbitrary"),
        name="sample_inproj",
    )(*args)


def _head_mask(rows, width):
    r = lax.broadcasted_iota(I32, (rows, width), 0)
    c = lax.broadcasted_iota(I32, (rows, width), 1)
    return (c >> 6) == r


def _sample_mem_kernel(qm_ref, mk_ref, mv_ref, o_ref):
    hmask = _head_mask(8, MEM_W)
    qbd = jnp.where(hmask, jnp.broadcast_to(qm_ref[0], (8, MEM_W)), 0.0).astype(BF16)
    p = _softmax_rows(_dot_nt(qbd, mk_ref[0].astype(BF16)))
    o = _dot(p.astype(BF16), mv_ref[0].astype(BF16))
    o_ref[0] = jnp.sum(jnp.where(hmask, o, 0.0), axis=0, keepdims=True).astype(BF16)


def _sample_mem(qm, mk, mv):
    n, n_mem, w = mk.shape
    return pl.pallas_call(
        _sample_mem_kernel,
        grid=(n,),
        in_specs=[pl.BlockSpec((1, 1, w), lambda i: (i, 0, 0)),
                  pl.BlockSpec((1, n_mem, w), lambda i: (i, 0, 0)),
                  pl.BlockSpec((1, n_mem, w), lambda i: (i, 0, 0))],
        out_specs=pl.BlockSpec((1, 1, w), lambda i: (i, 0, 0)),
        out_shape=jax.ShapeDtypeStruct((n, 1, w), BF16),
        compiler_params=_params("parallel"),
        name="sample_mem_attn",
    )(qm, mk, mv)


def _sample_scores_kernel(pt_ref, *refs, pages):
    page_refs = refs[:pages]
    qi_ref, wi_ref, o_ref = refs[pages:]
    keys_t = jnp.concatenate([r[0] for r in page_refs], axis=1).astype(BF16)
    s = _dot(qi_ref[0].astype(BF16), keys_t)
    o_ref[0, 0] = jnp.sum(wi_ref[0] * jnp.maximum(s, 0.0), axis=0, keepdims=True)


def _sample_scores(page_table, cki_t, qi, wi, pages):
    n, n_pages = page_table.shape
    ng = n_pages // pages
    width = pages * PAGE_SIZE

    def page_spec(j):
        return pl.BlockSpec((1, IDX_DIM, PAGE_SIZE), lambda s, pt: (pt[s // ng, (s % ng) * pages + j], 0, 0))

    grid_spec = pltpu.PrefetchScalarGridSpec(
        num_scalar_prefetch=1,
        grid=(n * ng,),
        in_specs=[page_spec(j) for j in range(pages)] + [
            pl.BlockSpec((1, N_IDX_HEADS, IDX_DIM), lambda s, pt: (s // ng, 0, 0)),
            pl.BlockSpec((1, N_IDX_HEADS, 1), lambda s, pt: (s // ng, 0, 0))],
        out_specs=pl.BlockSpec((1, 1, 1, width), lambda s, pt: (s // ng, s % ng, 0, 0)),
    )
    out = pl.pallas_call(
        functools.partial(_sample_scores_kernel, pages=pages),
        grid_spec=grid_spec,
        out_shape=jax.ShapeDtypeStruct((n, ng, 1, width), F32),
        compiler_params=_params("arbitrary"),
        name="sample_idx_scores",
    )(page_table, *([cki_t] * pages), qi, wi)
    return out.reshape(n, n_pages * PAGE_SIZE)


def _sample_select_kernel(sc_ref, qi_ref, ki_ref, wi_ref, bias_ref, bnew_ref, j_ref, *, topk):
    n, past = sc_ref.shape
    qi = qi_ref[...].astype(BF16).astype(F32)
    ki = ki_ref[...].astype(BF16).astype(F32)
    s_new = jnp.maximum(jnp.sum(qi * ki, axis=2), 0.0)
    sc_new = jnp.sum(wi_ref[...] * s_new, axis=1, keepdims=True)
    idx = lax.broadcasted_iota(I32, (n, past), 1)

    def count(pred):
        one = jnp.where(pred(sc_ref[...], idx), 1.0, 0.0)
        return jnp.sum(one, axis=1, keepdims=True) + jnp.where(pred(sc_new, past), 1.0, 0.0)

    def bit_body(it, carry):
        thr, cnt_thr = carry
        cand = thr ^ lax.shift_left(jnp.int32(1), 31 - it)
        cand_f = _key_to_float(cand)
        cnt = count(lambda sc, ii: sc >= cand_f)
        ok = cnt >= topk
        return jnp.where(ok, cand, thr), jnp.where(ok, cnt, cnt_thr)

    thr_key, cnt_thr = lax.fori_loop(
        0, 32, bit_body, (jnp.full((n, 1), INT_MIN, I32), jnp.full((n, 1), float(topk), F32)))
    thr = _key_to_float(thr_key)

    j_ref[...] = jnp.full((n, 1), BIG_IDX, I32)

    @pl.when(jnp.max(jnp.where(cnt_thr > topk, 1.0, 0.0)) > 0.0)
    def _():
        need = topk - count(lambda sc, ii: sc > thr)

        def idx_body(it, jj):
            cand = jj | lax.shift_left(jnp.int32(1), 29 - it)
            below = count(lambda sc, ii: (sc == thr) & (ii < cand))
            return jnp.where(below < need, cand, jj)

        j_ref[...] = lax.fori_loop(0, 30, idx_body, jnp.zeros((n, 1), I32))

    jlast = j_ref[...]
    sc = sc_ref[...]
    bias_ref[...] = jnp.where((sc > thr) | ((sc == thr) & (idx <= jlast)), 0.0, NEG)
    bnew_ref[...] = jnp.where((sc_new > thr) | ((sc_new == thr) & (past <= jlast)), 0.0, NEG)


def _sample_select(scores, qi, ki, wi, topk):
    n, past = scores.shape
    args = (scores, qi, ki, wi)
    full = lambda shape: pl.BlockSpec(shape, lambda i: (0,) * len(shape))
    return pl.pallas_call(
        functools.partial(_sample_select_kernel, topk=topk),
        grid=(1,),
        in_specs=[full(a.shape) for a in args],
        out_specs=[full((n, past)), full((n, 1))],
        out_shape=[jax.ShapeDtypeStruct((n, past), F32), jax.ShapeDtypeStruct((n, 1), F32)],
        scratch_shapes=[pltpu.VMEM((n, 1), I32)],
        compiler_params=_params("arbitrary"),
        name="sample_topk_select",
    )(*args)


def _sample_attn_kernel(pt_ref, ck_ref, cv_ref, q_ref, bias_ref, bnew_ref, kn_ref, vn_ref, o_ref,
                        m_ref, l_ref, acc_ref, kbuf_ref, vbuf_ref, sem, *, pages, nch):
    step = pl.program_id(0)
    n_steps = pl.num_programs(0)
    c = step % nch

    def page_copies(st):
        slot = st % RING_SLOTS
        seq = st // nch
        first = (st % nch) * pages
        out = []
        for j in range(pages):
            page = pt_ref[seq, first + j]
            out.append(pltpu.make_async_copy(ck_ref.at[page], kbuf_ref.at[slot, j], sem.at[slot]))
            out.append(pltpu.make_async_copy(cv_ref.at[page], vbuf_ref.at[slot, j], sem.at[slot]))
        return out

    @pl.when(step == 0)
    def _():
        for st in range(RING_SLOTS - 1):
            for cp in page_copies(st):
                cp.start()

    @pl.when(step + (RING_SLOTS - 1) < n_steps)
    def _():
        for cp in page_copies(step + (RING_SLOTS - 1)):
            cp.start()

    for cp in page_copies(step):
        cp.wait()
    slot = step % RING_SLOTS
    k_refs = [kbuf_ref.at[slot, j] for j in range(pages)]
    v_refs = [vbuf_ref.at[slot, j] for j in range(pages)]

    @pl.when(c == 0)
    def _():
        m_ref[...] = jnp.full_like(m_ref, NEG)
        l_ref[...] = jnp.zeros_like(l_ref)
        acc_ref[...] = jnp.zeros_like(acc_ref)

    rows = []
    for h in range(N_HEADS):
        qh = q_ref[0, h]
        rows.append(jnp.concatenate([jnp.sum(r[h] * qh, axis=0, keepdims=True) for r in k_refs], axis=1))
    s = jnp.concatenate(rows, axis=0) + bias_ref[0, 0]
    m_old = m_ref[...]
    m_new = jnp.maximum(m_old, jnp.max(s, axis=1, keepdims=True))
    alpha = jnp.exp(m_old - m_new)
    p = jnp.exp(s - m_new)
    l_ref[...] = alpha * l_ref[...] + jnp.sum(p, axis=1, keepdims=True)
    m_ref[...] = m_new
    for h in range(N_HEADS):
        acc = alpha[h:h + 1, :] * acc_ref[h]
        for j, r in enumerate(v_refs):
            acc = acc + r[h] * p[h:h + 1, j * PAGE_SIZE:(j + 1) * PAGE_SIZE]
        acc_ref[h] = acc

    @pl.when(c == nch - 1)
    def _():
        bn = bnew_ref[0]
        for h in range(N_HEADS):
            s_n = jnp.sum(q_ref[0, h] * kn_ref[0, h], axis=0, keepdims=True) + bn
            m0 = m_ref[h:h + 1, :]
            m1 = jnp.maximum(m0, s_n)
            a1 = jnp.exp(m0 - m1)
            p_n = jnp.exp(s_n - m1)
            l1 = a1 * l_ref[h:h + 1, :] + p_n
            out = a1 * jnp.sum(acc_ref[h], axis=1, keepdims=True) + p_n * vn_ref[0, h]
            o_ref[0, h] = out / l1


def _sample_attn(page_table, ck_t, cv_t, q, bias, bnew, kn, vn, pages):
    n, n_pages = page_table.shape
    nch = n_pages // pages
    width = pages * PAGE_SIZE

    assert n * nch >= RING_SLOTS
    any_space = pl.BlockSpec(memory_space=pl.ANY)
    page_buf = pltpu.VMEM((RING_SLOTS, pages, N_HEADS, HEAD_DIM, PAGE_SIZE), F32)
    per_seq = pl.BlockSpec((1, N_HEADS, HEAD_DIM, 1), lambda s, pt: (s // nch, 0, 0, 0))
    grid_spec = pltpu.PrefetchScalarGridSpec(
        num_scalar_prefetch=1,
        grid=(n * nch,),
        in_specs=[any_space, any_space,
            per_seq,
            pl.BlockSpec((1, 1, 1, width), lambda s, pt: (s // nch, s % nch, 0, 0)),
            pl.BlockSpec((1, 1, 1), lambda s, pt: (s // nch, 0, 0)),
            per_seq, per_seq],
        out_specs=per_seq,
        scratch_shapes=[pltpu.VMEM((N_HEADS, 1), F32), pltpu.VMEM((N_HEADS, 1), F32),
                        pltpu.VMEM((N_HEADS, HEAD_DIM, PAGE_SIZE), F32),
                        page_buf, page_buf, pltpu.SemaphoreType.DMA((RING_SLOTS,))],
    )
    return pl.pallas_call(
        functools.partial(_sample_attn_kernel, pages=pages, nch=nch),
        grid_spec=grid_spec,
        out_shape=jax.ShapeDtypeStruct((n, N_HEADS, HEAD_DIM, 1), F32),
        compiler_params=_params("arbitrary"),
        name="sample_attn",
    )(page_table, ck_t, cv_t, q, bias.reshape(n, nch, 1, width), bnew, kn, vn)


def _rope_tables(pos):
    inv = ROPE_THETA ** (-jnp.arange(HALF, dtype=F32) / HALF)
    ang = pos.astype(F32)[:, None] * inv[None, :]
    cos = jnp.cos(ang)
    sin = jnp.sin(ang)
    return jnp.concatenate([cos] * 4, axis=1), jnp.concatenate([-sin, sin] * 2, axis=1), cos.T, sin.T


def _largest_divisor(n, cap):
    for c in range(min(cap, n), 0, -1):
        if n % c == 0:
            return c
    return 1


def _pack_layer_weights(l, w_in, b_gate, w_br_conv, w_br_attn, w_br_mem, w_o, ln1_g, ln1_b, w_group, b_group,
                        w_expert, b_expert):
    d = w_in.shape[1]
    sizes = (CONV_W, CONV_W, CONV_W, ATTN_W, ATTN_W, ATTN_W, N_IDX_HEADS * IDX_DIM, IDX_DIM, N_IDX_HEADS, MEM_W,
             N_BRANCH * d)
    offs = [0]
    for s in sizes:
        offs.append(offs[-1] + s)
    col = lambda i: w_in[l][:, offs[i]:offs[i + 1]]
    u, c_b, c_c, q, k, v, qi, ki, wi, qm, gates = (col(i) for i in range(len(sizes)))
    zeros = lambda n: jnp.zeros((d, n), F32)
    wcat = jnp.concatenate([u, c_b, c_c, k, ki, zeros(LANES - IDX_DIM), qm], axis=1).astype(BF16)
    wt = jnp.concatenate([q, qi, k, v, ki, wi, zeros(16 - N_IDX_HEADS)], axis=1).T.astype(BF16)
    wr = jnp.concatenate([w_expert[l], w_group[l], zeros(LANES - N_EXPERTS - N_GROUPS)], axis=1)
    wrh = wr.astype(BF16)
    wrl = (wr - wrh.astype(F32)).astype(BF16)
    br = jnp.concatenate([b_expert[l], b_group[l], jnp.zeros((LANES - N_EXPERTS - N_GROUPS,), F32)])[None, :]
    merge_w = (gates.astype(BF16), b_gate[l][None, :], w_br_conv[l].astype(BF16), w_br_attn[l].astype(BF16),
               w_br_mem[l].astype(BF16), w_o[l].astype(BF16), ln1_g[l][None, :], ln1_b[l][None, :], wrh, wrl, br)
    return wcat, wt, merge_w


def kernel(x_prompt, x_sample, mem_prompt, cache_k, cache_v, cache_k_idx, state_conv, cache_mem_k, cache_mem_v,
           page_table, w_in, b_gate, conv_w, w_br_conv, w_br_attn, w_br_mem, w_o, w_mem_k, w_mem_v, ln1_g, ln1_b,
           w_group, b_group, w_expert, b_expert, w_up, w_down, ln2_g, ln2_b):
    depth = w_in.shape[0]
    b, s_len, d = x_prompt.shape
    bd, t_new, _ = x_sample.shape
    assert t_new == 1, "the sample group decodes one token per sequence"
    n_pages = page_table.shape[1]
    past = n_pages * PAGE_SIZE
    alpha = (2.0 * depth) ** 0.25

    tq = _largest_divisor(s_len, 256)
    tm_in = _largest_divisor(s_len, 512)
    assert tm_in % tq == 0
    n_tok = b * s_len
    tm_merge = _largest_divisor(n_tok, 512)
    tm_moe = _largest_divisor(n_tok, 1024)
    perm_chunk = _largest_divisor(n_tok, 1024)
    assert n_tok // tm_moe + N_GROUPS <= LANES, "the tile -> group map is one lane vector"
    topk_p = min(TOPK_MAX, s_len // 4)
    topk_s = min(TOPK_MAX, (past + t_new) // 4)
    pages_idx = _largest_divisor(n_pages, 32)
    pages_att = _largest_divisor(n_pages, 16)

    cos_p, sin_p, cost_p, sint_p = _rope_tables(jnp.arange(s_len, dtype=jnp.int32))
    cos_s, sin_s, _, _ = _rope_tables(past + jnp.arange(t_new, dtype=jnp.int32))

    hp = x_prompt
    hs = x_sample.reshape(bd, d)
    outs = [[] for _ in range(10)]
    for l in range(depth):
        wcat, wt, merge_w = _pack_layer_weights(l, w_in, b_gate, w_br_conv, w_br_attn, w_br_mem, w_o, ln1_g, ln1_b,
                                                w_group, b_group, w_expert, b_expert)
        wu = w_up[l].astype(BF16)
        wd = w_down[l].astype(BF16)
        g2, b2 = ln2_g[l][None, :], ln2_b[l][None, :]

        mk, mv = _memkv(mem_prompt, w_mem_k[l].astype(BF16), w_mem_v[l].astype(BF16))
        (conv_o, mem_o, qt_b, qit_b, wit_p, kt_p, vt_p, k_b, vt_b, kit_p, ki_b, conv_new) = _inproj(
            hp, wcat, wt, cos_p, sin_p, cost_p, sint_p, conv_w[l], mk, mv, tm_in, tq)
        attn_o = _dsa(qit_b, wit_p, ki_b, qt_b, k_b, vt_b, tq, topk_p)
        hg = _merge(hp.reshape(n_tok, d), conv_o.reshape(n_tok, CONV_W), attn_o.reshape(n_tok, ATTN_W),
                    mem_o.reshape(n_tok, MEM_W), merge_w, tm_merge, alpha)
        slot, tile_group = _route(hg, d, tm_moe, tm_moe)
        n_sorted = n_tok + N_GROUPS * tm_moe
        xs = _permute_rows(hg, slot, n_sorted, True, perm_chunk)
        ys = _moe_grouped(xs, tile_group[:n_sorted // tm_moe], wu, wd, g2, b2, tm_moe, alpha)
        hp = _permute_rows(ys, slot, n_tok, False, perm_chunk).reshape(b, s_len, d)
        for lst, val in zip(outs[:6], (jnp.transpose(kt_p, (0, 3, 1, 2)), jnp.transpose(vt_p, (0, 3, 1, 2)),
                                       jnp.transpose(kit_p, (0, 2, 1)), conv_new,
                                       mk.reshape(b, -1, N_MEM_HEADS, HEAD_DIM),
                                       mv.reshape(b, -1, N_MEM_HEADS, HEAD_DIM))):
            lst.append(val)

        st = state_conv[l].astype(F32)
        (conv_s, z_s, q_s, k_s, v_s, qi_s, ki_s, wi_s, qm_s) = _sample_inproj(
            hs, wcat, wt, cos_s, sin_s, conv_w[l], st[:, 0, :], st[:, 1, :])
        mem_s = _sample_mem(qm_s.reshape(bd, 1, MEM_W), cache_mem_k[l].reshape(bd, -1, MEM_W),
                            cache_mem_v[l].reshape(bd, -1, MEM_W))
        qi3 = qi_s.reshape(bd, N_IDX_HEADS, IDX_DIM)
        wi3 = wi_s.reshape(bd, N_IDX_HEADS, 1)
        scores = _sample_scores(page_table, jnp.transpose(cache_k_idx[l], (0, 2, 1)), qi3, wi3, pages_idx)
        bias, bnew = _sample_select(scores, qi3, ki_s.reshape(bd, 1, IDX_DIM), wi_s, topk_s)
        col4 = lambda a: a.reshape(bd, N_HEADS, HEAD_DIM, 1)
        attn_s = _sample_attn(page_table, jnp.transpose(cache_k[l], (0, 2, 3, 1)),
                              jnp.transpose(cache_v[l], (0, 2, 3, 1)), col4(q_s), bias, bnew.reshape(bd, 1, 1),
                              col4(k_s), col4(v_s), pages_att)
        hgs = _merge(hs, conv_s, attn_s.reshape(bd, ATTN_W).astype(BF16), mem_s.reshape(bd, MEM_W), merge_w,
                     bd, alpha)
        hs = _moe(hgs[:, :d], hgs[:, d:d + N_EXPERTS], wu, wd, g2, b2, bd, alpha)
        for lst, val in zip(outs[6:], (k_s.reshape(bd, t_new, N_HEADS, HEAD_DIM),
                                       v_s.reshape(bd, t_new, N_HEADS, HEAD_DIM), ki_s.reshape(bd, t_new, IDX_DIM),
                                       jnp.stack([st[:, 1, :], z_s], axis=1))):
            lst.append(val)

    return (hp, hs.reshape(bd, t_new, d)) + tuple(jnp.stack(o) for o in outs)
```
